```python
import jax, jax.numpy as jnp
from jax import lax
import numpy as np

D_MODEL = 1024
BATCH = 4
SEQ = 8192
DEPTH = 4

GRID_W = 64
CTX_LEN = 256
EPS = 1e-6
NEG_INF = -1e30

N_MIXERS = 4
MIX_DELTANET = 0
MIX_POOL = 1
MIX_SWA = 2
MIX_CONV = 3
CTX_READING_MIXERS = (MIX_DELTANET, MIX_SWA)

DN_HEADS = 8
DN_HEAD_DIM = D_MODEL // DN_HEADS
DN_DIM = DN_HEADS * DN_HEAD_DIM
DN_CONV = 5
DN_CHUNK = 64

POOL_WINDOWS = (2, 4, 8, 16)
POOL_GROUP = D_MODEL // len(POOL_WINDOWS)

SWA_HEADS = 16
SWA_KV_HEADS = 4
SWA_HEAD_DIM = D_MODEL // SWA_HEADS
SWA_GROUP = SWA_HEADS // SWA_KV_HEADS
SWA_DIM = SWA_HEADS * SWA_HEAD_DIM
SWA_KV_DIM = SWA_KV_HEADS * SWA_HEAD_DIM
SWA_WINDOW = 128
SWA_BLOCK = 128
ROPE_BASE = 10000.0

CNV_WIDTH = 31

FFN_DIM = 2816
FFN_CONV = 3

kernel_name = 'hybrid_interleaved_dit_block'


def _layers_of(kind):
    return len(range(kind, DEPTH, N_MIXERS))


def rms_norm(x, w):
    xf = x.astype(jnp.float32)
    y = xf * lax.rsqrt(jnp.mean(xf * xf, axis=-1, keepdims=True) + EPS)
    return (y * w.astype(jnp.float32)).astype(x.dtype)


def layer_norm(x, w, b):
    xf = x.astype(jnp.float32)
    mu = jnp.mean(xf, axis=-1, keepdims=True)
    xc = xf - mu
    var = jnp.mean(xc * xc, axis=-1, keepdims=True)
    return (xc * lax.rsqrt(var + EPS) * w.astype(jnp.float32) + b.astype(jnp.float32)).astype(x.dtype)


def l2norm(x):
    return x * lax.rsqrt(jnp.sum(x * x, axis=-1, keepdims=True) + EPS)


def dw_conv(x, w):
    pad = w.shape[0] // 2
    return lax.conv_general_dilated(x, w.astype(x.dtype)[:, None, :], window_strides=(1,),
                                    padding=[(pad, pad)], dimension_numbers=('NWC', 'WIO', 'NWC'),
                                    feature_group_count=x.shape[-1])


def _ada(cond, w, b):
    m = jax.nn.silu(cond) @ w + b
    return jnp.split(m, 6, axis=-1)


def axial_rope_tables(n_rows, head_dim):
    n_freq = head_dim // 4
    inv_freq = ROPE_BASE ** (-jnp.arange(n_freq, dtype=jnp.float32) / n_freq)
    rows = jnp.repeat(jnp.arange(n_rows, dtype=jnp.float32), GRID_W)
    cols = jnp.tile(jnp.arange(GRID_W, dtype=jnp.float32), n_rows)
    ang_r = rows[:, None] * inv_freq
    ang_c = cols[:, None] * inv_freq
    ang = jnp.concatenate([ang_r, ang_r, ang_c, ang_c], axis=-1)
    return jnp.cos(ang), jnp.sin(ang)


def apply_axial_rope(x, cos, sin):
    xf = x.astype(jnp.float32)
    x1, x2, x3, x4 = jnp.split(xf, 4, axis=-1)
    rot = jnp.concatenate([-x2, x1, -x4, x3], axis=-1)
    shp = (1, cos.shape[0]) + (1,) * (x.ndim - 3) + (cos.shape[-1],)
    return (xf * cos.reshape(shp) + rot * sin.reshape(shp)).astype(x.dtype)


def gated_delta_chunked(q, k, v, beta, g, s0):
    B, H, T, dk = q.shape
    dv = v.shape[-1]
    C = DN_CHUNK
    N = T // C
    q = q.reshape(B, H, N, C, dk) * (dk ** -0.5)
    k = k.reshape(B, H, N, C, dk)
    v = v.reshape(B, H, N, C, dv)
    beta = beta.reshape(B, H, N, C)
    gc = jnp.cumsum(g.reshape(B, H, N, C), axis=-1)
    idx = jnp.arange(C)
    lower = idx[:, None] >= idx[None, :]
    strict = idx[:, None] > idx[None, :]
    diff = gc[..., :, None] - gc[..., None, :]
    decay_mat = jnp.where(lower, jnp.exp(jnp.where(lower, diff, 0.0)), 0.0)
    kk = jnp.einsum('bhncd,bhnmd->bhncm', k, k)
    t_mat = jnp.where(strict, beta[..., None] * kk * decay_mat, 0.0) + jnp.eye(C, dtype=jnp.float32)
    rhs = jnp.concatenate([v * beta[..., None], k * (beta * jnp.exp(gc))[..., None]], axis=-1)
    sol = lax.linalg.triangular_solve(t_mat, rhs, left_side=True, lower=True, unit_diagonal=True)
    u, w = sol[..., :dv], sol[..., dv:]
    qk = jnp.where(lower, jnp.einsum('bhncd,bhnmd->bhncm', q, k) * decay_mat, 0.0)
    g_last = gc[..., -1]
    k_dec = k * jnp.exp(g_last[..., None] - gc)[..., None]
    q_dec = q * jnp.exp(gc)[..., None]

    def step(S, xs):
        q_c, qk_c, u_c, w_c, k_c, gl_c = xs
        v_new = u_c - jnp.einsum('bhcd,bhde->bhce', w_c, S)
        o = jnp.einsum('bhcd,bhde->bhce', q_c, S) + jnp.einsum('bhcm,bhme->bhce', qk_c, v_new)
        S = S * jnp.exp(gl_c)[..., None, None] + jnp.einsum('bhcd,bhce->bhde', k_c, v_new)
        return S, o

    xs = tuple(jnp.moveaxis(a, 2, 0) for a in (q_dec, qk, u, w, k_dec, g_last))
    s_final, o = lax.scan(step, s0, xs)
    o = jnp.moveaxis(o, 0, 2).reshape(B, H, T, dv)
    return o, s_final


def deltanet_mixer(h_lat, h_ctx, w_in, conv_w, a_log, dt_bias, norm_w, w_out, ctx_out):
    def project(h):
        b_, t_, _ = h.shape
        p = h @ w_in
        qkv = jax.nn.silu(dw_conv(p[..., :3 * DN_DIM], conv_w))
        z = p[..., 3 * DN_DIM:4 * DN_DIM]
        gates = p[..., 4 * DN_DIM:].astype(jnp.float32).reshape(b_, t_, 4, DN_HEADS).transpose(2, 0, 3, 1)
        heads = qkv.astype(jnp.float32).reshape(b_, t_, 3, DN_HEADS, DN_HEAD_DIM).transpose(2, 0, 3, 1, 4)
        q, k, v = l2norm(heads[0]), l2norm(heads[1]), heads[2]
        beta = jax.nn.sigmoid(gates[:2])
        g = -jnp.exp(a_log.astype(jnp.float32))[:, None, :, None] * jax.nn.softplus(
            gates[2:] + dt_bias.astype(jnp.float32)[:, None, :, None])
        return q, k, v, z, beta, g

    def scan_dir(q, k, v, beta, g, s0, backward):
        if backward:
            fl = lambda a: jnp.flip(a, axis=2)
            o, s = gated_delta_chunked(fl(q), fl(k), fl(v), fl(beta), fl(g), s0)
            return fl(o), s
        return gated_delta_chunked(q, k, v, beta, g, s0)

    def readout(o, z):
        b_, _, t_, _ = o.shape
        o = rms_norm(o.transpose(0, 2, 1, 3), norm_w)
        o = o * jax.nn.silu(z.astype(jnp.float32).reshape(b_, t_, DN_HEADS, DN_HEAD_DIM))
        return o.reshape(b_, t_, DN_DIM).astype(z.dtype) @ w_out

    qc, kc, vc, zc, bc, gc = project(h_ctx)
    ql, kl, vl, zl, bl, gl = project(h_lat)
    s0 = jnp.zeros((h_ctx.shape[0], DN_HEADS, DN_HEAD_DIM, DN_HEAD_DIM), jnp.float32)
    oc_f, sc_f = scan_dir(qc, kc, vc, bc[0], gc[0], s0, False)
    oc_b, sc_b = scan_dir(qc, kc, vc, bc[1], gc[1], s0, True)
    ol_f, _ = scan_dir(ql, kl, vl, bl[0], gl[0], sc_f, False)
    ol_b, _ = scan_dir(ql, kl, vl, bl[1], gl[1], sc_b, True)
    y_lat = readout(ol_f + ol_b, zl)
    y_ctx = readout(oc_f + oc_b, zc) if ctx_out else None
    return y_lat, y_ctx


def pool_mixer(h, w_grp, scale):
    T = h.shape[1]
    hf = h.astype(jnp.float32)
    csum = jnp.pad(jnp.cumsum(hf, axis=1), ((0, 0), (1, 0), (0, 0)))
    t = jnp.arange(T)
    outs = []
    for gi, win in enumerate(POOL_WINDOWS):
        sl = slice(gi * POOL_GROUP, (gi + 1) * POOL_GROUP)
        cs = csum[..., sl]
        lo = jnp.clip(t - win // 2, 0, T)
        hi = jnp.clip(t + win - win // 2, 0, T)
        mean = (cs[:, hi] - cs[:, lo]) / (hi - lo).astype(jnp.float32)[None, :, None]
        d = (mean - hf[..., sl]).astype(h.dtype)
        outs.append(d @ w_grp[gi])
    return jnp.concatenate(outs, axis=-1) * scale


def _sink_softmax(sink_logit, *parts):
    lead = parts[0].shape[:-1]
    s = jnp.broadcast_to(sink_logit, lead + (1,))
    return jax.nn.softmax(jnp.concatenate((s,) + parts, axis=-1), axis=-1)


def swa_mixer(h_lat, h_ctx, w_qkv, q_norm_w, k_norm_w, sink, w_out, cos, sin, ctx_out):
    scale = SWA_HEAD_DIM ** -0.5

    def project(h):
        b_, t_, _ = h.shape
        p = h @ w_qkv
        q = p[..., :SWA_DIM].reshape(b_, t_, SWA_KV_HEADS, SWA_GROUP, SWA_HEAD_DIM)
        k = p[..., SWA_DIM:SWA_DIM + SWA_KV_DIM].reshape(b_, t_, SWA_KV_HEADS, SWA_HEAD_DIM)
        v = p[..., SWA_DIM + SWA_KV_DIM:].reshape(b_, t_, SWA_KV_HEADS, SWA_HEAD_DIM)
        return rms_norm(q, q_norm_w), rms_norm(k, k_norm_w), v

    B, T, _ = h_lat.shape
    q_l, k_l, v_l = project(h_lat)
    q_l = apply_axial_rope(q_l, cos, sin)
    k_l = apply_axial_rope(k_l, cos, sin)
    q_c, k_c, v_c = project(h_ctx)
    sink_logit = sink.astype(jnp.float32).reshape(SWA_KV_HEADS, SWA_GROUP)[None, :, :, None, None]

    span = SWA_BLOCK + 2 * SWA_WINDOW
    kp = jnp.pad(k_l, ((0, 0), (SWA_WINDOW, SWA_WINDOW), (0, 0), (0, 0)))
    vp = jnp.pad(v_l, ((0, 0), (SWA_WINDOW, SWA_WINDOW), (0, 0), (0, 0)))
    rel = jnp.arange(span)[None, :] - SWA_WINDOW - jnp.arange(SWA_BLOCK)[:, None]
    band = jnp.abs(rel) <= SWA_WINDOW

    def block(b):
        start = b * SWA_BLOCK
        qb = lax.dynamic_slice_in_dim(q_l, start, SWA_BLOCK, axis=1)
        kb = lax.dynamic_slice_in_dim(kp, start, span, axis=1)
        vb = lax.dynamic_slice_in_dim(vp, start, span, axis=1)
        kpos = start - SWA_WINDOW + jnp.arange(span)
        valid = band & ((kpos >= 0) & (kpos < T))[None, :]
        s_loc = jnp.einsum('bqhgd,bkhd->bhgqk', qb, kb, preferred_element_type=jnp.float32) * scale
        s_loc = jnp.where(valid, s_loc, NEG_INF)
        s_ctx = jnp.einsum('bqhgd,bkhd->bhgqk', qb, k_c, preferred_element_type=jnp.float32) * scale
        p = _sink_softmax(sink_logit, s_loc, s_ctx)
        p_loc = p[..., 1:1 + span].astype(vb.dtype)
        p_ctx = p[..., 1 + span:].astype(v_c.dtype)
        return jnp.einsum('bhgqk,bkhd->bqhgd', p_loc, vb) + jnp.einsum('bhgqk,bkhd->bqhgd', p_ctx, v_c)

    o_l = lax.map(block, jnp.arange(T // SWA_BLOCK))
    y_lat = jnp.moveaxis(o_l, 0, 1).reshape(B, T, SWA_DIM) @ w_out
    y_ctx = None
    if ctx_out:
        L = h_ctx.shape[1]
        s_cc = jnp.einsum('bqhgd,bkhd->bhgqk', q_c, k_c, preferred_element_type=jnp.float32) * scale
        p_cc = _sink_softmax(sink_logit, s_cc)[..., 1:].astype(v_c.dtype)
        y_ctx = jnp.einsum('bhgqk,bkhd->bqhgd', p_cc, v_c).reshape(h_ctx.shape[0], L, SWA_DIM) @ w_out
    return y_lat, y_ctx


def conv_module(h, w_pw1, dw, ln_w, ln_b, w_pw2):
    a = h @ w_pw1
    u = a[..., :D_MODEL] * jax.nn.sigmoid(a[..., D_MODEL:])
    u = jax.nn.silu(layer_norm(dw_conv(u, dw), ln_w, ln_b))
    return u @ w_pw2


def conv_ffn(h, w_gate, w_up, conv_w, conv_b, w_down):
    g = dw_conv(h @ w_gate, conv_w) + conv_b
    return (jax.nn.silu(g) * (h @ w_up)) @ w_down


def setup_inputs(seed: int = 0) -> dict:
    key = jax.random.key(seed)
    ks = iter(jax.random.split(key, 48))
    D = D_MODEL
    nrm = lambda shape, s: jax.random.normal(next(ks), shape, jnp.float32) * s
    n_dn, n_pool, n_swa, n_cnv = (_layers_of(MIX_DELTANET), _layers_of(MIX_POOL),
                                  _layers_of(MIX_SWA), _layers_of(MIX_CONV))
    dn_a_log = jnp.log(jax.random.uniform(next(ks), (n_dn, 2, DN_HEADS), jnp.float32, 1.0, 16.0))
    dt = jnp.exp(jax.random.uniform(next(ks), (n_dn, 2, DN_HEADS), jnp.float32,
                                    float(np.log(1e-3)), float(np.log(1e-1))))
    dn_dt_bias = dt + jnp.log(-jnp.expm1(-dt))
    return {
        'x': nrm((BATCH, SEQ, D), 1.0),
        'c': nrm((BATCH, D), 1.0),
        'ctx': nrm((BATCH, CTX_LEN, D), 1.0),
        'c_ctx': nrm((D,), 1.0),
        'ada_w': nrm((DEPTH, D, 6 * D), 0.5 * D ** -0.5),
        'ada_b': nrm((DEPTH, 6 * D), 0.01),
        'norm_mix_w': 1.0 + nrm((DEPTH, D), 0.02),
        'norm_ffn_w': 1.0 + nrm((DEPTH, D), 0.02),
        'dn_w_in': nrm((n_dn, D, 4 * DN_DIM + 4 * DN_HEADS), D ** -0.5),
        'dn_conv': nrm((n_dn, DN_CONV, 3 * DN_DIM), DN_CONV ** -0.5),
        'dn_a_log': dn_a_log,
        'dn_dt_bias': dn_dt_bias,
        'dn_norm_w': 1.0 + nrm((n_dn, DN_HEAD_DIM), 0.02),
        'dn_w_out': nrm((n_dn, DN_DIM, D), DN_DIM ** -0.5),
        'pool_w_grp': nrm((n_pool, len(POOL_WINDOWS), POOL_GROUP, POOL_GROUP), POOL_GROUP ** -0.5),
        'pool_scale': 1.0 + nrm((n_pool, D), 0.1),
        'swa_w_qkv': nrm((n_swa, D, SWA_DIM + 2 * SWA_KV_DIM), D ** -0.5),
        'swa_q_norm': 1.0 + nrm((n_swa, SWA_HEAD_DIM), 0.02),
        'swa_k_norm': 1.0 + nrm((n_swa, SWA_HEAD_DIM), 0.02),
        'swa_sink': nrm((n_swa, SWA_HEADS), 1.0),
        'swa_w_out': nrm((n_swa, SWA_DIM, D), SWA_DIM ** -0.5),
        'cnv_w_pw1': nrm((n_cnv, D, 2 * D), D ** -0.5),
        'cnv_dw': nrm((n_cnv, CNV_WIDTH, D), CNV_WIDTH ** -0.5),
        'cnv_ln_w': 1.0 + nrm((n_cnv, D), 0.02),
        'cnv_ln_b': nrm((n_cnv, D), 0.01),
        'cnv_w_pw2': nrm((n_cnv, D, D), D ** -0.5),
        'ffn_w_gate': nrm((DEPTH, D, FFN_DIM), D ** -0.5),
        'ffn_w_up': nrm((DEPTH, D, FFN_DIM), D ** -0.5),
        'ffn_conv': nrm((DEPTH, FFN_CONV, FFN_DIM), FFN_CONV ** -0.5),
        'ffn_conv_b': nrm((DEPTH, FFN_DIM), 0.01),
        'ffn_w_down': nrm((DEPTH, FFN_DIM, D), FFN_DIM ** -0.5),
    }


def reference(x, c, ctx, c_ctx, ada_w, ada_b, norm_mix_w, norm_ffn_w,
              dn_w_in, dn_conv, dn_a_log, dn_dt_bias, dn_norm_w, dn_w_out,
              pool_w_grp, pool_scale,
              swa_w_qkv, swa_q_norm, swa_k_norm, swa_sink, swa_w_out,
              cnv_w_pw1, cnv_dw, cnv_ln_w, cnv_ln_b, cnv_w_pw2,
              ffn_w_gate, ffn_w_up, ffn_conv, ffn_conv_b, ffn_w_down):
    h, hc = x, ctx
    n_rows = x.shape[1] // GRID_W
    cos, sin = axial_rope_tables(n_rows, SWA_HEAD_DIM)
    for i in range(DEPTH):
        kind, slot = i % N_MIXERS, i // N_MIXERS
        keep_ctx = any((j % N_MIXERS) in CTX_READING_MIXERS for j in range(i + 1, DEPTH))
        need_ctx_in = keep_ctx or (kind in CTX_READING_MIXERS)
        sh1, sc1, gt1, sh2, sc2, gt2 = [m[:, None, :] for m in _ada(c, ada_w[i], ada_b[i])]
        hin = rms_norm(h, norm_mix_w[i]) * (1.0 + sc1) + sh1
        hcin = None
        if need_ctx_in:
            csh1, csc1, cgt1, csh2, csc2, cgt2 = _ada(c_ctx, ada_w[i], ada_b[i])
            hcin = rms_norm(hc, norm_mix_w[i]) * (1.0 + csc1) + csh1
        if kind == MIX_DELTANET:
            y, yc = deltanet_mixer(hin, hcin, dn_w_in[slot], dn_conv[slot], dn_a_log[slot], dn_dt_bias[slot],
                                   dn_norm_w[slot], dn_w_out[slot], keep_ctx)
        elif kind == MIX_POOL:
            y = pool_mixer(hin, pool_w_grp[slot], pool_scale[slot])
            yc = pool_mixer(hcin, pool_w_grp[slot], pool_scale[slot]) if keep_ctx else None
        elif kind == MIX_SWA:
            y, yc = swa_mixer(hin, hcin, swa_w_qkv[slot], swa_q_norm[slot], swa_k_norm[slot], swa_sink[slot],
                              swa_w_out[slot], cos, sin, keep_ctx)
        else:
            y = conv_module(hin, cnv_w_pw1[slot], cnv_dw[slot], cnv_ln_w[slot], cnv_ln_b[slot], cnv_w_pw2[slot])
            yc = (conv_module(hcin, cnv_w_pw1[slot], cnv_dw[slot], cnv_ln_w[slot], cnv_ln_b[slot], cnv_w_pw2[slot])
                  if keep_ctx else None)
        h = h + gt1 * y
        h = h + gt2 * conv_ffn(rms_norm(h, norm_ffn_w[i]) * (1.0 + sc2) + sh2,
                               ffn_w_gate[i], ffn_w_up[i], ffn_conv[i], ffn_conv_b[i], ffn_w_down[i])
        if keep_ctx:
            hc = hc + cgt1 * yc
            hc = hc + cgt2 * conv_ffn(rms_norm(hc, norm_ffn_w[i]) * (1.0 + csc2) + csh2,
                                      ffn_w_gate[i], ffn_w_up[i], ffn_conv[i], ffn_conv_b[i], ffn_w_down[i])
    return h
```

```python
import functools

import jax
import jax.numpy as jnp
from jax import lax
from jax.experimental import pallas as pl
from jax.experimental.pallas import tpu as pltpu

D_MODEL = 1024
EPS = 1e-6
NEG_INF = -1e30
GRID_W = 64
ROPE_BASE = 10000.0
N_MIXERS = 4
MIX_DELTANET, MIX_POOL, MIX_SWA, MIX_CONV = 0, 1, 2, 3
CTX_READING_MIXERS = (MIX_DELTANET, MIX_SWA)

DN_HEADS = 8
DN_HEAD_DIM = 128
DN_DIM = DN_HEADS * DN_HEAD_DIM
DN_CONV = 5
DN_CHUNK = 128

POOL_WINDOWS = (2, 4, 8, 16)
POOL_GROUP = D_MODEL // len(POOL_WINDOWS)

SWA_HEADS = 16
SWA_KV_HEADS = 4
SWA_HEAD_DIM = 64
SWA_GROUP = SWA_HEADS // SWA_KV_HEADS
SWA_DIM = SWA_HEADS * SWA_HEAD_DIM
SWA_KV_DIM = SWA_KV_HEADS * SWA_HEAD_DIM
SWA_WINDOW = 128
SWA_BLOCK = 128

CNV_WIDTH = 31
FFN_DIM = 2816
FFN_CONV = 3

LANES = 128
SUBLANES = 8
VMEM_LIMIT_BYTES = 56 * 1024 * 1024

F32 = jnp.float32
BF16 = jnp.bfloat16
HALO = 2 * SUBLANES
POOL_HALO = SUBLANES


def _cparams(n_axes=1):
    return pltpu.CompilerParams(
        dimension_semantics=("arbitrary",) * n_axes,
        vmem_limit_bytes=VMEM_LIMIT_BYTES,
    )


def _resident(shape):
    nd = len(shape)
    return pl.BlockSpec(shape, lambda *_: (0,) * nd, pipeline_mode=pl.Buffered(1))


def _halo_specs(tm, hb, n_tiles, width):
    r = tm // hb
    last = n_tiles * r - 1
    prev = pl.BlockSpec((hb, width), lambda i: (jnp.maximum(i * r - 1, 0), 0))
    main = pl.BlockSpec((tm, width), lambda i: (i, 0))
    nxt = pl.BlockSpec((hb, width), lambda i: (jnp.minimum((i + 1) * r, last), 0))
    return prev, main, nxt


def _mod_spec(tiles_per_seq):
    return pl.BlockSpec((None, 6, D_MODEL), lambda i: (i // tiles_per_seq, 0, 0))


def _rms_mod(x, nw, sc, sh):
    ms = jnp.mean(x * x, axis=-1, keepdims=True)
    return (x * lax.rsqrt(ms + EPS) * nw) * (1.0 + sc) + sh


def _fill_hin_ext(hin_ref, xp_ref, x_ref, xn_ref, nw, sc, sh, tiles_per_seq, hb, tm):
    i = pl.program_id(0)
    pos = lax.rem(i, tiles_per_seq)
    first = pos == 0
    last = pos == tiles_per_seq - 1
    hp = jnp.where(first, 0.0, _rms_mod(xp_ref[...], nw, sc, sh))
    hn = jnp.where(last, 0.0, _rms_mod(xn_ref[...], nw, sc, sh))
    hin_ref[0:hb, :] = hp.astype(hin_ref.dtype)
    hin_ref[hb:hb + tm, :] = _rms_mod(x_ref[...], nw, sc, sh).astype(hin_ref.dtype)
    hin_ref[hb + tm:hb + tm + hb, :] = hn.astype(hin_ref.dtype)


def _shift_rows(a, d, n_rows):
    if d == 0:
        return a
    return pltpu.roll(a, (-d) % n_rows, 0)


ADA_TN = 1536


def _ada_kernel(cond_ref, w_ref, b_ref, o_ref):
    s = jax.nn.silu(cond_ref[...])
    o_ref[...] = jnp.dot(s, w_ref[...], preferred_element_type=F32,
                         precision=lax.Precision.HIGHEST) + b_ref[...]


def ada_modulation(cond, ada_w, ada_b):
    depth, d, n = ada_w.shape
    rows = cond.shape[0]
    return pl.pallas_call(
        _ada_kernel,
        out_shape=jax.ShapeDtypeStruct((depth, rows, n), F32),
        grid=(depth, n // ADA_TN),
        in_specs=[
            pl.BlockSpec((rows, d), lambda l, j: (0, 0)),
            pl.BlockSpec((None, d, ADA_TN), lambda l, j: (l, 0, j)),
            pl.BlockSpec((None, 1, ADA_TN), lambda l, j: (l, 0, j)),
        ],
        out_specs=pl.BlockSpec((None, rows, ADA_TN), lambda l, j: (l, 0, j)),
        compiler_params=_cparams(2),
        name="ada_modulation",
    )(cond, ada_w, ada_b.reshape(depth, 1, n))


FFN_FC = 256


def _ffn_kernel(xp_ref, x_ref, xn_ref, mod_ref, nw_ref, wg_ref, wu_ref, cw_ref, cb_ref, wd_ref,
                o_ref, hin_ref, acc_ref, *, tm, tiles_per_seq):
    sh, sc, gt = mod_ref[3:4, :], mod_ref[4:5, :], mod_ref[5:6, :]
    _fill_hin_ext(hin_ref, xp_ref, x_ref, xn_ref, nw_ref[...], sc, sh, tiles_per_seq, HALO, tm)
    n_ext = tm + 2 * HALO
    for c in range(FFN_DIM // FFN_FC):
        cols = slice(c * FFN_FC, (c + 1) * FFN_FC)
        g_ext = jnp.dot(hin_ref[...], wg_ref[:, cols], preferred_element_type=F32)
        u = jnp.dot(hin_ref[HALO:HALO + tm, :], wu_ref[:, cols], preferred_element_type=F32)
        g = cb_ref[:, cols]
        for k in range(FFN_CONV):
            g = g + cw_ref[k:k + 1, cols] * _shift_rows(g_ext, k - 1, n_ext)[HALO:HALO + tm, :]
        act = (jax.nn.silu(g) * u).astype(BF16)
        y = jnp.dot(act, wd_ref[cols, :], preferred_element_type=F32)
        if c == 0:
            acc_ref[...] = y
        else:
            acc_ref[...] += y
    o_ref[...] = x_ref[...] + gt * acc_ref[...]


def conv_ffn(h, mod, nw, wg, wu, cw, cb, wd, *, seq, tm):
    n, d = h.shape
    n_tiles = n // tm
    tps = seq // tm
    prev, main, nxt = _halo_specs(tm, HALO, n_tiles, d)
    return pl.pallas_call(
        functools.partial(_ffn_kernel, tm=tm, tiles_per_seq=tps),
        out_shape=jax.ShapeDtypeStruct((n, d), F32),
        grid=(n_tiles,),
        in_specs=[prev, main, nxt, _mod_spec(tps), _resident((1, d)),
                  _resident(wg.shape), _resident(wu.shape), _resident(cw.shape),
                  _resident(cb.shape), _resident(wd.shape)],
        out_specs=pl.BlockSpec((tm, d), lambda i: (i, 0)),
        scratch_shapes=[pltpu.VMEM((tm + 2 * HALO, d), BF16), pltpu.VMEM((tm, d), F32)],
        compiler_params=_cparams(),
        name="conv_ffn",
    )(h, h, h, mod, nw, wg, wu, cw, cb, wd)


def _mm_res_kernel(a_ref, res_ref, mod_ref, w_ref, o_ref):
    gt = mod_ref[2:3, :]
    o_ref[...] = res_ref[...] + gt * jnp.dot(a_ref[...], w_ref[...], preferred_element_type=F32)


def matmul_residual(a, res, mod, w, *, seq, tm):
    n, k = a.shape
    d = w.shape[1]
    tps = seq // tm
    return pl.pallas_call(
        _mm_res_kernel,
        out_shape=jax.ShapeDtypeStruct((n, d), F32),
        grid=(n // tm,),
        in_specs=[pl.BlockSpec((tm, k), lambda i: (i, 0)), pl.BlockSpec((tm, d), lambda i: (i, 0)),
                  _mod_spec(tps), _resident(w.shape)],
        out_specs=pl.BlockSpec((tm, d), lambda i: (i, 0)),
        compiler_params=_cparams(),
        name="matmul_residual",
    )(a, res, mod, w)


DN_PC = 256


def _dn_proj_kernel(xp_ref, x_ref, xn_ref, mod_ref, nw_ref, wqkv_ref, cw_ref, wz_ref, wgt_ref,
                    aneg_ref, dtb_ref, q_ref, k_ref, v_ref, z_ref, bg_ref, hin_ref,
                    *, tm, tiles_per_seq):
    sh, sc = mod_ref[0:1, :], mod_ref[1:2, :]
    _fill_hin_ext(hin_ref, xp_ref, x_ref, xn_ref, nw_ref[...], sc, sh, tiles_per_seq, HALO, tm)
    n_ext = tm + 2 * HALO
    outs = (q_ref, k_ref, v_ref)
    per_out = DN_DIM // DN_PC
    for c in range(3 * per_out):
        cols = slice(c * DN_PC, (c + 1) * DN_PC)
        p_ext = jnp.dot(hin_ref[...], wqkv_ref[:, cols], preferred_element_type=F32)
        acc = None
        for k in range(DN_CONV):
            term = cw_ref[k:k + 1, cols] * _shift_rows(p_ext, k - DN_CONV // 2, n_ext)[HALO:HALO + tm, :]
            acc = term if acc is None else acc + term
        a = jax.nn.silu(acc)
        which, cc = divmod(c, per_out)
        for hh in range(DN_PC // DN_HEAD_DIM):
            ah = a[:, hh * DN_HEAD_DIM:(hh + 1) * DN_HEAD_DIM]
            if which < 2:
                ah = ah * lax.rsqrt(jnp.sum(ah * ah, axis=-1, keepdims=True) + EPS)
                if which == 0:
                    ah = ah * (DN_HEAD_DIM ** -0.5)
            lo = cc * DN_PC + hh * DN_HEAD_DIM
            outs[which][:, lo:lo + DN_HEAD_DIM] = ah.astype(BF16)
    hm = hin_ref[HALO:HALO + tm, :]
    z_ref[...] = jnp.dot(hm, wz_ref[...], preferred_element_type=F32).astype(BF16)
    gates = jnp.dot(hm, wgt_ref[...], preferred_element_type=F32)
    lane = lax.broadcasted_iota(jnp.int32, gates.shape, 1)
    beta = jax.nn.sigmoid(gates)
    g = aneg_ref[...] * jax.nn.softplus(gates + dtb_ref[...])
    bg_ref[...] = jnp.where(lane < 2 * DN_HEADS, beta, g)


def dn_project(h, mod, nw, wqkv, cw, wz, wgt, aneg, dtb, *, seq, tm):
    n, d = h.shape
    n_tiles = n // tm
    tps = seq // tm
    prev, main, nxt = _halo_specs(tm, HALO, n_tiles, d)
    row = lambda w: pl.BlockSpec((tm, w), lambda i: (i, 0))
    return pl.pallas_call(
        functools.partial(_dn_proj_kernel, tm=tm, tiles_per_seq=tps),
        out_shape=[jax.ShapeDtypeStruct((n, DN_DIM), BF16)] * 4 + [jax.ShapeDtypeStruct((n, LANES), F32)],
        grid=(n_tiles,),
        in_specs=[prev, main, nxt, _mod_spec(tps), _resident((1, d)), _resident(wqkv.shape),
                  _resident(cw.shape), _resident(wz.shape), _resident(wgt.shape),
                  _resident(aneg.shape), _resident(dtb.shape)],
        out_specs=[row(DN_DIM)] * 4 + [row(LANES)],
        scratch_shapes=[pltpu.VMEM((tm + 2 * HALO, d), BF16)],
        compiler_params=_cparams(),
        name="dn_project",
    )(h, h, h, mod, nw, wqkv, cw, wz, wgt, aneg, dtb)


def _bdot(a, b):
    return jnp.dot(a.astype(BF16), b.astype(BF16), preferred_element_type=F32)


def _unit_tri_inverse(lmat, eye, blk):
    c = lmat.shape[0]
    same16 = blk[0]
    n = jnp.where(same16, -lmat, 0.0)
    x = eye + n
    p = n
    for _ in range(3):
        p = _bdot(p, p)
        x = x + _bdot(x, p)
    for lvl in range(1, len(blk)):
        off = jnp.where(jnp.logical_and(blk[lvl], jnp.logical_not(blk[lvl - 1])), lmat, 0.0)
        x = x - _bdot(x, _bdot(off, x))
    del c
    return x


def _dn_scan_kernel(qf_ref, kf_ref, vf_ref, bgf_ref, qb_ref, kb_ref, vb_ref, bgb_ref, s0_ref,
                    of_ref, ob_ref, sfin_ref, s_ref, *, n_chunks):
    n = pl.program_id(1)
    c = DN_CHUNK

    @pl.when(n == 0)
    def _():
        s_ref[...] = s0_ref[...]

    ri = lax.broadcasted_iota(jnp.int32, (c, c), 0)
    ci = lax.broadcasted_iota(jnp.int32, (c, c), 1)
    eye = (ri == ci).astype(F32)
    blk = []
    bs = 16
    while bs <= c:
        blk.append((ri // bs) == (ci // bs))
        bs *= 2

    dirs = ((qf_ref, kf_ref, vf_ref, bgf_ref, of_ref, ri >= ci, ri > ci),
            (qb_ref, kb_ref, vb_ref, bgb_ref, ob_ref, ri <= ci, ri < ci))
    for d, (q_ref, k_ref, v_ref, bg_ref, o_ref, incl, strict) in enumerate(dirs):
        bg = bg_ref[...]
        tri = incl.astype(F32)
        cum = jnp.dot(tri, bg, preferred_element_type=F32, precision=lax.Precision.HIGHEST)
        cum_t = cum.T
        tot = jnp.sum(bg, axis=0, keepdims=True)
        for h in range(DN_HEADS):
            lb = d * DN_HEADS + h
            lg = 2 * DN_HEADS + lb
            hs = slice(h * DN_HEAD_DIM, (h + 1) * DN_HEAD_DIM)
            beta = bg[:, lb:lb + 1]
            gc = cum[:, lg:lg + 1]
            gc_row = cum_t[lg:lg + 1, :]
            gl = tot[:, lg:lg + 1]
            decay = jnp.where(incl, jnp.exp(jnp.where(incl, gc - gc_row, 0.0)), 0.0)
            kh = k_ref[:, hs]
            qh = q_ref[:, hs]
            vh = v_ref[:, hs].astype(F32)
            kq = lax.dot_general(jnp.concatenate([kh, qh], axis=0), kh,
                                 (((1,), (1,)), ((), ())), preferred_element_type=F32)
            kk, qk = kq[:c], kq[c:]
            lmat = jnp.where(strict, beta * kk * decay, 0.0)
            qk = qk * decay
            tinv = _unit_tri_inverse(lmat, eye, blk)
            khf = kh.astype(F32)
            rhs = jnp.concatenate([vh * beta, khf * (beta * jnp.exp(gc))], axis=1)
            uw = _bdot(tinv, rhs)
            u, w = uw[:, :DN_HEAD_DIM], uw[:, DN_HEAD_DIM:]
            q_dec = qh.astype(F32) * jnp.exp(gc)
            k_dec = khf * jnp.exp(gl - gc)
            s = s_ref[d, h]
            wq = _bdot(jnp.concatenate([w, q_dec], axis=0), s)
            v_new = u - wq[:c]
            o = wq[c:] + _bdot(qk, v_new)
            o_ref[:, hs] = o.astype(o_ref.dtype)
            upd = lax.dot_general(k_dec.astype(BF16), v_new.astype(BF16),
                                  (((0,), (0,)), ((), ())), preferred_element_type=F32)
            s_ref[d, h] = s * jnp.exp(gl) + upd

    @pl.when(n == n_chunks - 1)
    def _():
        sfin_ref[...] = s_ref[...]


def dn_scan(q, k, v, bg, s0, *, batch, seq):
    nc = seq // DN_CHUNK
    fwd = lambda w: pl.BlockSpec((DN_CHUNK, w), lambda b, n: (b * nc + n, 0))
    bwd = lambda w: pl.BlockSpec((DN_CHUNK, w), lambda b, n: (b * nc + nc - 1 - n, 0))
    st = pl.BlockSpec((None, 2, DN_HEADS, DN_HEAD_DIM, DN_HEAD_DIM), lambda b, n: (b, 0, 0, 0, 0))
    return pl.pallas_call(
        functools.partial(_dn_scan_kernel, n_chunks=nc),
        out_shape=[jax.ShapeDtypeStruct(q.shape, BF16), jax.ShapeDtypeStruct(q.shape, BF16),
                   jax.ShapeDtypeStruct(s0.shape, F32)],
        grid=(batch, nc),
        in_specs=[fwd(DN_DIM), fwd(DN_DIM), fwd(DN_DIM), fwd(LANES),
                  bwd(DN_DIM), bwd(DN_DIM), bwd(DN_DIM), bwd(LANES), st],
        out_specs=[fwd(DN_DIM), bwd(DN_DIM), st],
        scratch_shapes=[pltpu.VMEM((2, DN_HEADS, DN_HEAD_DIM, DN_HEAD_DIM), F32)],
        compiler_params=_cparams(2),
        name="dn_scan",
    )(q, k, v, bg, q, k, v, bg, s0)


def _dn_readout_kernel(of_ref, ob_ref, z_ref, res_ref, mod_ref, nw_ref, w_ref, o_ref, y_ref):
    gt = mod_ref[2:3, :]
    for h in range(DN_HEADS):
        hs = slice(h * DN_HEAD_DIM, (h + 1) * DN_HEAD_DIM)
        o = of_ref[:, hs].astype(F32) + ob_ref[:, hs].astype(F32)
        ms = jnp.mean(o * o, axis=-1, keepdims=True)
        y = o * lax.rsqrt(ms + EPS) * nw_ref[...]
        y_ref[:, hs] = (y * jax.nn.silu(z_ref[:, hs].astype(F32))).astype(BF16)
    o_ref[...] = res_ref[...] + gt * jnp.dot(y_ref[...], w_ref[...], preferred_element_type=F32)


def dn_readout(o_f, o_b, z, res, mod, nw, w, *, seq, tm):
    n, d = res.shape
    tps = seq // tm
    row = lambda w_: pl.BlockSpec((tm, w_), lambda i: (i, 0))
    return pl.pallas_call(
        _dn_readout_kernel,
        out_shape=jax.ShapeDtypeStruct((n, d), F32),
        grid=(n // tm,),
        in_specs=[row(DN_DIM), row(DN_DIM), row(DN_DIM), row(d), _mod_spec(tps),
                  _resident(nw.shape), _resident(w.shape)],
        out_specs=row(d),
        scratch_shapes=[pltpu.VMEM((tm, DN_DIM), BF16)],
        compiler_params=_cparams(),
        name="dn_readout",
    )(o_f, o_b, z, res, mod, nw, w)


def _pool_kernel(xp_ref, x_ref, xn_ref, mod_ref, nw_ref, wg_ref, scale_ref, o_ref, hin_ref, y_ref,
                 *, tm, tiles_per_seq, seq):
    sh, sc, gt = mod_ref[0:1, :], mod_ref[1:2, :], mod_ref[2:3, :]
    hb = POOL_HALO
    _fill_hin_ext(hin_ref, xp_ref, x_ref, xn_ref, nw_ref[...], sc, sh, tiles_per_seq, hb, tm)
    t0 = lax.rem(pl.program_id(0), tiles_per_seq) * tm
    t = t0 + lax.broadcasted_iota(jnp.int32, (tm, 1), 0)
    for gi, win in enumerate(POOL_WINDOWS):
        cols = slice(gi * POOL_GROUP, (gi + 1) * POOL_GROUP)
        half = win // 2
        s = None
        for dlt in range(-half, win - half):
            term = hin_ref[hb + dlt:hb + dlt + tm, cols]
            s = term if s is None else s + term
        cnt = (jnp.minimum(t + (win - half), seq) - jnp.maximum(t - half, 0)).astype(F32)
        dd = (s / cnt - hin_ref[hb:hb + tm, cols]).astype(BF16)
        y_ref[:, cols] = jnp.dot(dd, wg_ref[gi], preferred_element_type=F32)
    o_ref[...] = x_ref[...] + gt * (y_ref[...] * scale_ref[...])


def pool_mixer(h, mod, nw, wg, scale, *, seq, tm):
    n, d = h.shape
    n_tiles = n // tm
    tps = seq // tm
    prev, main, nxt = _halo_specs(tm, POOL_HALO, n_tiles, d)
    return pl.pallas_call(
        functools.partial(_pool_kernel, tm=tm, tiles_per_seq=tps, seq=seq),
        out_shape=jax.ShapeDtypeStruct((n, d), F32),
        grid=(n_tiles,),
        in_specs=[prev, main, nxt, _mod_spec(tps), _resident((1, d)), _resident(wg.shape),
                  _resident(scale.shape)],
        out_specs=pl.BlockSpec((tm, d), lambda i: (i, 0)),
        scratch_shapes=[pltpu.VMEM((tm + 2 * POOL_HALO, d), F32), pltpu.VMEM((tm, d), F32)],
        compiler_params=_cparams(),
        name="pool_mixer",
    )(h, h, h, mod, nw, wg, scale)


SWA_QKV_COLS = SWA_DIM + 4 * SWA_KV_DIM
SWA_NORM_COLS = SWA_DIM + 2 * SWA_KV_DIM


def _swa_qkv_kernel(x_ref, mod_ref, nw_ref, w_ref, hnw_ref, seg_ref, cos_ref, sin_ref,
                    q_ref, k_ref, v_ref, *, rope):
    sh, sc = mod_ref[0:1, :], mod_ref[1:2, :]
    hin = _rms_mod(x_ref[...], nw_ref[...], sc, sh).astype(BF16)
    n_tiles = SWA_QKV_COLS // LANES
    q_tiles = SWA_DIM // LANES
    k_tiles = 2 * SWA_KV_DIM // LANES
    for j in range(n_tiles):
        cols = slice(j * LANES, (j + 1) * LANES)
        p = jnp.dot(hin, w_ref[:, cols], preferred_element_type=F32)
        if j < q_tiles + k_tiles:
            sq = p * p
            hi = sq.astype(BF16)
            lo = (sq - hi.astype(F32)).astype(BF16)
            ms = (jnp.dot(hi, seg_ref[...], preferred_element_type=F32)
                  + jnp.dot(lo, seg_ref[...], preferred_element_type=F32))
            p = p * lax.rsqrt(ms + EPS) * hnw_ref[:, cols]
            if rope:
                lane = lax.broadcasted_iota(jnp.int32, p.shape, 1)
                partner = jnp.where(lax.rem(lane, 32) < 16, pltpu.roll(p, LANES - 16, 1),
                                    pltpu.roll(p, 16, 1))
                p = p * cos_ref[...] + partner * sin_ref[...]
        if j < q_tiles:
            q_ref[:, cols] = (p * (SWA_HEAD_DIM ** -0.5)).astype(BF16)
        elif j < q_tiles + k_tiles:
            k_ref[:, (j - q_tiles) * LANES:(j - q_tiles + 1) * LANES] = p.astype(BF16)
        else:
            jj = j - q_tiles - k_tiles
            v_ref[:, jj * LANES:(jj + 1) * LANES] = p.astype(BF16)


def swa_qkv(h, mod, nw, w, hnw, seg, cos, sin, *, seq, tm, rope):
    n, d = h.shape
    tps = seq // tm
    row = lambda w_: pl.BlockSpec((tm, w_), lambda i: (i, 0))
    tab = pl.BlockSpec((tm, LANES), lambda i: (i % tps, 0))
    kvw = 2 * SWA_KV_DIM
    return pl.pallas_call(
        functools.partial(_swa_qkv_kernel, rope=rope),
        out_shape=[jax.ShapeDtypeStruct((n, SWA_DIM), BF16), jax.ShapeDtypeStruct((n, kvw), BF16),
                   jax.ShapeDtypeStruct((n, kvw), BF16)],
        grid=(n // tm,),
        in_specs=[row(d), _mod_spec(tps), _resident((1, d)), _resident(w.shape), _resident(hnw.shape),
                  _resident(seg.shape), tab, tab],
        out_specs=[row(SWA_DIM), row(kvw), row(kvw)],
        compiler_params=_cparams(),
        name="swa_qkv",
    )(h, mod, nw, w, hnw, seg, cos, sin)


def _swa_attn_kernel(q_ref, kp_ref, kc_ref, kn_ref, vp_ref, vc_ref, vn_ref, kx_ref, vx_ref, sink_ref,
                     o_ref, *, blocks_per_seq):
    i = pl.program_id(1)
    blk = SWA_BLOCK
    ri = lax.broadcasted_iota(jnp.int32, (SWA_GROUP * blk, blk), 0) % blk
    ci = lax.broadcasted_iota(jnp.int32, (SWA_GROUP * blk, blk), 1)
    ok_prev = jnp.logical_and(ci >= ri, i > 0)
    ok_next = jnp.logical_and(ci <= ri, i < blocks_per_seq - 1)
    lane = lax.broadcasted_iota(jnp.int32, (blk, LANES), 1)
    low = lane < SWA_HEAD_DIM
    nt = (((1,), (1,)), ((), ()))
    for g in range(SWA_KV_HEADS):
        gs = slice(g * LANES, (g + 1) * LANES)
        qs = []
        for t in range(2):
            qt = q_ref[:, (2 * g + t) * LANES:(2 * g + t + 1) * LANES]
            qs.append(jnp.where(low, qt, jnp.zeros_like(qt)))
            qs.append(jnp.where(low, jnp.zeros_like(qt), qt))
        qst = jnp.concatenate(qs, axis=0)
        s_p = jnp.where(ok_prev, lax.dot_general(qst, kp_ref[:, gs], nt, preferred_element_type=F32), NEG_INF)
        s_c = lax.dot_general(qst, kc_ref[:, gs], nt, preferred_element_type=F32)
        s_n = jnp.where(ok_next, lax.dot_general(qst, kn_ref[:, gs], nt, preferred_element_type=F32), NEG_INF)
        s_x = lax.dot_general(qst, kx_ref[:, gs], nt, preferred_element_type=F32)
        sink = jnp.concatenate(
            [jnp.broadcast_to(sink_ref[SWA_GROUP * g + hh:SWA_GROUP * g + hh + 1, 0:1], (blk, 1))
             for hh in range(SWA_GROUP)], axis=0)
        m = jnp.maximum(jnp.maximum(jnp.max(s_p, axis=-1, keepdims=True), jnp.max(s_c, axis=-1, keepdims=True)),
                        jnp.maximum(jnp.max(s_n, axis=-1, keepdims=True), jnp.max(s_x, axis=-1, keepdims=True)))
        m = jnp.maximum(m, sink)
        e_p, e_c, e_n, e_x = (jnp.exp(s - m) for s in (s_p, s_c, s_n, s_x))
        den = (jnp.sum(e_p, axis=-1, keepdims=True) + jnp.sum(e_c, axis=-1, keepdims=True)
               + jnp.sum(e_n, axis=-1, keepdims=True) + jnp.sum(e_x, axis=-1, keepdims=True)
               + jnp.exp(sink - m))
        inv = 1.0 / den
        pv = (jnp.dot((e_p * inv).astype(BF16), vp_ref[:, gs], preferred_element_type=F32)
              + jnp.dot((e_c * inv).astype(BF16), vc_ref[:, gs], preferred_element_type=F32)
              + jnp.dot((e_n * inv).astype(BF16), vn_ref[:, gs], preferred_element_type=F32)
              + jnp.dot((e_x * inv).astype(BF16), vx_ref[:, gs], preferred_element_type=F32))
        for t in range(2):
            a = pv[(2 * t) * blk:(2 * t + 1) * blk]
            b = pv[(2 * t + 1) * blk:(2 * t + 2) * blk]
            o_ref[:, (2 * g + t) * LANES:(2 * g + t + 1) * LANES] = jnp.where(low, a, b).astype(BF16)


def swa_attention(q, k2, v2, kx2, vx2, sink, *, batch, seq, ctx_len):
    nb = seq // SWA_BLOCK
    kvw = 2 * SWA_KV_DIM
    qspec = pl.BlockSpec((SWA_BLOCK, SWA_DIM), lambda b, i: (b * nb + i, 0))
    kv = lambda off: pl.BlockSpec(
        (SWA_BLOCK, kvw), lambda b, i: (b * nb + jnp.clip(i + off, 0, nb - 1), 0))
    xspec = pl.BlockSpec((ctx_len, kvw), lambda b, i: (b, 0))
    return pl.pallas_call(
        functools.partial(_swa_attn_kernel, blocks_per_seq=nb),
        out_shape=jax.ShapeDtypeStruct(q.shape, BF16),
        grid=(batch, nb),
        in_specs=[qspec, kv(-1), kv(0), kv(1), kv(-1), kv(0), kv(1), xspec, xspec, _resident(sink.shape)],
        out_specs=qspec,
        compiler_params=_cparams(2),
        name="swa_attention",
    )(q, k2, k2, k2, v2, v2, v2, kx2, vx2, sink)


CNV_RB = 32
CNV_PC = 256


def _conv_module_kernel(xp_ref, x_ref, xn_ref, mod_ref, nw_ref, w1_ref, dw_ref, lnw_ref, lnb_ref, w2_ref,
                        o_ref, hin_ref, u_ref, s_ref, *, tm, tiles_per_seq):
    sh, sc, gt = mod_ref[0:1, :], mod_ref[1:2, :], mod_ref[2:3, :]
    _fill_hin_ext(hin_ref, xp_ref, x_ref, xn_ref, nw_ref[...], sc, sh, tiles_per_seq, HALO, tm)
    d = D_MODEL
    for c in range(d // CNV_PC):
        cols = slice(c * CNV_PC, (c + 1) * CNV_PC)
        gcols = slice(d + c * CNV_PC, d + (c + 1) * CNV_PC)
        a1 = jnp.dot(hin_ref[...], w1_ref[:, cols], preferred_element_type=F32)
        a2 = jnp.dot(hin_ref[...], w1_ref[:, gcols], preferred_element_type=F32)
        u_ref[:, cols] = a1 * jax.nn.sigmoid(a2)
    pad = CNV_WIDTH // 2
    for rb in range(tm // CNV_RB):
        r0 = HALO + rb * CNV_RB
        acc = None
        for k in range(CNV_WIDTH):
            term = dw_ref[k:k + 1, :] * u_ref[r0 + k - pad:r0 + k - pad + CNV_RB, :]
            acc = term if acc is None else acc + term
        mu = jnp.mean(acc, axis=-1, keepdims=True)
        xc = acc - mu
        var = jnp.mean(xc * xc, axis=-1, keepdims=True)
        y = xc * lax.rsqrt(var + EPS) * lnw_ref[...] + lnb_ref[...]
        s_ref[rb * CNV_RB:(rb + 1) * CNV_RB, :] = jax.nn.silu(y).astype(BF16)
    o_ref[...] = x_ref[...] + gt * jnp.dot(s_ref[...], w2_ref[...], preferred_element_type=F32)


def conv_module(h, mod, nw, w1, dw, lnw, lnb, w2, *, seq, tm):
    n, d = h.shape
    n_tiles = n // tm
    tps = seq // tm
    prev, main, nxt = _halo_specs(tm, HALO, n_tiles, d)
    return pl.pallas_call(
        functools.partial(_conv_module_kernel, tm=tm, tiles_per_seq=tps),
        out_shape=jax.ShapeDtypeStruct((n, d), F32),
        grid=(n_tiles,),
        in_specs=[prev, main, nxt, _mod_spec(tps), _resident((1, d)), _resident(w1.shape),
                  _resident(dw.shape), _resident(lnw.shape), _resident(lnb.shape), _resident(w2.shape)],
        out_specs=pl.BlockSpec((tm, d), lambda i: (i, 0)),
        scratch_shapes=[pltpu.VMEM((tm + 2 * HALO, d), BF16), pltpu.VMEM((tm + 2 * HALO, d), F32),
                        pltpu.VMEM((tm, d), BF16)],
        compiler_params=_cparams(),
        name="conv_module",
    )(h, h, h, mod, nw, w1, dw, lnw, lnb, w2)


def _rope_tables(seq):
    n_freq = SWA_HEAD_DIM // 4
    inv_freq = ROPE_BASE ** (-jnp.arange(n_freq, dtype=F32) / n_freq)
    pos = jnp.arange(seq)
    rows = (pos // GRID_W).astype(F32)
    cols = (pos % GRID_W).astype(F32)
    ang_r = rows[:, None] * inv_freq
    ang_c = cols[:, None] * inv_freq
    ang = jnp.concatenate([ang_r, ang_r, ang_c, ang_c], axis=-1)
    sign = jnp.where((jnp.arange(SWA_HEAD_DIM) % 32) < 16, -1.0, 1.0)
    cos = jnp.tile(jnp.cos(ang), (1, LANES // SWA_HEAD_DIM))
    sin = jnp.tile(jnp.sin(ang) * sign, (1, LANES // SWA_HEAD_DIM))
    return cos, sin


def _tile_rows(seq):
    return min(seq, 512)


def kernel(x, c, ctx, c_ctx, ada_w, ada_b, norm_mix_w, norm_ffn_w, dn_w_in, dn_conv, dn_a_log, dn_dt_bias,
           dn_norm_w, dn_w_out, pool_w_grp, pool_scale, swa_w_qkv, swa_q_norm, swa_k_norm, swa_sink,
           swa_w_out, cnv_w_pw1, cnv_dw, cnv_ln_w, cnv_ln_b, cnv_w_pw2, ffn_w_gate, ffn_w_up, ffn_conv,
           ffn_conv_b, ffn_w_down):
    bsz, seq, d = x.shape
    ctx_len = ctx.shape[1]
    depth = ada_w.shape[0]
    h = x.reshape(bsz * seq, d)
    hc = ctx.reshape(bsz * ctx_len, d)
    p = dict(norm_mix_w=norm_mix_w, norm_ffn_w=norm_ffn_w, dn_w_in=dn_w_in, dn_conv=dn_conv,
             dn_a_log=dn_a_log, dn_dt_bias=dn_dt_bias, dn_norm_w=dn_norm_w, dn_w_out=dn_w_out,
             pool_w_grp=pool_w_grp, pool_scale=pool_scale, swa_w_qkv=swa_w_qkv, swa_q_norm=swa_q_norm,
             swa_k_norm=swa_k_norm, swa_sink=swa_sink, swa_w_out=swa_w_out, cnv_w_pw1=cnv_w_pw1,
             cnv_dw=cnv_dw, cnv_ln_w=cnv_ln_w, cnv_ln_b=cnv_ln_b, cnv_w_pw2=cnv_w_pw2,
             ffn_w_gate=ffn_w_gate, ffn_w_up=ffn_w_up, ffn_conv=ffn_conv, ffn_conv_b=ffn_conv_b,
             ffn_w_down=ffn_w_down)
    dims = dict(bsz=bsz, seq=seq, ctx_len=ctx_len, depth=depth)
    mods = modulation(c, c_ctx, ada_w, ada_b)
    for i in range(depth):
        mod_l, mod_c = layer_modulation(mods, i, bsz)
        h, hc = mixer_layer(i, h, hc, mod_l, mod_c, p, dims)
        h, hc = ffn_layer(i, h, hc, mod_l, mod_c, p, dims)
    return h.reshape(bsz, seq, d)


def modulation(c, c_ctx, ada_w, ada_b):
    bsz, d = c.shape
    n_cond = -(-(bsz + 1) // SUBLANES) * SUBLANES
    cond = jnp.zeros((n_cond, d), F32).at[:bsz].set(c).at[bsz].set(c_ctx)
    return ada_modulation(cond, ada_w, ada_b)


def layer_modulation(mods, i, bsz):
    d = mods.shape[-1] // 6
    mod_l = mods[i, :bsz].reshape(bsz, 6, d)
    mod_c = jnp.broadcast_to(mods[i, bsz].reshape(1, 6, d), (bsz, 6, d))
    return mod_l, mod_c


def _keeps_ctx(i, depth):
    return any((j % N_MIXERS) in CTX_READING_MIXERS for j in range(i + 1, depth))


def ffn_layer(i, h, hc, mod_l, mod_c, p, dims):
    d = h.shape[-1]
    seq, ctx_len = dims["seq"], dims["ctx_len"]
    ffn_args = (p["norm_ffn_w"][i].reshape(1, d), p["ffn_w_gate"][i].astype(BF16),
                p["ffn_w_up"][i].astype(BF16), p["ffn_conv"][i], p["ffn_conv_b"][i].reshape(1, FFN_DIM),
                p["ffn_w_down"][i].astype(BF16))
    h = conv_ffn(h, mod_l, *ffn_args, seq=seq, tm=_tile_rows(seq))
    if _keeps_ctx(i, dims["depth"]):
        hc = conv_ffn(hc, mod_c, *ffn_args, seq=ctx_len, tm=_tile_rows(ctx_len))
    return h, hc


def mixer_layer(i, h, hc, mod_l, mod_c, p, dims):
    d = h.shape[-1]
    bsz, seq, ctx_len = dims["bsz"], dims["seq"], dims["ctx_len"]
    tm_l, tm_c = _tile_rows(seq), _tile_rows(ctx_len)
    norm_mix_w = p["norm_mix_w"]
    dn_w_in, dn_conv, dn_a_log, dn_dt_bias = p["dn_w_in"], p["dn_conv"], p["dn_a_log"], p["dn_dt_bias"]
    dn_norm_w, dn_w_out = p["dn_norm_w"], p["dn_w_out"]
    pool_w_grp, pool_scale = p["pool_w_grp"], p["pool_scale"]
    swa_w_qkv, swa_q_norm, swa_k_norm = p["swa_w_qkv"], p["swa_q_norm"], p["swa_k_norm"]
    swa_sink, swa_w_out = p["swa_sink"], p["swa_w_out"]
    cnv_w_pw1, cnv_dw, cnv_ln_w, cnv_ln_b, cnv_w_pw2 = (p["cnv_w_pw1"], p["cnv_dw"], p["cnv_ln_w"],
                                                       p["cnv_ln_b"], p["cnv_w_pw2"])
    if True:
        kind, slot = i % N_MIXERS, i // N_MIXERS
        keep_ctx = _keeps_ctx(i, dims["depth"])
        nw_mix = norm_mix_w[i].reshape(1, d)
        lat = dict(seq=seq, tm=tm_l)
        cx = dict(seq=ctx_len, tm=tm_c)

        if kind == MIX_DELTANET:
            w_in = dn_w_in[slot]
            wqkv = w_in[:, :3 * DN_DIM].astype(BF16)
            wz = w_in[:, 3 * DN_DIM:4 * DN_DIM].astype(BF16)
            wgt = jnp.zeros((d, LANES), F32).at[:, :4 * DN_HEADS].set(w_in[:, 4 * DN_DIM:]).astype(BF16)
            aneg = jnp.zeros((1, LANES), F32).at[0, 2 * DN_HEADS:4 * DN_HEADS].set(
                -jnp.exp(dn_a_log[slot].astype(F32)).reshape(-1))
            dtb = jnp.zeros((1, LANES), F32).at[0, 2 * DN_HEADS:4 * DN_HEADS].set(
                dn_dt_bias[slot].astype(F32).reshape(-1))
            hnw = jnp.tile(dn_norm_w[slot].reshape(1, DN_HEAD_DIM), (1, 1))
            wout = dn_w_out[slot].astype(BF16)
            proj = functools.partial(dn_project, nw=nw_mix, wqkv=wqkv, cw=dn_conv[slot], wz=wz, wgt=wgt,
                                     aneg=aneg, dtb=dtb)
            qc, kc, vc, zc, bgc = proj(hc, mod_c, **cx)
            ql, kl, vl, zl, bgl = proj(h, mod_l, **lat)
            s0 = jnp.zeros((bsz, 2, DN_HEADS, DN_HEAD_DIM, DN_HEAD_DIM), F32)
            ocf, ocb, s_ctx = dn_scan(qc, kc, vc, bgc, s0, batch=bsz, seq=ctx_len)
            olf, olb, _ = dn_scan(ql, kl, vl, bgl, s_ctx, batch=bsz, seq=seq)
            h = dn_readout(olf, olb, zl, h, mod_l, hnw, wout, **lat)
            if keep_ctx:
                hc = dn_readout(ocf, ocb, zc, hc, mod_c, hnw, wout, **cx)
        elif kind == MIX_POOL:
            wg = pool_w_grp[slot].astype(BF16)
            scale = pool_scale[slot].reshape(1, d)
            h = pool_mixer(h, mod_l, nw_mix, wg, scale, **lat)
            if keep_ctx:
                hc = pool_mixer(hc, mod_c, nw_mix, wg, scale, **cx)
        elif kind == MIX_SWA:
            wq = swa_w_qkv[slot]
            dup = lambda w_: jnp.repeat(w_.reshape(d, SWA_KV_HEADS, 1, SWA_HEAD_DIM), 2, axis=2).reshape(
                d, 2 * SWA_KV_DIM)
            w_all = jnp.concatenate([wq[:, :SWA_DIM], dup(wq[:, SWA_DIM:SWA_DIM + SWA_KV_DIM]),
                                     dup(wq[:, SWA_DIM + SWA_KV_DIM:])], axis=1).astype(BF16)
            hnw = jnp.concatenate([jnp.tile(swa_q_norm[slot], SWA_HEADS),
                                   jnp.tile(swa_k_norm[slot], 2 * SWA_KV_HEADS)]).reshape(1, SWA_NORM_COLS)
            li = jnp.arange(LANES)
            seg = jnp.where((li[:, None] // SWA_HEAD_DIM) == (li[None, :] // SWA_HEAD_DIM),
                            1.0 / SWA_HEAD_DIM, 0.0).astype(BF16)
            cos, sin = _rope_tables(seq)
            sink = jnp.broadcast_to(swa_sink[slot].astype(F32).reshape(SWA_HEADS, 1), (SWA_HEADS, LANES))
            ql, k2, v2 = swa_qkv(h, mod_l, nw_mix, w_all, hnw, seg, cos, sin, rope=True, **lat)
            _, kx2, vx2 = swa_qkv(hc, mod_c, nw_mix, w_all, hnw, seg, cos[:tm_c], sin[:tm_c], rope=False, **cx)
            o = swa_attention(ql, k2, v2, kx2, vx2, sink, batch=bsz, seq=seq, ctx_len=ctx_len)
            h = matmul_residual(o, h, mod_l, swa_w_out[slot].astype(BF16), **lat)
            assert not keep_ctx
        else:
            args = (nw_mix, cnv_w_pw1[slot].astype(BF16), cnv_dw[slot], cnv_ln_w[slot].reshape(1, d),
                    cnv_ln_b[slot].reshape(1, d), cnv_w_pw2[slot].astype(BF16))
            h = conv_module(h, mod_l, *args, **lat)
            if keep_ctx:
                hc = conv_module(hc, mod_c, *args, **cx)
    return h, hc
```

```python
import functools

import jax
import jax.numpy as jnp
from jax import lax
from jax.experimental import pallas as pl
from jax.experimental.pallas import tpu as pltpu

D_MODEL = 1024
EPS = 1e-6
NEG_INF = -1e30
GRID_W = 64
ROPE_BASE = 10000.0
N_MIXERS = 4
MIX_DELTANET, MIX_POOL, MIX_SWA, MIX_CONV = 0, 1, 2, 3
CTX_READING_MIXERS = (MIX_DELTANET, MIX_SWA)

DN_HEADS = 8
DN_HEAD_DIM = 128
DN_DIM = DN_HEADS * DN_HEAD_DIM
DN_CONV = 5
DN_CHUNK = 128

POOL_WINDOWS = (2, 4, 8, 16)
POOL_GROUP = D_MODEL // len(POOL_WINDOWS)

SWA_HEADS = 16
SWA_KV_HEADS = 4
SWA_HEAD_DIM = 64
SWA_GROUP = SWA_HEADS // SWA_KV_HEADS
SWA_DIM = SWA_HEADS * SWA_HEAD_DIM
SWA_KV_DIM = SWA_KV_HEADS * SWA_HEAD_DIM
SWA_WINDOW = 128
SWA_BLOCK = 128

CNV_WIDTH = 31
FFN_DIM = 2816
FFN_CONV = 3

LANES = 128
SUBLANES = 8
VMEM_LIMIT_BYTES = 56 * 1024 * 1024

F32 = jnp.float32
BF16 = jnp.bfloat16
HALO = 2 * SUBLANES
POOL_HALO = SUBLANES


def _cparams(n_axes=1):
    return pltpu.CompilerParams(
        dimension_semantics=("arbitrary",) * n_axes,
        vmem_limit_bytes=VMEM_LIMIT_BYTES,
    )


def _resident(shape):
    nd = len(shape)
    return pl.BlockSpec(shape, lambda *_: (0,) * nd, pipeline_mode=pl.Buffered(1))


def _halo_specs(tm, hb, n_tiles, width):
    r = tm // hb
    last = n_tiles * r - 1
    prev = pl.BlockSpec((hb, width), lambda i: (jnp.maximum(i * r - 1, 0), 0))
    main = pl.BlockSpec((tm, width), lambda i: (i, 0))
    nxt = pl.BlockSpec((hb, width), lambda i: (jnp.minimum((i + 1) * r, last), 0))
    return prev, main, nxt


def _mod_spec(tiles_per_seq):
    return pl.BlockSpec((None, 6, D_MODEL), lambda i: (i // tiles_per_seq, 0, 0))


def _rms_mod(x, nw, sc, sh):
    ms = jnp.mean(x * x, axis=-1, keepdims=True)
    return (x * lax.rsqrt(ms + EPS) * nw) * (1.0 + sc) + sh


def _fill_hin_ext(hin_ref, xp_ref, x_ref, xn_ref, nw, sc, sh, tiles_per_seq, hb, tm):
    i = pl.program_id(0)
    pos = lax.rem(i, tiles_per_seq)
    first = pos == 0
    last = pos == tiles_per_seq - 1
    hp = jnp.where(first, 0.0, _rms_mod(xp_ref[...], nw, sc, sh))
    hn = jnp.where(last, 0.0, _rms_mod(xn_ref[...], nw, sc, sh))
    hin_ref[0:hb, :] = hp.astype(hin_ref.dtype)
    hin_ref[hb:hb + tm, :] = _rms_mod(x_ref[...], nw, sc, sh).astype(hin_ref.dtype)
    hin_ref[hb + tm:hb + tm + hb, :] = hn.astype(hin_ref.dtype)


def _shift_rows(a, d, n_rows):
    if d == 0:
        return a
    return pltpu.roll(a, (-d) % n_rows, 0)


ADA_TN = 1536


def _ada_kernel(cond_ref, w_ref, b_ref, o_ref):
    s = jax.nn.silu(cond_ref[...])
    o_ref[...] = jnp.dot(s, w_ref[...], preferred_element_type=F32,
                         precision=lax.Precision.HIGHEST) + b_ref[...]


def ada_modulation(cond, ada_w, ada_b):
    depth, d, n = ada_w.shape
    rows = cond.shape[0]
    return pl.pallas_call(
        _ada_kernel,
        out_shape=jax.ShapeDtypeStruct((depth, rows, n), F32),
        grid=(depth, n // ADA_TN),
        in_specs=[
            pl.BlockSpec((rows, d), lambda l, j: (0, 0)),
            pl.BlockSpec((None, d, ADA_TN), lambda l, j: (l, 0, j)),
            pl.BlockSpec((None, 1, ADA_TN), lambda l, j: (l, 0, j)),
        ],
        out_specs=pl.BlockSpec((None, rows, ADA_TN), lambda l, j: (l, 0, j)),
        compiler_params=_cparams(2),
        name="ada_modulation",
    )(cond, ada_w, ada_b.reshape(depth, 1, n))


FFN_FC = 256


def _ffn_kernel(xp_ref, x_ref, xn_ref, mod_ref, nw_ref, wg_ref, wu_ref, cw_ref, cb_ref, wd_ref,
                o_ref, hin_ref, acc_ref, *, tm, tiles_per_seq):
    sh, sc, gt = mod_ref[3:4, :], mod_ref[4:5, :], mod_ref[5:6, :]
    _fill_hin_ext(hin_ref, xp_ref, x_ref, xn_ref, nw_ref[...], sc, sh, tiles_per_seq, HALO, tm)
    n_ext = tm + 2 * HALO
    for c in range(FFN_DIM // FFN_FC):
        cols = slice(c * FFN_FC, (c + 1) * FFN_FC)
        g_ext = jnp.dot(hin_ref[...], wg_ref[:, cols], preferred_element_type=F32)
        u = jnp.dot(hin_ref[HALO:HALO + tm, :], wu_ref[:, cols], preferred_element_type=F32)
        g = cb_ref[:, cols]
        for k in range(FFN_CONV):
            g = g + cw_ref[k:k + 1, cols] * _shift_rows(g_ext, k - 1, n_ext)[HALO:HALO + tm, :]
        act = (jax.nn.silu(g) * u).astype(BF16)
        y = jnp.dot(act, wd_ref[cols, :], preferred_element_type=F32)
        if c == 0:
            acc_ref[...] = y
        else:
            acc_ref[...] += y
    o_ref[...] = x_ref[...] + gt * acc_ref[...]


def conv_ffn(h, mod, nw, wg, wu, cw, cb, wd, *, seq, tm):
    n, d = h.shape
    n_tiles = n // tm
    tps = seq // tm
    prev, main, nxt = _halo_specs(tm, HALO, n_tiles, d)
    return pl.pallas_call(
        functools.partial(_ffn_kernel, tm=tm, tiles_per_seq=tps),
        out_shape=jax.ShapeDtypeStruct((n, d), F32),
        grid=(n_tiles,),
        in_specs=[prev, main, nxt, _mod_spec(tps), _resident((1, d)),
                  _resident(wg.shape), _resident(wu.shape), _resident(cw.shape),
                  _resident(cb.shape), _resident(wd.shape)],
        out_specs=pl.BlockSpec((tm, d), lambda i: (i, 0)),
        scratch_shapes=[pltpu.VMEM((tm + 2 * HALO, d), BF16), pltpu.VMEM((tm, d), F32)],
        compiler_params=_cparams(),
        name="conv_ffn",
    )(h, h, h, mod, nw, wg, wu, cw, cb, wd)


def _mm_res_kernel(a_ref, res_ref, mod_ref, w_ref, o_ref):
    gt = mod_ref[2:3, :]
    o_ref[...] = res_ref[...] + gt * jnp.dot(a_ref[...], w_ref[...], preferred_element_type=F32)


def matmul_residual(a, res, mod, w, *, seq, tm):
    n, k = a.shape
    d = w.shape[1]
    tps = seq // tm
    return pl.pallas_call(
        _mm_res_kernel,
        out_shape=jax.ShapeDtypeStruct((n, d), F32),
        grid=(n // tm,),
        in_specs=[pl.BlockSpec((tm, k), lambda i: (i, 0)), pl.BlockSpec((tm, d), lambda i: (i, 0)),
                  _mod_spec(tps), _resident(w.shape)],
        out_specs=pl.BlockSpec((tm, d), lambda i: (i, 0)),
        compiler_params=_cparams(),
        name="matmul_residual",
    )(a, res, mod, w)


DN_PC = 256


def _dn_proj_kernel(xp_ref, x_ref, xn_ref, mod_ref, nw_ref, wqkv_ref, cw_ref, wz_ref, wgt_ref,
                    aneg_ref, dtb_ref, q_ref, k_ref, v_ref, z_ref, bg_ref, hin_ref,
                    *, tm, tiles_per_seq):
    sh, sc = mod_ref[0:1, :], mod_ref[1:2, :]
    _fill_hin_ext(hin_ref, xp_ref, x_ref, xn_ref, nw_ref[...], sc, sh, tiles_per_seq, HALO, tm)
    n_ext = tm + 2 * HALO
    outs = (q_ref, k_ref, v_ref)
    per_out = DN_DIM // DN_PC
    for c in range(3 * per_out):
        cols = slice(c * DN_PC, (c + 1) * DN_PC)
        p_ext = jnp.dot(hin_ref[...], wqkv_ref[:, cols], preferred_element_type=F32)
        acc = None
        for k in range(DN_CONV):
            term = cw_ref[k:k + 1, cols] * _shift_rows(p_ext, k - DN_CONV // 2, n_ext)[HALO:HALO + tm, :]
            acc = term if acc is None else acc + term
        a = jax.nn.silu(acc)
        which, cc = divmod(c, per_out)
        for hh in range(DN_PC // DN_HEAD_DIM):
            ah = a[:, hh * DN_HEAD_DIM:(hh + 1) * DN_HEAD_DIM]
            if which < 2:
                ah = ah * lax.rsqrt(jnp.sum(ah * ah, axis=-1, keepdims=True) + EPS)
                if which == 0:
                    ah = ah * (DN_HEAD_DIM ** -0.5)
            lo = cc * DN_PC + hh * DN_HEAD_DIM
            outs[which][:, lo:lo + DN_HEAD_DIM] = ah.astype(BF16)
    hm = hin_ref[HALO:HALO + tm, :]
    z_ref[...] = jnp.dot(hm, wz_ref[...], preferred_element_type=F32).astype(BF16)
    gates = jnp.dot(hm, wgt_ref[...], preferred_element_type=F32)
    lane = lax.broadcasted_iota(jnp.int32, gates.shape, 1)
    beta = jax.nn.sigmoid(gates)
    g = aneg_ref[...] * jax.nn.softplus(gates + dtb_ref[...])
    bg_ref[...] = jnp.where(lane < 2 * DN_HEADS, beta, g)


def dn_project(h, mod, nw, wqkv, cw, wz, wgt, aneg, dtb, *, seq, tm):
    n, d = h.shape
    n_tiles = n // tm
    tps = seq // tm
    prev, main, nxt = _halo_specs(tm, HALO, n_tiles, d)
    row = lambda w: pl.BlockSpec((tm, w), lambda i: (i, 0))
    return pl.pallas_call(
        functools.partial(_dn_proj_kernel, tm=tm, tiles_per_seq=tps),
        out_shape=[jax.ShapeDtypeStruct((n, DN_DIM), BF16)] * 4 + [jax.ShapeDtypeStruct((n, LANES), F32)],
        grid=(n_tiles,),
        in_specs=[prev, main, nxt, _mod_spec(tps), _resident((1, d)), _resident(wqkv.shape),
                  _resident(cw.shape), _resident(wz.shape), _resident(wgt.shape),
                  _resident(aneg.shape), _resident(dtb.shape)],
        out_specs=[row(DN_DIM)] * 4 + [row(LANES)],
        scratch_shapes=[pltpu.VMEM((tm + 2 * HALO, d), BF16)],
        compiler_params=_cparams(),
        name="dn_project",
    )(h, h, h, mod, nw, wqkv, cw, wz, wgt, aneg, dtb)


def _bdot(a, b):
    return jnp.dot(a.astype(BF16), b.astype(BF16), preferred_element_type=F32)


def _unit_tri_inverses(lmats, eye, blk):
    c = lmats[0].shape[0]
    ns = [jnp.where(blk[0], -l, 0.0) for l in lmats]
    xs = [eye + n for n in ns]
    ps = [_bdot(n, n) for n in ns]
    for _ in range(2):
        rs = [_bdot(jnp.concatenate([x, p], axis=0), p) for x, p in zip(xs, ps)]
        xs = [x + r[:c] for x, r in zip(xs, rs)]
        ps = [r[c:] for r in rs]
    xs = [x + _bdot(x, p) for x, p in zip(xs, ps)]
    for lvl in range(1, len(blk)):
        sel = jnp.logical_and(blk[lvl], jnp.logical_not(blk[lvl - 1]))
        ts = [_bdot(jnp.where(sel, l, 0.0), x) for l, x in zip(lmats, xs)]
        xs = [x - _bdot(x, t) for x, t in zip(xs, ts)]
    return xs


def _dn_scan_kernel(qf_ref, kf_ref, vf_ref, bgf_ref, qb_ref, kb_ref, vb_ref, bgb_ref, s0_ref,
                    of_ref, ob_ref, sfin_ref, s_ref, *, n_chunks):
    n = pl.program_id(1)
    c = DN_CHUNK

    @pl.when(n == 0)
    def _():
        s_ref[...] = s0_ref[...]

    ri = lax.broadcasted_iota(jnp.int32, (c, c), 0)
    ci = lax.broadcasted_iota(jnp.int32, (c, c), 1)
    eye = (ri == ci).astype(F32)
    blk = []
    bs = 16
    while bs <= c:
        blk.append((ri // bs) == (ci // bs))
        bs *= 2

    dirs = ((qf_ref, kf_ref, vf_ref, bgf_ref, of_ref, ri >= ci, ri > ci),
            (qb_ref, kb_ref, vb_ref, bgb_ref, ob_ref, ri <= ci, ri < ci))
    for d, (q_ref, k_ref, v_ref, bg_ref, o_ref, incl, strict) in enumerate(dirs):
        bg = bg_ref[...]
        tri = incl.astype(F32)
        is_g = lax.broadcasted_iota(jnp.int32, bg.shape, 1) >= 2 * DN_HEADS
        bgg = jnp.where(is_g, bg, 0.0)
        cum = jnp.dot(tri, bgg, preferred_element_type=F32, precision=lax.Precision.HIGHEST)
        cum_t = cum.T
        tot = jnp.sum(bgg, axis=0, keepdims=True)
        egc_all = jnp.exp(cum)
        ekd_all = jnp.exp(tot - cum)
        etot = jnp.exp(tot)
        heads = range(DN_HEADS)
        hsl = [slice(h * DN_HEAD_DIM, (h + 1) * DN_HEAD_DIM) for h in heads]
        lbs = [d * DN_HEADS + h for h in heads]
        lgs = [2 * DN_HEADS + lb for lb in lbs]
        betas = [bg[:, lb:lb + 1] for lb in lbs]
        kq = [lax.dot_general(jnp.concatenate([k_ref[:, hs], q_ref[:, hs]], axis=0), k_ref[:, hs],
                              (((1,), (1,)), ((), ())), preferred_element_type=F32) for hs in hsl]
        decays = [jnp.where(incl, jnp.exp(jnp.where(incl, cum[:, lg:lg + 1] - cum_t[lg:lg + 1, :], 0.0)), 0.0)
                  for lg in lgs]
        lmats = [jnp.where(strict, b * r[:c] * dc, 0.0) for b, r, dc in zip(betas, kq, decays)]
        qks = [(r[c:] * dc).astype(BF16) for r, dc in zip(kq, decays)]
        tinvs = _unit_tri_inverses(lmats, eye, blk)
        rhss = [jnp.concatenate([v_ref[:, hs].astype(F32) * b,
                                 k_ref[:, hs].astype(F32) * (b * egc_all[:, lg:lg + 1])], axis=1)
                for hs, b, lg in zip(hsl, betas, lgs)]
        uws = [_bdot(t, r) for t, r in zip(tinvs, rhss)]
        wqs = [_bdot(jnp.concatenate([uw[:, DN_HEAD_DIM:],
                                      q_ref[:, hs].astype(F32) * egc_all[:, lg:lg + 1]], axis=0), s_ref[d, h])
               for uw, hs, lg, h in zip(uws, hsl, lgs, heads)]
        v_news = [(uw[:, :DN_HEAD_DIM] - wq[:c]).astype(BF16) for uw, wq in zip(uws, wqs)]
        outs = [wq[c:] + jnp.dot(qk, vn, preferred_element_type=F32) for wq, qk, vn in zip(wqs, qks, v_news)]
        upds = [lax.dot_general((k_ref[:, hs].astype(F32) * ekd_all[:, lg:lg + 1]).astype(BF16), vn,
                                (((0,), (0,)), ((), ())), preferred_element_type=F32)
                for hs, lg, vn in zip(hsl, lgs, v_news)]
        for h, hs, lg, o, upd in zip(heads, hsl, lgs, outs, upds):
            o_ref[:, hs] = o.astype(o_ref.dtype)
            s_ref[d, h] = s_ref[d, h] * etot[:, lg:lg + 1] + upd

    @pl.when(n == n_chunks - 1)
    def _():
        sfin_ref[...] = s_ref[...]


def dn_scan(q, k, v, bg, s0, *, batch, seq):
    nc = seq // DN_CHUNK
    fwd = lambda w: pl.BlockSpec((DN_CHUNK, w), lambda b, n: (b * nc + n, 0))
    bwd = lambda w: pl.BlockSpec((DN_CHUNK, w), lambda b, n: (b * nc + nc - 1 - n, 0))
    st = pl.BlockSpec((None, 2, DN_HEADS, DN_HEAD_DIM, DN_HEAD_DIM), lambda b, n: (b, 0, 0, 0, 0))
    return pl.pallas_call(
        functools.partial(_dn_scan_kernel, n_chunks=nc),
        out_shape=[jax.ShapeDtypeStruct(q.shape, BF16), jax.ShapeDtypeStruct(q.shape, BF16),
                   jax.ShapeDtypeStruct(s0.shape, F32)],
        grid=(batch, nc),
        in_specs=[fwd(DN_DIM), fwd(DN_DIM), fwd(DN_DIM), fwd(LANES),
                  bwd(DN_DIM), bwd(DN_DIM), bwd(DN_DIM), bwd(LANES), st],
        out_specs=[fwd(DN_DIM), bwd(DN_DIM), st],
        scratch_shapes=[pltpu.VMEM((2, DN_HEADS, DN_HEAD_DIM, DN_HEAD_DIM), F32)],
        compiler_params=_cparams(2),
        name="dn_scan",
    )(q, k, v, bg, q, k, v, bg, s0)


def _dn_readout_kernel(of_ref, ob_ref, z_ref, res_ref, mod_ref, nw_ref, w_ref, o_ref, y_ref):
    gt = mod_ref[2:3, :]
    for h in range(DN_HEADS):
        hs = slice(h * DN_HEAD_DIM, (h + 1) * DN_HEAD_DIM)
        o = of_ref[:, hs].astype(F32) + ob_ref[:, hs].astype(F32)
        ms = jnp.mean(o * o, axis=-1, keepdims=True)
        y = o * lax.rsqrt(ms + EPS) * nw_ref[...]
        y_ref[:, hs] = (y * jax.nn.silu(z_ref[:, hs].astype(F32))).astype(BF16)
    o_ref[...] = res_ref[...] + gt * jnp.dot(y_ref[...], w_ref[...], preferred_element_type=F32)


def dn_readout(o_f, o_b, z, res, mod, nw, w, *, seq, tm):
    n, d = res.shape
    tps = seq // tm
    row = lambda w_: pl.BlockSpec((tm, w_), lambda i: (i, 0))
    return pl.pallas_call(
        _dn_readout_kernel,
        out_shape=jax.ShapeDtypeStruct((n, d), F32),
        grid=(n // tm,),
        in_specs=[row(DN_DIM), row(DN_DIM), row(DN_DIM), row(d), _mod_spec(tps),
                  _resident(nw.shape), _resident(w.shape)],
        out_specs=row(d),
        scratch_shapes=[pltpu.VMEM((tm, DN_DIM), BF16)],
        compiler_params=_cparams(),
        name="dn_readout",
    )(o_f, o_b, z, res, mod, nw, w)


def _pool_kernel(xp_ref, x_ref, xn_ref, mod_ref, nw_ref, wg_ref, scale_ref, o_ref, hin_ref, y_ref,
                 *, tm, tiles_per_seq, seq):
    sh, sc, gt = mod_ref[0:1, :], mod_ref[1:2, :], mod_ref[2:3, :]
    hb = POOL_HALO
    _fill_hin_ext(hin_ref, xp_ref, x_ref, xn_ref, nw_ref[...], sc, sh, tiles_per_seq, hb, tm)
    t0 = lax.rem(pl.program_id(0), tiles_per_seq) * tm
    t = t0 + lax.broadcasted_iota(jnp.int32, (tm, 1), 0)
    for gi, win in enumerate(POOL_WINDOWS):
        cols = slice(gi * POOL_GROUP, (gi + 1) * POOL_GROUP)
        half = win // 2
        s = None
        for dlt in range(-half, win - half):
            term = hin_ref[hb + dlt:hb + dlt + tm, cols]
            s = term if s is None else s + term
        cnt = (jnp.minimum(t + (win - half), seq) - jnp.maximum(t - half, 0)).astype(F32)
        dd = (s / cnt - hin_ref[hb:hb + tm, cols]).astype(BF16)
        y_ref[:, cols] = jnp.dot(dd, wg_ref[gi], preferred_element_type=F32)
    o_ref[...] = x_ref[...] + gt * (y_ref[...] * scale_ref[...])


def pool_mixer(h, mod, nw, wg, scale, *, seq, tm):
    n, d = h.shape
    n_tiles = n // tm
    tps = seq // tm
    prev, main, nxt = _halo_specs(tm, POOL_HALO, n_tiles, d)
    return pl.pallas_call(
        functools.partial(_pool_kernel, tm=tm, tiles_per_seq=tps, seq=seq),
        out_shape=jax.ShapeDtypeStruct((n, d), F32),
        grid=(n_tiles,),
        in_specs=[prev, main, nxt, _mod_spec(tps), _resident((1, d)), _resident(wg.shape),
                  _resident(scale.shape)],
        out_specs=pl.BlockSpec((tm, d), lambda i: (i, 0)),
        scratch_shapes=[pltpu.VMEM((tm + 2 * POOL_HALO, d), F32), pltpu.VMEM((tm, d), F32)],
        compiler_params=_cparams(),
        name="pool_mixer",
    )(h, h, h, mod, nw, wg, scale)


SWA_QKV_COLS = SWA_DIM + 4 * SWA_KV_DIM
SWA_NORM_COLS = SWA_DIM + 2 * SWA_KV_DIM


def _swa_qkv_kernel(x_ref, mod_ref, nw_ref, w_ref, hnw_ref, seg_ref, cos_ref, sin_ref,
                    q_ref, k_ref, v_ref, *, rope):
    sh, sc = mod_ref[0:1, :], mod_ref[1:2, :]
    hin = _rms_mod(x_ref[...], nw_ref[...], sc, sh).astype(BF16)
    n_tiles = SWA_QKV_COLS // LANES
    q_tiles = SWA_DIM // LANES
    k_tiles = 2 * SWA_KV_DIM // LANES
    for j in range(n_tiles):
        cols = slice(j * LANES, (j + 1) * LANES)
        p = jnp.dot(hin, w_ref[:, cols], preferred_element_type=F32)
        if j < q_tiles + k_tiles:
            sq = p * p
            hi = sq.astype(BF16)
            lo = (sq - hi.astype(F32)).astype(BF16)
            ms = (jnp.dot(hi, seg_ref[...], preferred_element_type=F32)
                  + jnp.dot(lo, seg_ref[...], preferred_element_type=F32))
            p = p * lax.rsqrt(ms + EPS) * hnw_ref[:, cols]
            if rope:
                lane = lax.broadcasted_iota(jnp.int32, p.shape, 1)
                partner = jnp.where(lax.rem(lane, 32) < 16, pltpu.roll(p, LANES - 16, 1),
                                    pltpu.roll(p, 16, 1))
                p = p * cos_ref[...] + partner * sin_ref[...]
        if j < q_tiles:
            q_ref[:, cols] = (p * (SWA_HEAD_DIM ** -0.5)).astype(BF16)
        elif j < q_tiles + k_tiles:
            k_ref[:, (j - q_tiles) * LANES:(j - q_tiles + 1) * LANES] = p.astype(BF16)
        else:
            jj = j - q_tiles - k_tiles
            v_ref[:, jj * LANES:(jj + 1) * LANES] = p.astype(BF16)


def swa_qkv(h, mod, nw, w, hnw, seg, cos, sin, *, seq, tm, rope):
    n, d = h.shape
    tps = seq // tm
    row = lambda w_: pl.BlockSpec((tm, w_), lambda i: (i, 0))
    tab = pl.BlockSpec((tm, LANES), lambda i: (i % tps, 0))
    kvw = 2 * SWA_KV_DIM
    return pl.pallas_call(
        functools.partial(_swa_qkv_kernel, rope=rope),
        out_shape=[jax.ShapeDtypeStruct((n, SWA_DIM), BF16), jax.ShapeDtypeStruct((n, kvw), BF16),
                   jax.ShapeDtypeStruct((n, kvw), BF16)],
        grid=(n // tm,),
        in_specs=[row(d), _mod_spec(tps), _resident((1, d)), _resident(w.shape), _resident(hnw.shape),
                  _resident(seg.shape), tab, tab],
        out_specs=[row(SWA_DIM), row(kvw), row(kvw)],
        compiler_params=_cparams(),
        name="swa_qkv",
    )(h, mod, nw, w, hnw, seg, cos, sin)


def _swa_attn_kernel(q_ref, kp_ref, kc_ref, kn_ref, vp_ref, vc_ref, vn_ref, kx_ref, vx_ref, sink_ref,
                     o_ref, *, blocks_per_seq):
    i = pl.program_id(1)
    blk = SWA_BLOCK
    ri = lax.broadcasted_iota(jnp.int32, (SWA_GROUP * blk, blk), 0) % blk
    ci = lax.broadcasted_iota(jnp.int32, (SWA_GROUP * blk, blk), 1)
    ok_prev = jnp.logical_and(ci >= ri, i > 0)
    ok_next = jnp.logical_and(ci <= ri, i < blocks_per_seq - 1)
    lane = lax.broadcasted_iota(jnp.int32, (blk, LANES), 1)
    low = lane < SWA_HEAD_DIM
    nt = (((1,), (1,)), ((), ()))
    for g in range(SWA_KV_HEADS):
        gs = slice(g * LANES, (g + 1) * LANES)
        qs = []
        for t in range(2):
            qt = q_ref[:, (2 * g + t) * LANES:(2 * g + t + 1) * LANES]
            qs.append(jnp.where(low, qt, jnp.zeros_like(qt)))
            qs.append(jnp.where(low, jnp.zeros_like(qt), qt))
        qst = jnp.concatenate(qs, axis=0)
        s_p = jnp.where(ok_prev, lax.dot_general(qst, kp_ref[:, gs], nt, preferred_element_type=F32), NEG_INF)
        s_c = lax.dot_general(qst, kc_ref[:, gs], nt, preferred_element_type=F32)
        s_n = jnp.where(ok_next, lax.dot_general(qst, kn_ref[:, gs], nt, preferred_element_type=F32), NEG_INF)
        s_x = lax.dot_general(qst, kx_ref[:, gs], nt, preferred_element_type=F32)
        sink = jnp.concatenate(
            [jnp.broadcast_to(sink_ref[SWA_GROUP * g + hh:SWA_GROUP * g + hh + 1, 0:1], (blk, 1))
             for hh in range(SWA_GROUP)], axis=0)
        m = jnp.maximum(jnp.maximum(jnp.max(s_p, axis=-1, keepdims=True), jnp.max(s_c, axis=-1, keepdims=True)),
                        jnp.maximum(jnp.max(s_n, axis=-1, keepdims=True), jnp.max(s_x, axis=-1, keepdims=True)))
        m = jnp.maximum(m, sink)
        e_p, e_c, e_n, e_x = (jnp.exp(s - m) for s in (s_p, s_c, s_n, s_x))
        den = (jnp.sum(e_p, axis=-1, keepdims=True) + jnp.sum(e_c, axis=-1, keepdims=True)
               + jnp.sum(e_n, axis=-1, keepdims=True) + jnp.sum(e_x, axis=-1, keepdims=True)
               + jnp.exp(sink - m))
        inv = 1.0 / den
        pv = (jnp.dot((e_p * inv).astype(BF16), vp_ref[:, gs], preferred_element_type=F32)
              + jnp.dot((e_c * inv).astype(BF16), vc_ref[:, gs], preferred_element_type=F32)
              + jnp.dot((e_n * inv).astype(BF16), vn_ref[:, gs], preferred_element_type=F32)
              + jnp.dot((e_x * inv).astype(BF16), vx_ref[:, gs], preferred_element_type=F32))
        for t in range(2):
            a = pv[(2 * t) * blk:(2 * t + 1) * blk]
            b = pv[(2 * t + 1) * blk:(2 * t + 2) * blk]
            o_ref[:, (2 * g + t) * LANES:(2 * g + t + 1) * LANES] = jnp.where(low, a, b).astype(BF16)


def swa_attention(q, k2, v2, kx2, vx2, sink, *, batch, seq, ctx_len):
    nb = seq // SWA_BLOCK
    kvw = 2 * SWA_KV_DIM
    qspec = pl.BlockSpec((SWA_BLOCK, SWA_DIM), lambda b, i: (b * nb + i, 0))
    kv = lambda off: pl.BlockSpec(
        (SWA_BLOCK, kvw), lambda b, i: (b * nb + jnp.clip(i + off, 0, nb - 1), 0))
    xspec = pl.BlockSpec((ctx_len, kvw), lambda b, i: (b, 0))
    return pl.pallas_call(
        functools.partial(_swa_attn_kernel, blocks_per_seq=nb),
        out_shape=jax.ShapeDtypeStruct(q.shape, BF16),
        grid=(batch, nb),
        in_specs=[qspec, kv(-1), kv(0), kv(1), kv(-1), kv(0), kv(1), xspec, xspec, _resident(sink.shape)],
        out_specs=qspec,
        compiler_params=_cparams(2),
        name="swa_attention",
    )(q, k2, k2, k2, v2, v2, v2, kx2, vx2, sink)


CNV_RB = 32
CNV_PC = 256


def _conv_module_kernel(xp_ref, x_ref, xn_ref, mod_ref, nw_ref, w1_ref, dw_ref, lnw_ref, lnb_ref, w2_ref,
                        o_ref, hin_ref, u_ref, s_ref, *, tm, tiles_per_seq):
    sh, sc, gt = mod_ref[0:1, :], mod_ref[1:2, :], mod_ref[2:3, :]
    _fill_hin_ext(hin_ref, xp_ref, x_ref, xn_ref, nw_ref[...], sc, sh, tiles_per_seq, HALO, tm)
    d = D_MODEL
    for c in range(d // CNV_PC):
        cols = slice(c * CNV_PC, (c + 1) * CNV_PC)
        gcols = slice(d + c * CNV_PC, d + (c + 1) * CNV_PC)
        a1 = jnp.dot(hin_ref[...], w1_ref[:, cols], preferred_element_type=F32)
        a2 = jnp.dot(hin_ref[...], w1_ref[:, gcols], preferred_element_type=F32)
        u_ref[:, cols] = a1 * jax.nn.sigmoid(a2)
    pad = CNV_WIDTH // 2
    for rb in range(tm // CNV_RB):
        r0 = HALO + rb * CNV_RB
        acc = None
        for k in range(CNV_WIDTH):
            term = dw_ref[k:k + 1, :] * u_ref[r0 + k - pad:r0 + k - pad + CNV_RB, :]
            acc = term if acc is None else acc + term
        mu = jnp.mean(acc, axis=-1, keepdims=True)
        xc = acc - mu
        var = jnp.mean(xc * xc, axis=-1, keepdims=True)
        y = xc * lax.rsqrt(var + EPS) * lnw_ref[...] + lnb_ref[...]
        s_ref[rb * CNV_RB:(rb + 1) * CNV_RB, :] = jax.nn.silu(y).astype(BF16)
    o_ref[...] = x_ref[...] + gt * jnp.dot(s_ref[...], w2_ref[...], preferred_element_type=F32)


def conv_module(h, mod, nw, w1, dw, lnw, lnb, w2, *, seq, tm):
    n, d = h.shape
    n_tiles = n // tm
    tps = seq // tm
    prev, main, nxt = _halo_specs(tm, HALO, n_tiles, d)
    return pl.pallas_call(
        functools.partial(_conv_module_kernel, tm=tm, tiles_per_seq=tps),
        out_shape=jax.ShapeDtypeStruct((n, d), F32),
        grid=(n_tiles,),
        in_specs=[prev, main, nxt, _mod_spec(tps), _resident((1, d)), _resident(w1.shape),
                  _resident(dw.shape), _resident(lnw.shape), _resident(lnb.shape), _resident(w2.shape)],
        out_specs=pl.BlockSpec((tm, d), lambda i: (i, 0)),
        scratch_shapes=[pltpu.VMEM((tm + 2 * HALO, d), BF16), pltpu.VMEM((tm + 2 * HALO, d), F32),
                        pltpu.VMEM((tm, d), BF16)],
        compiler_params=_cparams(),
        name="conv_module",
    )(h, h, h, mod, nw, w1, dw, lnw, lnb, w2)


def _rope_tables(seq):
    n_freq = SWA_HEAD_DIM // 4
    inv_freq = ROPE_BASE ** (-jnp.arange(n_freq, dtype=F32) / n_freq)
    pos = jnp.arange(seq)
    rows = (pos // GRID_W).astype(F32)
    cols = (pos % GRID_W).astype(F32)
    ang_r = rows[:, None] * inv_freq
    ang_c = cols[:, None] * inv_freq
    ang = jnp.concatenate([ang_r, ang_r, ang_c, ang_c], axis=-1)
    sign = jnp.where((jnp.arange(SWA_HEAD_DIM) % 32) < 16, -1.0, 1.0)
    cos = jnp.tile(jnp.cos(ang), (1, LANES // SWA_HEAD_DIM))
    sin = jnp.tile(jnp.sin(ang) * sign, (1, LANES // SWA_HEAD_DIM))
    return cos, sin


def _tile_rows(seq):
    return min(seq, 512)


def kernel(x, c, ctx, c_ctx, ada_w, ada_b, norm_mix_w, norm_ffn_w, dn_w_in, dn_conv, dn_a_log, dn_dt_bias,
           dn_norm_w, dn_w_out, pool_w_grp, pool_scale, swa_w_qkv, swa_q_norm, swa_k_norm, swa_sink,
           swa_w_out, cnv_w_pw1, cnv_dw, cnv_ln_w, cnv_ln_b, cnv_w_pw2, ffn_w_gate, ffn_w_up, ffn_conv,
           ffn_conv_b, ffn_w_down):
    bsz, seq, d = x.shape
    ctx_len = ctx.shape[1]
    depth = ada_w.shape[0]
    h = x.reshape(bsz * seq, d)
    hc = ctx.reshape(bsz * ctx_len, d)
    p = dict(norm_mix_w=norm_mix_w, norm_ffn_w=norm_ffn_w, dn_w_in=dn_w_in, dn_conv=dn_conv,
             dn_a_log=dn_a_log, dn_dt_bias=dn_dt_bias, dn_norm_w=dn_norm_w, dn_w_out=dn_w_out,
             pool_w_grp=pool_w_grp, pool_scale=pool_scale, swa_w_qkv=swa_w_qkv, swa_q_norm=swa_q_norm,
             swa_k_norm=swa_k_norm, swa_sink=swa_sink, swa_w_out=swa_w_out, cnv_w_pw1=cnv_w_pw1,
             cnv_dw=cnv_dw, cnv_ln_w=cnv_ln_w, cnv_ln_b=cnv_ln_b, cnv_w_pw2=cnv_w_pw2,
             ffn_w_gate=ffn_w_gate, ffn_w_up=ffn_w_up, ffn_conv=ffn_conv, ffn_conv_b=ffn_conv_b,
             ffn_w_down=ffn_w_down)
    dims = dict(bsz=bsz, seq=seq, ctx_len=ctx_len, depth=depth)
    mods = modulation(c, c_ctx, ada_w, ada_b)
    for i in range(depth):
        mod_l, mod_c = layer_modulation(mods, i, bsz)
        h, hc = mixer_layer(i, h, hc, mod_l, mod_c, p, dims)
        h, hc = ffn_layer(i, h, hc, mod_l, mod_c, p, dims)
    return h.reshape(bsz, seq, d)


def modulation(c, c_ctx, ada_w, ada_b):
    bsz, d = c.shape
    n_cond = -(-(bsz + 1) // SUBLANES) * SUBLANES
    cond = jnp.zeros((n_cond, d), F32).at[:bsz].set(c).at[bsz].set(c_ctx)
    return ada_modulation(cond, ada_w, ada_b)


def layer_modulation(mods, i, bsz):
    d = mods.shape[-1] // 6
    mod_l = mods[i, :bsz].reshape(bsz, 6, d)
    mod_c = jnp.broadcast_to(mods[i, bsz].reshape(1, 6, d), (bsz, 6, d))
    return mod_l, mod_c


def _keeps_ctx(i, depth):
    return any((j % N_MIXERS) in CTX_READING_MIXERS for j in range(i + 1, depth))


def ffn_layer(i, h, hc, mod_l, mod_c, p, dims):
    d = h.shape[-1]
    seq, ctx_len = dims["seq"], dims["ctx_len"]
    ffn_args = (p["norm_ffn_w"][i].reshape(1, d), p["ffn_w_gate"][i].astype(BF16),
                p["ffn_w_up"][i].astype(BF16), p["ffn_conv"][i], p["ffn_conv_b"][i].reshape(1, FFN_DIM),
                p["ffn_w_down"][i].astype(BF16))
    h = conv_ffn(h, mod_l, *ffn_args, seq=seq, tm=_tile_rows(seq))
    if _keeps_ctx(i, dims["depth"]):
        hc = conv_ffn(hc, mod_c, *ffn_args, seq=ctx_len, tm=_tile_rows(ctx_len))
    return h, hc


def mixer_layer(i, h, hc, mod_l, mod_c, p, dims):
    d = h.shape[-1]
    bsz, seq, ctx_len = dims["bsz"], dims["seq"], dims["ctx_len"]
    tm_l, tm_c = _tile_rows(seq), _tile_rows(ctx_len)
    norm_mix_w = p["norm_mix_w"]
    dn_w_in, dn_conv, dn_a_log, dn_dt_bias = p["dn_w_in"], p["dn_conv"], p["dn_a_log"], p["dn_dt_bias"]
    dn_norm_w, dn_w_out = p["dn_norm_w"], p["dn_w_out"]
    pool_w_grp, pool_scale = p["pool_w_grp"], p["pool_scale"]
    swa_w_qkv, swa_q_norm, swa_k_norm = p["swa_w_qkv"], p["swa_q_norm"], p["swa_k_norm"]
    swa_sink, swa_w_out = p["swa_sink"], p["swa_w_out"]
    cnv_w_pw1, cnv_dw, cnv_ln_w, cnv_ln_b, cnv_w_pw2 = (p["cnv_w_pw1"], p["cnv_dw"], p["cnv_ln_w"],
                                                       p["cnv_ln_b"], p["cnv_w_pw2"])
    if True:
        kind, slot = i % N_MIXERS, i // N_MIXERS
        keep_ctx = _keeps_ctx(i, dims["depth"])
        nw_mix = norm_mix_w[i].reshape(1, d)
        lat = dict(seq=seq, tm=tm_l)
        cx = dict(seq=ctx_len, tm=tm_c)

        if kind == MIX_DELTANET:
            w_in = dn_w_in[slot]
            wqkv = w_in[:, :3 * DN_DIM].astype(BF16)
            wz = w_in[:, 3 * DN_DIM:4 * DN_DIM].astype(BF16)
            wgt = jnp.zeros((d, LANES), F32).at[:, :4 * DN_HEADS].set(w_in[:, 4 * DN_DIM:]).astype(BF16)
            aneg = jnp.zeros((1, LANES), F32).at[0, 2 * DN_HEADS:4 * DN_HEADS].set(
                -jnp.exp(dn_a_log[slot].astype(F32)).reshape(-1))
            dtb = jnp.zeros((1, LANES), F32).at[0, 2 * DN_HEADS:4 * DN_HEADS].set(
                dn_dt_bias[slot].astype(F32).reshape(-1))
            hnw = jnp.tile(dn_norm_w[slot].reshape(1, DN_HEAD_DIM), (1, 1))
            wout = dn_w_out[slot].astype(BF16)
            proj = functools.partial(dn_project, nw=nw_mix, wqkv=wqkv, cw=dn_conv[slot], wz=wz, wgt=wgt,
                                     aneg=aneg, dtb=dtb)
            qc, kc, vc, zc, bgc = proj(hc, mod_c, **cx)
            ql, kl, vl, zl, bgl = proj(h, mod_l, **lat)
            s0 = jnp.zeros((bsz, 2, DN_HEADS, DN_HEAD_DIM, DN_HEAD_DIM), F32)
            ocf, ocb, s_ctx = dn_scan(qc, kc, vc, bgc, s0, batch=bsz, seq=ctx_len)
            olf, olb, _ = dn_scan(ql, kl, vl, bgl, s_ctx, batch=bsz, seq=seq)
            h = dn_readout(olf, olb, zl, h, mod_l, hnw, wout, **lat)
            if keep_ctx:
                hc = dn_readout(ocf, ocb, zc, hc, mod_c, hnw, wout, **cx)
        elif kind == MIX_POOL:
            wg = pool_w_grp[slot].astype(BF16)
            scale = pool_scale[slot].reshape(1, d)
            h = pool_mixer(h, mod_l, nw_mix, wg, scale, **lat)
            if keep_ctx:
                hc = pool_mixer(hc, mod_c, nw_mix, wg, scale, **cx)
        elif kind == MIX_SWA:
            wq = swa_w_qkv[slot]
            dup = lambda w_: jnp.repeat(w_.reshape(d, SWA_KV_HEADS, 1, SWA_HEAD_DIM), 2, axis=2).reshape(
                d, 2 * SWA_KV_DIM)
            w_all = jnp.concatenate([wq[:, :SWA_DIM], dup(wq[:, SWA_DIM:SWA_DIM + SWA_KV_DIM]),
                                     dup(wq[:, SWA_DIM + SWA_KV_DIM:])], axis=1).astype(BF16)
            hnw = jnp.concatenate([jnp.tile(swa_q_norm[slot], SWA_HEADS),
                                   jnp.tile(swa_k_norm[slot], 2 * SWA_KV_HEADS)]).reshape(1, SWA_NORM_COLS)
            li = jnp.arange(LANES)
            seg = jnp.where((li[:, None] // SWA_HEAD_DIM) == (li[None, :] // SWA_HEAD_DIM),
                            1.0 / SWA_HEAD_DIM, 0.0).astype(BF16)
            cos, sin = _rope_tables(seq)
            sink = jnp.broadcast_to(swa_sink[slot].astype(F32).reshape(SWA_HEADS, 1), (SWA_HEADS, LANES))
            ql, k2, v2 = swa_qkv(h, mod_l, nw_mix, w_all, hnw, seg, cos, sin, rope=True, **lat)
            _, kx2, vx2 = swa_qkv(hc, mod_c, nw_mix, w_all, hnw, seg, cos[:tm_c], sin[:tm_c], rope=False, **cx)
            o = swa_attention(ql, k2, v2, kx2, vx2, sink, batch=bsz, seq=seq, ctx_len=ctx_len)
            h = matmul_residual(o, h, mod_l, swa_w_out[slot].astype(BF16), **lat)
            assert not keep_ctx
        else:
            args = (nw_mix, cnv_w_pw1[slot].astype(BF16), cnv_dw[slot], cnv_ln_w[slot].reshape(1, d),
                    cnv_ln_b[slot].reshape(1, d), cnv_w_pw2[slot].astype(BF16))
            h = conv_module(h, mod_l, *args, **lat)
            if keep_ctx:
                hc = conv_module(hc, mod_c, *args, **cx)
    return h, hc
```

```python
import functools

import jax
import jax.numpy as jnp
from jax import lax
from jax.experimental import pallas as pl
from jax.experimental.pallas import tpu as pltpu

D_MODEL = 1024
EPS = 1e-6
NEG_INF = -1e30
GRID_W = 64
ROPE_BASE = 10000.0
N_MIXERS = 4
MIX_DELTANET, MIX_POOL, MIX_SWA, MIX_CONV = 0, 1, 2, 3
CTX_READING_MIXERS = (MIX_DELTANET, MIX_SWA)

DN_HEADS = 8
DN_HEAD_DIM = 128
DN_DIM = DN_HEADS * DN_HEAD_DIM
DN_CONV = 5
DN_CHUNK = 128

POOL_WINDOWS = (2, 4, 8, 16)
POOL_GROUP = D_MODEL // len(POOL_WINDOWS)

SWA_HEADS = 16
SWA_KV_HEADS = 4
SWA_HEAD_DIM = 64
SWA_GROUP = SWA_HEADS // SWA_KV_HEADS
SWA_DIM = SWA_HEADS * SWA_HEAD_DIM
SWA_KV_DIM = SWA_KV_HEADS * SWA_HEAD_DIM
SWA_WINDOW = 128
SWA_BLOCK = 128

CNV_WIDTH = 31
FFN_DIM = 2816
FFN_CONV = 3

LANES = 128
SUBLANES = 8
VMEM_LIMIT_BYTES = 56 * 1024 * 1024

F32 = jnp.float32
BF16 = jnp.bfloat16
HALO = 2 * SUBLANES
POOL_HALO = SUBLANES


def _cparams(n_axes=1):
    return pltpu.CompilerParams(
        dimension_semantics=("arbitrary",) * n_axes,
        vmem_limit_bytes=VMEM_LIMIT_BYTES,
    )


def _resident(shape):
    nd = len(shape)
    return pl.BlockSpec(shape, lambda *_: (0,) * nd, pipeline_mode=pl.Buffered(1))


def _halo_specs(tm, hb, n_tiles, width):
    r = tm // hb
    last = n_tiles * r - 1
    prev = pl.BlockSpec((hb, width), lambda i: (jnp.maximum(i * r - 1, 0), 0))
    main = pl.BlockSpec((tm, width), lambda i: (i, 0))
    nxt = pl.BlockSpec((hb, width), lambda i: (jnp.minimum((i + 1) * r, last), 0))
    return prev, main, nxt


def _mod_spec(tiles_per_seq):
    return pl.BlockSpec((None, 6, D_MODEL), lambda i: (i // tiles_per_seq, 0, 0))


def _rms_mod(x, nw, sc, sh):
    ms = jnp.mean(x * x, axis=-1, keepdims=True)
    return (x * lax.rsqrt(ms + EPS) * nw) * (1.0 + sc) + sh


def _fill_hin_ext(hin_ref, xp_ref, x_ref, xn_ref, nw, sc, sh, tiles_per_seq, hb, tm):
    i = pl.program_id(0)
    pos = lax.rem(i, tiles_per_seq)
    first = pos == 0
    last = pos == tiles_per_seq - 1
    hp = jnp.where(first, 0.0, _rms_mod(xp_ref[...], nw, sc, sh))
    hn = jnp.where(last, 0.0, _rms_mod(xn_ref[...], nw, sc, sh))
    hin_ref[0:hb, :] = hp.astype(hin_ref.dtype)
    hin_ref[hb:hb + tm, :] = _rms_mod(x_ref[...], nw, sc, sh).astype(hin_ref.dtype)
    hin_ref[hb + tm:hb + tm + hb, :] = hn.astype(hin_ref.dtype)


def _shift_rows(a, d, n_rows):
    if d == 0:
        return a
    return pltpu.roll(a, (-d) % n_rows, 0)


ADA_TN = 1536


def _ada_kernel(cond_ref, w_ref, b_ref, o_ref):
    s = jax.nn.silu(cond_ref[...])
    o_ref[...] = jnp.dot(s, w_ref[...], preferred_element_type=F32,
                         precision=lax.Precision.HIGHEST) + b_ref[...]


def ada_modulation(cond, ada_w, ada_b):
    depth, d, n = ada_w.shape
    rows = cond.shape[0]
    return pl.pallas_call(
        _ada_kernel,
        out_shape=jax.ShapeDtypeStruct((depth, rows, n), F32),
        grid=(depth, n // ADA_TN),
        in_specs=[
            pl.BlockSpec((rows, d), lambda l, j: (0, 0)),
            pl.BlockSpec((None, d, ADA_TN), lambda l, j: (l, 0, j)),
            pl.BlockSpec((None, 1, ADA_TN), lambda l, j: (l, 0, j)),
        ],
        out_specs=pl.BlockSpec((None, rows, ADA_TN), lambda l, j: (l, 0, j)),
        compiler_params=_cparams(2),
        name="ada_modulation",
    )(cond, ada_w, ada_b.reshape(depth, 1, n))


FFN_FC = 256
FFN_DOWN_GROUP = 6


def _ffn_kernel(xp_ref, x_ref, xn_ref, mod_ref, nw_ref, wg_ref, wu_ref, cw_ref, cb_ref, wd_ref,
                o_ref, hin_ref, act_ref, acc_ref, *, tm, tiles_per_seq):
    sh, sc, gt = mod_ref[3:4, :], mod_ref[4:5, :], mod_ref[5:6, :]
    _fill_hin_ext(hin_ref, xp_ref, x_ref, xn_ref, nw_ref[...], sc, sh, tiles_per_seq, HALO, tm)
    n_ext = tm + 2 * HALO
    n_chunks = FFN_DIM // FFN_FC
    group_start = 0
    for c in range(n_chunks):
        cols = slice(c * FFN_FC, (c + 1) * FFN_FC)
        g_ext = jnp.dot(hin_ref[...], wg_ref[:, cols], preferred_element_type=F32)
        u = jnp.dot(hin_ref[HALO:HALO + tm, :], wu_ref[:, cols], preferred_element_type=F32)
        g = cb_ref[:, cols]
        for k in range(FFN_CONV):
            g = g + cw_ref[k:k + 1, cols] * _shift_rows(g_ext, k - 1, n_ext)[HALO:HALO + tm, :]
        act_ref[:, cols] = (jax.nn.silu(g) * u).astype(BF16)
        if (c + 1) % FFN_DOWN_GROUP == 0 or c == n_chunks - 1:
            ks = slice(group_start * FFN_FC, (c + 1) * FFN_FC)
            y = jnp.dot(act_ref[:, ks], wd_ref[ks, :], preferred_element_type=F32)
            if group_start == 0:
                acc_ref[...] = y
            else:
                acc_ref[...] += y
            group_start = c + 1
    o_ref[...] = x_ref[...] + gt * acc_ref[...]


def conv_ffn(h, mod, nw, wg, wu, cw, cb, wd, *, seq, tm):
    n, d = h.shape
    n_tiles = n // tm
    tps = seq // tm
    prev, main, nxt = _halo_specs(tm, HALO, n_tiles, d)
    return pl.pallas_call(
        functools.partial(_ffn_kernel, tm=tm, tiles_per_seq=tps),
        out_shape=jax.ShapeDtypeStruct((n, d), F32),
        grid=(n_tiles,),
        in_specs=[prev, main, nxt, _mod_spec(tps), _resident((1, d)),
                  _resident(wg.shape), _resident(wu.shape), _resident(cw.shape),
                  _resident(cb.shape), _resident(wd.shape)],
        out_specs=pl.BlockSpec((tm, d), lambda i: (i, 0)),
        scratch_shapes=[pltpu.VMEM((tm + 2 * HALO, d), BF16), pltpu.VMEM((tm, FFN_DIM), BF16),
                        pltpu.VMEM((tm, d), F32)],
        compiler_params=_cparams(),
        name="conv_ffn",
    )(h, h, h, mod, nw, wg, wu, cw, cb, wd)


def _mm_res_kernel(a_ref, res_ref, mod_ref, w_ref, o_ref):
    gt = mod_ref[2:3, :]
    o_ref[...] = res_ref[...] + gt * jnp.dot(a_ref[...], w_ref[...], preferred_element_type=F32)


def matmul_residual(a, res, mod, w, *, seq, tm):
    n, k = a.shape
    d = w.shape[1]
    tps = seq // tm
    return pl.pallas_call(
        _mm_res_kernel,
        out_shape=jax.ShapeDtypeStruct((n, d), F32),
        grid=(n // tm,),
        in_specs=[pl.BlockSpec((tm, k), lambda i: (i, 0)), pl.BlockSpec((tm, d), lambda i: (i, 0)),
                  _mod_spec(tps), _resident(w.shape)],
        out_specs=pl.BlockSpec((tm, d), lambda i: (i, 0)),
        compiler_params=_cparams(),
        name="matmul_residual",
    )(a, res, mod, w)


DN_PC = 256


def _dn_proj_kernel(xp_ref, x_ref, xn_ref, mod_ref, nw_ref, wqkv_ref, cw_ref, wz_ref, wgt_ref,
                    aneg_ref, dtb_ref, q_ref, k_ref, v_ref, z_ref, bg_ref, hin_ref,
                    *, tm, tiles_per_seq):
    sh, sc = mod_ref[0:1, :], mod_ref[1:2, :]
    _fill_hin_ext(hin_ref, xp_ref, x_ref, xn_ref, nw_ref[...], sc, sh, tiles_per_seq, HALO, tm)
    n_ext = tm + 2 * HALO
    outs = (q_ref, k_ref, v_ref)
    per_out = DN_DIM // DN_PC
    for c in range(3 * per_out):
        cols = slice(c * DN_PC, (c + 1) * DN_PC)
        p_ext = jnp.dot(hin_ref[...], wqkv_ref[:, cols], preferred_element_type=F32)
        acc = None
        for k in range(DN_CONV):
            term = cw_ref[k:k + 1, cols] * _shift_rows(p_ext, k - DN_CONV // 2, n_ext)[HALO:HALO + tm, :]
            acc = term if acc is None else acc + term
        a = jax.nn.silu(acc)
        which, cc = divmod(c, per_out)
        for hh in range(DN_PC // DN_HEAD_DIM):
            ah = a[:, hh * DN_HEAD_DIM:(hh + 1) * DN_HEAD_DIM]
            if which < 2:
                ah = ah * lax.rsqrt(jnp.sum(ah * ah, axis=-1, keepdims=True) + EPS)
                if which == 0:
                    ah = ah * (DN_HEAD_DIM ** -0.5)
            lo = cc * DN_PC + hh * DN_HEAD_DIM
            outs[which][:, lo:lo + DN_HEAD_DIM] = ah.astype(BF16)
    hm = hin_ref[HALO:HALO + tm, :]
    z_ref[...] = jnp.dot(hm, wz_ref[...], preferred_element_type=F32).astype(BF16)
    gates = jnp.dot(hm, wgt_ref[...], preferred_element_type=F32)
    lane = lax.broadcasted_iota(jnp.int32, gates.shape, 1)
    beta = jax.nn.sigmoid(gates)
    g = aneg_ref[...] * jax.nn.softplus(gates + dtb_ref[...])
    bg_ref[...] = jnp.where(lane < 2 * DN_HEADS, beta, g)


def dn_project(h, mod, nw, wqkv, cw, wz, wgt, aneg, dtb, *, seq, tm):
    n, d = h.shape
    n_tiles = n // tm
    tps = seq // tm
    prev, main, nxt = _halo_specs(tm, HALO, n_tiles, d)
    row = lambda w: pl.BlockSpec((tm, w), lambda i: (i, 0))
    return pl.pallas_call(
        functools.partial(_dn_proj_kernel, tm=tm, tiles_per_seq=tps),
        out_shape=[jax.ShapeDtypeStruct((n, DN_DIM), BF16)] * 4 + [jax.ShapeDtypeStruct((n, LANES), F32)],
        grid=(n_tiles,),
        in_specs=[prev, main, nxt, _mod_spec(tps), _resident((1, d)), _resident(wqkv.shape),
                  _resident(cw.shape), _resident(wz.shape), _resident(wgt.shape),
                  _resident(aneg.shape), _resident(dtb.shape)],
        out_specs=[row(DN_DIM)] * 4 + [row(LANES)],
        scratch_shapes=[pltpu.VMEM((tm + 2 * HALO, d), BF16)],
        compiler_params=_cparams(),
        name="dn_project",
    )(h, h, h, mod, nw, wqkv, cw, wz, wgt, aneg, dtb)


def _bdot(a, b):
    return jnp.dot(a.astype(BF16), b.astype(BF16), preferred_element_type=F32)


def _unit_tri_inverses(lmats, eye, blk):
    c = lmats[0].shape[0]
    ns = [jnp.where(blk[0], -l, 0.0) for l in lmats]
    xs = [eye + n for n in ns]
    ps = [_bdot(n, n) for n in ns]
    for _ in range(2):
        rs = [_bdot(jnp.concatenate([x, p], axis=0), p) for x, p in zip(xs, ps)]
        xs = [x + r[:c] for x, r in zip(xs, rs)]
        ps = [r[c:] for r in rs]
    xs = [x + _bdot(x, p) for x, p in zip(xs, ps)]
    for lvl in range(1, len(blk)):
        sel = jnp.logical_and(blk[lvl], jnp.logical_not(blk[lvl - 1]))
        ts = [_bdot(jnp.where(sel, l, 0.0), x) for l, x in zip(lmats, xs)]
        xs = [x - _bdot(x, t) for x, t in zip(xs, ts)]
    return xs


def _dn_scan_kernel(qf_ref, kf_ref, vf_ref, bgf_ref, qb_ref, kb_ref, vb_ref, bgb_ref, s0_ref,
                    of_ref, ob_ref, sfin_ref, s_ref, *, n_chunks):
    n = pl.program_id(1)
    c = DN_CHUNK

    @pl.when(n == 0)
    def _():
        s_ref[...] = s0_ref[...]

    ri = lax.broadcasted_iota(jnp.int32, (c, c), 0)
    ci = lax.broadcasted_iota(jnp.int32, (c, c), 1)
    eye = (ri == ci).astype(F32)
    blk = []
    bs = 16
    while bs <= c:
        blk.append((ri // bs) == (ci // bs))
        bs *= 2

    dirs = ((qf_ref, kf_ref, vf_ref, bgf_ref, of_ref, ri >= ci, ri > ci),
            (qb_ref, kb_ref, vb_ref, bgb_ref, ob_ref, ri <= ci, ri < ci))
    for d, (q_ref, k_ref, v_ref, bg_ref, o_ref, incl, strict) in enumerate(dirs):
        bg = bg_ref[...]
        tri = incl.astype(F32)
        is_g = lax.broadcasted_iota(jnp.int32, bg.shape, 1) >= 2 * DN_HEADS
        bgg = jnp.where(is_g, bg, 0.0)
        cum = jnp.dot(tri, bgg, preferred_element_type=F32, precision=lax.Precision.HIGHEST)
        cum_t = cum.T
        tot = jnp.sum(bgg, axis=0, keepdims=True)
        egc_all = jnp.exp(cum)
        ekd_all = jnp.exp(tot - cum)
        etot = jnp.exp(tot)
        heads = range(DN_HEADS)
        hsl = [slice(h * DN_HEAD_DIM, (h + 1) * DN_HEAD_DIM) for h in heads]
        lbs = [d * DN_HEADS + h for h in heads]
        lgs = [2 * DN_HEADS + lb for lb in lbs]
        betas = [bg[:, lb:lb + 1] for lb in lbs]
        kq = [lax.dot_general(jnp.concatenate([k_ref[:, hs], q_ref[:, hs]], axis=0), k_ref[:, hs],
                              (((1,), (1,)), ((), ())), preferred_element_type=F32) for hs in hsl]
        decays = [jnp.where(incl, jnp.exp(jnp.where(incl, cum[:, lg:lg + 1] - cum_t[lg:lg + 1, :], 0.0)), 0.0)
                  for lg in lgs]
        lmats = [jnp.where(strict, b * r[:c] * dc, 0.0) for b, r, dc in zip(betas, kq, decays)]
        qks = [(r[c:] * dc).astype(BF16) for r, dc in zip(kq, decays)]
        tinvs = _unit_tri_inverses(lmats, eye, blk)
        rhss = [jnp.concatenate([v_ref[:, hs].astype(F32) * b,
                                 k_ref[:, hs].astype(F32) * (b * egc_all[:, lg:lg + 1])], axis=1)
                for hs, b, lg in zip(hsl, betas, lgs)]
        uws = [_bdot(t, r) for t, r in zip(tinvs, rhss)]
        wqs = [_bdot(jnp.concatenate([uw[:, DN_HEAD_DIM:],
                                      q_ref[:, hs].astype(F32) * egc_all[:, lg:lg + 1]], axis=0), s_ref[d, h])
               for uw, hs, lg, h in zip(uws, hsl, lgs, heads)]
        v_news = [(uw[:, :DN_HEAD_DIM] - wq[:c]).astype(BF16) for uw, wq in zip(uws, wqs)]
        outs = [wq[c:] + jnp.dot(qk, vn, preferred_element_type=F32) for wq, qk, vn in zip(wqs, qks, v_news)]
        upds = [lax.dot_general((k_ref[:, hs].astype(F32) * ekd_all[:, lg:lg + 1]).astype(BF16), vn,
                                (((0,), (0,)), ((), ())), preferred_element_type=F32)
                for hs, lg, vn in zip(hsl, lgs, v_news)]
        for h, hs, lg, o, upd in zip(heads, hsl, lgs, outs, upds):
            o_ref[:, hs] = o.astype(o_ref.dtype)
            s_ref[d, h] = s_ref[d, h] * etot[:, lg:lg + 1] + upd

    @pl.when(n == n_chunks - 1)
    def _():
        sfin_ref[...] = s_ref[...]


def dn_scan(q, k, v, bg, s0, *, batch, seq):
    nc = seq // DN_CHUNK
    fwd = lambda w: pl.BlockSpec((DN_CHUNK, w), lambda b, n: (b * nc + n, 0))
    bwd = lambda w: pl.BlockSpec((DN_CHUNK, w), lambda b, n: (b * nc + nc - 1 - n, 0))
    st = pl.BlockSpec((None, 2, DN_HEADS, DN_HEAD_DIM, DN_HEAD_DIM), lambda b, n: (b, 0, 0, 0, 0))
    return pl.pallas_call(
        functools.partial(_dn_scan_kernel, n_chunks=nc),
        out_shape=[jax.ShapeDtypeStruct(q.shape, BF16), jax.ShapeDtypeStruct(q.shape, BF16),
                   jax.ShapeDtypeStruct(s0.shape, F32)],
        grid=(batch, nc),
        in_specs=[fwd(DN_DIM), fwd(DN_DIM), fwd(DN_DIM), fwd(LANES),
                  bwd(DN_DIM), bwd(DN_DIM), bwd(DN_DIM), bwd(LANES), st],
        out_specs=[fwd(DN_DIM), bwd(DN_DIM), st],
        scratch_shapes=[pltpu.VMEM((2, DN_HEADS, DN_HEAD_DIM, DN_HEAD_DIM), F32)],
        compiler_params=_cparams(2),
        name="dn_scan",
    )(q, k, v, bg, q, k, v, bg, s0)


def _dn_readout_kernel(of_ref, ob_ref, z_ref, res_ref, mod_ref, nw_ref, w_ref, o_ref, y_ref):
    gt = mod_ref[2:3, :]
    for h in range(DN_HEADS):
        hs = slice(h * DN_HEAD_DIM, (h + 1) * DN_HEAD_DIM)
        o = of_ref[:, hs].astype(F32) + ob_ref[:, hs].astype(F32)
        ms = jnp.mean(o * o, axis=-1, keepdims=True)
        y = o * lax.rsqrt(ms + EPS) * nw_ref[...]
        y_ref[:, hs] = (y * jax.nn.silu(z_ref[:, hs].astype(F32))).astype(BF16)
    o_ref[...] = res_ref[...] + gt * jnp.dot(y_ref[...], w_ref[...], preferred_element_type=F32)


def dn_readout(o_f, o_b, z, res, mod, nw, w, *, seq, tm):
    n, d = res.shape
    tps = seq // tm
    row = lambda w_: pl.BlockSpec((tm, w_), lambda i: (i, 0))
    return pl.pallas_call(
        _dn_readout_kernel,
        out_shape=jax.ShapeDtypeStruct((n, d), F32),
        grid=(n // tm,),
        in_specs=[row(DN_DIM), row(DN_DIM), row(DN_DIM), row(d), _mod_spec(tps),
                  _resident(nw.shape), _resident(w.shape)],
        out_specs=row(d),
        scratch_shapes=[pltpu.VMEM((tm, DN_DIM), BF16)],
        compiler_params=_cparams(),
        name="dn_readout",
    )(o_f, o_b, z, res, mod, nw, w)


def _pool_kernel(xp_ref, x_ref, xn_ref, mod_ref, nw_ref, wg_ref, scale_ref, o_ref, hin_ref, y_ref,
                 *, tm, tiles_per_seq, seq):
    sh, sc, gt = mod_ref[0:1, :], mod_ref[1:2, :], mod_ref[2:3, :]
    hb = POOL_HALO
    _fill_hin_ext(hin_ref, xp_ref, x_ref, xn_ref, nw_ref[...], sc, sh, tiles_per_seq, hb, tm)
    t0 = lax.rem(pl.program_id(0), tiles_per_seq) * tm
    t = t0 + lax.broadcasted_iota(jnp.int32, (tm, 1), 0)
    for gi, win in enumerate(POOL_WINDOWS):
        cols = slice(gi * POOL_GROUP, (gi + 1) * POOL_GROUP)
        half = win // 2
        s = None
        for dlt in range(-half, win - half):
            term = hin_ref[hb + dlt:hb + dlt + tm, cols]
            s = term if s is None else s + term
        cnt = (jnp.minimum(t + (win - half), seq) - jnp.maximum(t - half, 0)).astype(F32)
        dd = (s / cnt - hin_ref[hb:hb + tm, cols]).astype(BF16)
        y_ref[:, cols] = jnp.dot(dd, wg_ref[gi], preferred_element_type=F32)
    o_ref[...] = x_ref[...] + gt * (y_ref[...] * scale_ref[...])


def pool_mixer(h, mod, nw, wg, scale, *, seq, tm):
    n, d = h.shape
    n_tiles = n // tm
    tps = seq // tm
    prev, main, nxt = _halo_specs(tm, POOL_HALO, n_tiles, d)
    return pl.pallas_call(
        functools.partial(_pool_kernel, tm=tm, tiles_per_seq=tps, seq=seq),
        out_shape=jax.ShapeDtypeStruct((n, d), F32),
        grid=(n_tiles,),
        in_specs=[prev, main, nxt, _mod_spec(tps), _resident((1, d)), _resident(wg.shape),
                  _resident(scale.shape)],
        out_specs=pl.BlockSpec((tm, d), lambda i: (i, 0)),
        scratch_shapes=[pltpu.VMEM((tm + 2 * POOL_HALO, d), F32), pltpu.VMEM((tm, d), F32)],
        compiler_params=_cparams(),
        name="pool_mixer",
    )(h, h, h, mod, nw, wg, scale)


SWA_QKV_COLS = SWA_DIM + 4 * SWA_KV_DIM
SWA_NORM_COLS = SWA_DIM + 2 * SWA_KV_DIM
SWA_PC = 256


def _swa_qkv_kernel(x_ref, mod_ref, nw_ref, w_ref, hnw_ref, seg_ref, cos_ref, sin_ref,
                    q_ref, k_ref, v_ref, *, rope):
    sh, sc = mod_ref[0:1, :], mod_ref[1:2, :]
    hin = _rms_mod(x_ref[...], nw_ref[...], sc, sh).astype(BF16)
    pc = SWA_PC
    n_chunks = SWA_QKV_COLS // pc
    q_chunks = SWA_DIM // pc
    k_chunks = 2 * SWA_KV_DIM // pc
    for j in range(n_chunks):
        cols = slice(j * pc, (j + 1) * pc)
        p = jnp.dot(hin, w_ref[:, cols], preferred_element_type=F32)
        if j < q_chunks + k_chunks:
            ms = jnp.dot((p * p).astype(BF16), seg_ref[...], preferred_element_type=F32)
            p = p * lax.rsqrt(ms + EPS) * hnw_ref[:, cols]
            if rope:
                lane = lax.broadcasted_iota(jnp.int32, (p.shape[0], LANES), 1)
                up = lax.rem(lane, 32) < 16
                halves = []
                for t in range(pc // LANES):
                    ph = p[:, t * LANES:(t + 1) * LANES]
                    partner = jnp.where(up, pltpu.roll(ph, LANES - 16, 1), pltpu.roll(ph, 16, 1))
                    halves.append(ph * cos_ref[...] + partner * sin_ref[...])
                p = jnp.concatenate(halves, axis=1)
        if j < q_chunks:
            q_ref[:, cols] = (p * (SWA_HEAD_DIM ** -0.5)).astype(BF16)
        elif j < q_chunks + k_chunks:
            k_ref[:, (j - q_chunks) * pc:(j - q_chunks + 1) * pc] = p.astype(BF16)
        else:
            jj = j - q_chunks - k_chunks
            v_ref[:, jj * pc:(jj + 1) * pc] = p.astype(BF16)


def swa_qkv(h, mod, nw, w, hnw, seg, cos, sin, *, seq, tm, rope):
    n, d = h.shape
    tps = seq // tm
    row = lambda w_: pl.BlockSpec((tm, w_), lambda i: (i, 0))
    tab = pl.BlockSpec((tm, LANES), lambda i: (i % tps, 0))
    kvw = 2 * SWA_KV_DIM
    return pl.pallas_call(
        functools.partial(_swa_qkv_kernel, rope=rope),
        out_shape=[jax.ShapeDtypeStruct((n, SWA_DIM), BF16), jax.ShapeDtypeStruct((n, kvw), BF16),
                   jax.ShapeDtypeStruct((n, kvw), BF16)],
        grid=(n // tm,),
        in_specs=[row(d), _mod_spec(tps), _resident((1, d)), _resident(w.shape), _resident(hnw.shape),
                  _resident(seg.shape), tab, tab],
        out_specs=[row(SWA_DIM), row(kvw), row(kvw)],
        compiler_params=_cparams(),
        name="swa_qkv",
    )(h, mod, nw, w, hnw, seg, cos, sin)


def _swa_attn_kernel(q_ref, kp_ref, kc_ref, kn_ref, vp_ref, vc_ref, vn_ref, kx_ref, vx_ref, sink_ref,
                     o_ref, *, blocks_per_seq):
    i = pl.program_id(1)
    blk = SWA_BLOCK
    ri = lax.broadcasted_iota(jnp.int32, (SWA_GROUP * blk, blk), 0) % blk
    ci = lax.broadcasted_iota(jnp.int32, (SWA_GROUP * blk, blk), 1)
    ok_prev = jnp.logical_and(ci >= ri, i > 0)
    ok_next = jnp.logical_and(ci <= ri, i < blocks_per_seq - 1)
    lane = lax.broadcasted_iota(jnp.int32, (blk, LANES), 1)
    low = lane < SWA_HEAD_DIM
    nt = (((1,), (1,)), ((), ()))
    for g in range(SWA_KV_HEADS):
        gs = slice(g * LANES, (g + 1) * LANES)
        qs = []
        for t in range(2):
            qt = q_ref[:, (2 * g + t) * LANES:(2 * g + t + 1) * LANES]
            qs.append(jnp.where(low, qt, jnp.zeros_like(qt)))
            qs.append(jnp.where(low, jnp.zeros_like(qt), qt))
        qst = jnp.concatenate(qs, axis=0)
        s_p = jnp.where(ok_prev, lax.dot_general(qst, kp_ref[:, gs], nt, preferred_element_type=F32), NEG_INF)
        s_c = lax.dot_general(qst, kc_ref[:, gs], nt, preferred_element_type=F32)
        s_n = jnp.where(ok_next, lax.dot_general(qst, kn_ref[:, gs], nt, preferred_element_type=F32), NEG_INF)
        s_x = lax.dot_general(qst, kx_ref[:, gs], nt, preferred_element_type=F32)
        sink = jnp.concatenate(
            [jnp.broadcast_to(sink_ref[SWA_GROUP * g + hh:SWA_GROUP * g + hh + 1, 0:1], (blk, 1))
             for hh in range(SWA_GROUP)], axis=0)
        lane_tiles = lambda a: [a[:, t * LANES:(t + 1) * LANES] for t in range(a.shape[1] // LANES)]
        m = jnp.max(functools.reduce(jnp.maximum, [s_p, s_c, s_n] + lane_tiles(s_x)), axis=-1, keepdims=True)
        m = jnp.maximum(m, sink)
        e_p, e_c, e_n, e_x = (jnp.exp(s - m) for s in (s_p, s_c, s_n, s_x))
        den = jnp.sum(functools.reduce(jnp.add, [e_p, e_c, e_n] + lane_tiles(e_x)), axis=-1, keepdims=True)
        inv = 1.0 / (den + jnp.exp(sink - m))
        pv = (jnp.dot(e_p.astype(BF16), vp_ref[:, gs], preferred_element_type=F32)
              + jnp.dot(e_c.astype(BF16), vc_ref[:, gs], preferred_element_type=F32)
              + jnp.dot(e_n.astype(BF16), vn_ref[:, gs], preferred_element_type=F32)
              + jnp.dot(e_x.astype(BF16), vx_ref[:, gs], preferred_element_type=F32)) * inv
        for t in range(2):
            a = pv[(2 * t) * blk:(2 * t + 1) * blk]
            b = pv[(2 * t + 1) * blk:(2 * t + 2) * blk]
            o_ref[:, (2 * g + t) * LANES:(2 * g + t + 1) * LANES] = jnp.where(low, a, b).astype(BF16)


def swa_attention(q, k2, v2, kx2, vx2, sink, *, batch, seq, ctx_len):
    nb = seq // SWA_BLOCK
    kvw = 2 * SWA_KV_DIM
    qspec = pl.BlockSpec((SWA_BLOCK, SWA_DIM), lambda b, i: (b * nb + i, 0))
    kv = lambda off: pl.BlockSpec(
        (SWA_BLOCK, kvw), lambda b, i: (b * nb + jnp.clip(i + off, 0, nb - 1), 0))
    xspec = pl.BlockSpec((ctx_len, kvw), lambda b, i: (b, 0))
    return pl.pallas_call(
        functools.partial(_swa_attn_kernel, blocks_per_seq=nb),
        out_shape=jax.ShapeDtypeStruct(q.shape, BF16),
        grid=(batch, nb),
        in_specs=[qspec, kv(-1), kv(0), kv(1), kv(-1), kv(0), kv(1), xspec, xspec, _resident(sink.shape)],
        out_specs=qspec,
        compiler_params=_cparams(2),
        name="swa_attention",
    )(q, k2, k2, k2, v2, v2, v2, kx2, vx2, sink)


CNV_RB = 32
CNV_PC = 256
CNV_STRIDE = 4


def _conv_module_kernel(xp_ref, x_ref, xn_ref, mod_ref, nw_ref, w1_ref, dw_ref, lnw_ref, lnb_ref, w2_ref,
                        o_ref, hin_ref, u_ref, c_ref, s_ref, *, tm, tiles_per_seq):
    sh, sc, gt = mod_ref[0:1, :], mod_ref[1:2, :], mod_ref[2:3, :]
    _fill_hin_ext(hin_ref, xp_ref, x_ref, xn_ref, nw_ref[...], sc, sh, tiles_per_seq, HALO, tm)
    d = D_MODEL
    n_slab = d // LANES
    for c in range(d // CNV_PC):
        cols = slice(c * CNV_PC, (c + 1) * CNV_PC)
        gcols = slice(d + c * CNV_PC, d + (c + 1) * CNV_PC)
        a1 = jnp.dot(hin_ref[...], w1_ref[:, cols], preferred_element_type=F32)
        a2 = jnp.dot(hin_ref[...], w1_ref[:, gcols], preferred_element_type=F32)
        u = a1 * jax.nn.sigmoid(a2)
        for t in range(CNV_PC // LANES):
            u_ref[c * (CNV_PC // LANES) + t] = u[:, t * LANES:(t + 1) * LANES]
    pad = CNV_WIDTH // 2
    rows = tm // CNV_STRIDE
    for j in range(n_slab):
        lanes = slice(j * LANES, (j + 1) * LANES)
        for v in range(CNV_STRIDE):
            acc = None
            for k in range(CNV_WIDTH):
                term = dw_ref[k:k + 1, lanes] * u_ref[j, pl.ds(HALO + v + k - pad, rows, stride=CNV_STRIDE), :]
                acc = term if acc is None else acc + term
            c_ref[j, pl.ds(v, rows, stride=CNV_STRIDE), :] = acc
    for rb in range(tm // CNV_RB):
        rsl = slice(rb * CNV_RB, (rb + 1) * CNV_RB)
        xs = [c_ref[j, rsl, :] for j in range(n_slab)]
        tot = xs[0]
        for xj in xs[1:]:
            tot = tot + xj
        mu = jnp.sum(tot, axis=-1, keepdims=True) * (1.0 / d)
        xcs = [xj - mu for xj in xs]
        sq = xcs[0] * xcs[0]
        for xc in xcs[1:]:
            sq = sq + xc * xc
        rstd = lax.rsqrt(jnp.sum(sq, axis=-1, keepdims=True) * (1.0 / d) + EPS)
        for j, xc in enumerate(xcs):
            lanes = slice(j * LANES, (j + 1) * LANES)
            y = xc * rstd * lnw_ref[:, lanes] + lnb_ref[:, lanes]
            s_ref[rsl, lanes] = jax.nn.silu(y).astype(BF16)
    o_ref[...] = x_ref[...] + gt * jnp.dot(s_ref[...], w2_ref[...], preferred_element_type=F32)


def conv_module(h, mod, nw, w1, dw, lnw, lnb, w2, *, seq, tm):
    n, d = h.shape
    n_tiles = n // tm
    tps = seq // tm
    prev, main, nxt = _halo_specs(tm, HALO, n_tiles, d)
    return pl.pallas_call(
        functools.partial(_conv_module_kernel, tm=tm, tiles_per_seq=tps),
        out_shape=jax.ShapeDtypeStruct((n, d), F32),
        grid=(n_tiles,),
        in_specs=[prev, main, nxt, _mod_spec(tps), _resident((1, d)), _resident(w1.shape),
                  _resident(dw.shape), _resident(lnw.shape), _resident(lnb.shape), _resident(w2.shape)],
        out_specs=pl.BlockSpec((tm, d), lambda i: (i, 0)),
        scratch_shapes=[pltpu.VMEM((tm + 2 * HALO, d), BF16),
                        pltpu.VMEM((d // LANES, tm + 2 * HALO, LANES), F32),
                        pltpu.VMEM((d // LANES, tm, LANES), F32),
                        pltpu.VMEM((tm, d), BF16)],
        compiler_params=_cparams(),
        name="conv_module",
    )(h, h, h, mod, nw, w1, dw, lnw, lnb, w2)


def _rope_tables(seq):
    n_freq = SWA_HEAD_DIM // 4
    inv_freq = ROPE_BASE ** (-jnp.arange(n_freq, dtype=F32) / n_freq)
    pos = jnp.arange(seq)
    rows = (pos // GRID_W).astype(F32)
    cols = (pos % GRID_W).astype(F32)
    ang_r = rows[:, None] * inv_freq
    ang_c = cols[:, None] * inv_freq
    ang = jnp.concatenate([ang_r, ang_r, ang_c, ang_c], axis=-1)
    sign = jnp.where((jnp.arange(SWA_HEAD_DIM) % 32) < 16, -1.0, 1.0)
    cos = jnp.tile(jnp.cos(ang), (1, LANES // SWA_HEAD_DIM))
    sin = jnp.tile(jnp.sin(ang) * sign, (1, LANES // SWA_HEAD_DIM))
    return cos, sin


def _tile_rows(seq):
    return min(seq, 512)


def kernel(x, c, ctx, c_ctx, ada_w, ada_b, norm_mix_w, norm_ffn_w, dn_w_in, dn_conv, dn_a_log, dn_dt_bias,
           dn_norm_w, dn_w_out, pool_w_grp, pool_scale, swa_w_qkv, swa_q_norm, swa_k_norm, swa_sink,
           swa_w_out, cnv_w_pw1, cnv_dw, cnv_ln_w, cnv_ln_b, cnv_w_pw2, ffn_w_gate, ffn_w_up, ffn_conv,
           ffn_conv_b, ffn_w_down):
    bsz, seq, d = x.shape
    ctx_len = ctx.shape[1]
    depth = ada_w.shape[0]
    h = x.reshape(bsz * seq, d)
    hc = ctx.reshape(bsz * ctx_len, d)
    p = dict(norm_mix_w=norm_mix_w, norm_ffn_w=norm_ffn_w, dn_w_in=dn_w_in, dn_conv=dn_conv,
             dn_a_log=dn_a_log, dn_dt_bias=dn_dt_bias, dn_norm_w=dn_norm_w, dn_w_out=dn_w_out,
             pool_w_grp=pool_w_grp, pool_scale=pool_scale, swa_w_qkv=swa_w_qkv, swa_q_norm=swa_q_norm,
             swa_k_norm=swa_k_norm, swa_sink=swa_sink, swa_w_out=swa_w_out, cnv_w_pw1=cnv_w_pw1,
             cnv_dw=cnv_dw, cnv_ln_w=cnv_ln_w, cnv_ln_b=cnv_ln_b, cnv_w_pw2=cnv_w_pw2,
             ffn_w_gate=ffn_w_gate, ffn_w_up=ffn_w_up, ffn_conv=ffn_conv, ffn_conv_b=ffn_conv_b,
             ffn_w_down=ffn_w_down)
    dims = dict(bsz=bsz, seq=seq, ctx_len=ctx_len, depth=depth)
    mods = modulation(c, c_ctx, ada_w, ada_b)
    for i in range(depth):
        mod_l, mod_c = layer_modulation(mods, i, bsz)
        h, hc = mixer_layer(i, h, hc, mod_l, mod_c, p, dims)
        h, hc = ffn_layer(i, h, hc, mod_l, mod_c, p, dims)
    return h.reshape(bsz, seq, d)


def modulation(c, c_ctx, ada_w, ada_b):
    bsz, d = c.shape
    n_cond = -(-(bsz + 1) // SUBLANES) * SUBLANES
    cond = jnp.zeros((n_cond, d), F32).at[:bsz].set(c).at[bsz].set(c_ctx)
    return ada_modulation(cond, ada_w, ada_b)


def layer_modulation(mods, i, bsz):
    d = mods.shape[-1] // 6
    mod_l = mods[i, :bsz].reshape(bsz, 6, d)
    mod_c = jnp.broadcast_to(mods[i, bsz].reshape(1, 6, d), (bsz, 6, d))
    return mod_l, mod_c


def _keeps_ctx(i, depth):
    return any((j % N_MIXERS) in CTX_READING_MIXERS for j in range(i + 1, depth))


def ffn_layer(i, h, hc, mod_l, mod_c, p, dims):
    d = h.shape[-1]
    seq, ctx_len = dims["seq"], dims["ctx_len"]
    ffn_args = (p["norm_ffn_w"][i].reshape(1, d), p["ffn_w_gate"][i].astype(BF16),
                p["ffn_w_up"][i].astype(BF16), p["ffn_conv"][i], p["ffn_conv_b"][i].reshape(1, FFN_DIM),
                p["ffn_w_down"][i].astype(BF16))
    h = conv_ffn(h, mod_l, *ffn_args, seq=seq, tm=_tile_rows(seq))
    if _keeps_ctx(i, dims["depth"]):
        hc = conv_ffn(hc, mod_c, *ffn_args, seq=ctx_len, tm=_tile_rows(ctx_len))
    return h, hc


def mixer_layer(i, h, hc, mod_l, mod_c, p, dims):
    d = h.shape[-1]
    bsz, seq, ctx_len = dims["bsz"], dims["seq"], dims["ctx_len"]
    tm_l, tm_c = _tile_rows(seq), _tile_rows(ctx_len)
    norm_mix_w = p["norm_mix_w"]
    dn_w_in, dn_conv, dn_a_log, dn_dt_bias = p["dn_w_in"], p["dn_conv"], p["dn_a_log"], p["dn_dt_bias"]
    dn_norm_w, dn_w_out = p["dn_norm_w"], p["dn_w_out"]
    pool_w_grp, pool_scale = p["pool_w_grp"], p["pool_scale"]
    swa_w_qkv, swa_q_norm, swa_k_norm = p["swa_w_qkv"], p["swa_q_norm"], p["swa_k_norm"]
    swa_sink, swa_w_out = p["swa_sink"], p["swa_w_out"]
    cnv_w_pw1, cnv_dw, cnv_ln_w, cnv_ln_b, cnv_w_pw2 = (p["cnv_w_pw1"], p["cnv_dw"], p["cnv_ln_w"],
                                                       p["cnv_ln_b"], p["cnv_w_pw2"])
    if True:
        kind, slot = i % N_MIXERS, i // N_MIXERS
        keep_ctx = _keeps_ctx(i, dims["depth"])
        nw_mix = norm_mix_w[i].reshape(1, d)
        lat = dict(seq=seq, tm=tm_l)
        cx = dict(seq=ctx_len, tm=tm_c)

        if kind == MIX_DELTANET:
            w_in = dn_w_in[slot]
            wqkv = w_in[:, :3 * DN_DIM].astype(BF16)
            wz = w_in[:, 3 * DN_DIM:4 * DN_DIM].astype(BF16)
            wgt = jnp.zeros((d, LANES), F32).at[:, :4 * DN_HEADS].set(w_in[:, 4 * DN_DIM:]).astype(BF16)
            aneg = jnp.zeros((1, LANES), F32).at[0, 2 * DN_HEADS:4 * DN_HEADS].set(
                -jnp.exp(dn_a_log[slot].astype(F32)).reshape(-1))
            dtb = jnp.zeros((1, LANES), F32).at[0, 2 * DN_HEADS:4 * DN_HEADS].set(
                dn_dt_bias[slot].astype(F32).reshape(-1))
            hnw = jnp.tile(dn_norm_w[slot].reshape(1, DN_HEAD_DIM), (1, 1))
            wout = dn_w_out[slot].astype(BF16)
            proj = functools.partial(dn_project, nw=nw_mix, wqkv=wqkv, cw=dn_conv[slot], wz=wz, wgt=wgt,
                                     aneg=aneg, dtb=dtb)
            qc, kc, vc, zc, bgc = proj(hc, mod_c, **cx)
            ql, kl, vl, zl, bgl = proj(h, mod_l, **lat)
            s0 = jnp.zeros((bsz, 2, DN_HEADS, DN_HEAD_DIM, DN_HEAD_DIM), F32)
            ocf, ocb, s_ctx = dn_scan(qc, kc, vc, bgc, s0, batch=bsz, seq=ctx_len)
            olf, olb, _ = dn_scan(ql, kl, vl, bgl, s_ctx, batch=bsz, seq=seq)
            h = dn_readout(olf, olb, zl, h, mod_l, hnw, wout, **lat)
            if keep_ctx:
                hc = dn_readout(ocf, ocb, zc, hc, mod_c, hnw, wout, **cx)
        elif kind == MIX_POOL:
            wg = pool_w_grp[slot].astype(BF16)
            scale = pool_scale[slot].reshape(1, d)
            h = pool_mixer(h, mod_l, nw_mix, wg, scale, **lat)
            if keep_ctx:
                hc = pool_mixer(hc, mod_c, nw_mix, wg, scale, **cx)
        elif kind == MIX_SWA:
            wq = swa_w_qkv[slot]
            dup = lambda w_: jnp.repeat(w_.reshape(d, SWA_KV_HEADS, 1, SWA_HEAD_DIM), 2, axis=2).reshape(
                d, 2 * SWA_KV_DIM)
            w_all = jnp.concatenate([wq[:, :SWA_DIM], dup(wq[:, SWA_DIM:SWA_DIM + SWA_KV_DIM]),
                                     dup(wq[:, SWA_DIM + SWA_KV_DIM:])], axis=1).astype(BF16)
            hnw = jnp.concatenate([jnp.tile(swa_q_norm[slot], SWA_HEADS),
                                   jnp.tile(swa_k_norm[slot], 2 * SWA_KV_HEADS)]).reshape(1, SWA_NORM_COLS)
            li = jnp.arange(SWA_PC)
            seg = jnp.where((li[:, None] // SWA_HEAD_DIM) == (li[None, :] // SWA_HEAD_DIM),
                            1.0 / SWA_HEAD_DIM, 0.0).astype(BF16)
            cos, sin = _rope_tables(seq)
            sink = jnp.broadcast_to(swa_sink[slot].astype(F32).reshape(SWA_HEADS, 1), (SWA_HEADS, LANES))
            ql, k2, v2 = swa_qkv(h, mod_l, nw_mix, w_all, hnw, seg, cos, sin, rope=True, **lat)
            _, kx2, vx2 = swa_qkv(hc, mod_c, nw_mix, w_all, hnw, seg, cos[:tm_c], sin[:tm_c], rope=False, **cx)
            o = swa_attention(ql, k2, v2, kx2, vx2, sink, batch=bsz, seq=seq, ctx_len=ctx_len)
            h = matmul_residual(o, h, mod_l, swa_w_out[slot].astype(BF16), **lat)
            assert not keep_ctx
        else:
            args = (nw_mix, cnv_w_pw1[slot].astype(BF16), cnv_dw[slot], cnv_ln_w[slot].reshape(1, d),
                    cnv_ln_b[slot].reshape(1, d), cnv_w_pw2[slot].astype(BF16))
            h = conv_module(h, mod_l, *args, **lat)
            if keep_ctx:
                hc = conv_module(hc, mod_c, *args, **cx)
    return h, hc
```

```python
import functools

import jax
import jax.numpy as jnp
from jax import lax
from jax.experimental import pallas as pl
from jax.experimental.pallas import tpu as pltpu

D_MODEL = 1024
EPS = 1e-6
NEG_INF = -1e30
GRID_W = 64
ROPE_BASE = 10000.0
N_MIXERS = 4
MIX_DELTANET, MIX_POOL, MIX_SWA, MIX_CONV = 0, 1, 2, 3
CTX_READING_MIXERS = (MIX_DELTANET, MIX_SWA)

DN_HEADS = 8
DN_HEAD_DIM = 128
DN_DIM = DN_HEADS * DN_HEAD_DIM
DN_CONV = 5
DN_CHUNK = 128

POOL_WINDOWS = (2, 4, 8, 16)
POOL_GROUP = D_MODEL // len(POOL_WINDOWS)

SWA_HEADS = 16
SWA_KV_HEADS = 4
SWA_HEAD_DIM = 64
SWA_GROUP = SWA_HEADS // SWA_KV_HEADS
SWA_DIM = SWA_HEADS * SWA_HEAD_DIM
SWA_KV_DIM = SWA_KV_HEADS * SWA_HEAD_DIM
SWA_WINDOW = 128
SWA_BLOCK = 128

CNV_WIDTH = 31
FFN_DIM = 2816
FFN_CONV = 3

LANES = 128
SUBLANES = 8
VMEM_LIMIT_BYTES = 56 * 1024 * 1024

F32 = jnp.float32
BF16 = jnp.bfloat16
HALO = 2 * SUBLANES
POOL_HALO = SUBLANES
POOL_STRIDE = 4


def _cparams(n_axes=1):
    return pltpu.CompilerParams(
        dimension_semantics=("arbitrary",) * n_axes,
        vmem_limit_bytes=VMEM_LIMIT_BYTES,
    )


def _resident(shape):
    nd = len(shape)
    return pl.BlockSpec(shape, lambda *_: (0,) * nd, pipeline_mode=pl.Buffered(1))


def _halo_specs(tm, hb, n_tiles, width):
    r = tm // hb
    last = n_tiles * r - 1
    prev = pl.BlockSpec((hb, width), lambda i: (jnp.maximum(i * r - 1, 0), 0))
    main = pl.BlockSpec((tm, width), lambda i: (i, 0))
    nxt = pl.BlockSpec((hb, width), lambda i: (jnp.minimum((i + 1) * r, last), 0))
    return prev, main, nxt


def _mod_spec(tiles_per_seq):
    return pl.BlockSpec((None, 6, D_MODEL), lambda i: (i // tiles_per_seq, 0, 0))


def _rms_mod(x, nw, sc, sh):
    ms = jnp.mean(x * x, axis=-1, keepdims=True)
    return (x * lax.rsqrt(ms + EPS) * nw) * (1.0 + sc) + sh


def _fill_hin_ext(hin_ref, xp_ref, x_ref, xn_ref, nw, sc, sh, tiles_per_seq, hb, tm):
    i = pl.program_id(0)
    pos = lax.rem(i, tiles_per_seq)
    first = pos == 0
    last = pos == tiles_per_seq - 1
    hp = jnp.where(first, 0.0, _rms_mod(xp_ref[...], nw, sc, sh))
    hn = jnp.where(last, 0.0, _rms_mod(xn_ref[...], nw, sc, sh))
    hin_ref[0:hb, :] = hp.astype(hin_ref.dtype)
    hin_ref[hb:hb + tm, :] = _rms_mod(x_ref[...], nw, sc, sh).astype(hin_ref.dtype)
    hin_ref[hb + tm:hb + tm + hb, :] = hn.astype(hin_ref.dtype)


def _shift_rows(a, d, n_rows):
    if d == 0:
        return a
    return pltpu.roll(a, (-d) % n_rows, 0)


ADA_TN = 1536


def _ada_kernel(cond_ref, w_ref, b_ref, o_ref):
    s = jax.nn.silu(cond_ref[...])
    o_ref[...] = jnp.dot(s, w_ref[...], preferred_element_type=F32,
                         precision=lax.Precision.HIGHEST) + b_ref[...]


def ada_modulation(cond, ada_w, ada_b):
    depth, d, n = ada_w.shape
    rows = cond.shape[0]
    return pl.pallas_call(
        _ada_kernel,
        out_shape=jax.ShapeDtypeStruct((depth, rows, n), F32),
        grid=(depth, n // ADA_TN),
        in_specs=[
            pl.BlockSpec((rows, d), lambda l, j: (0, 0)),
            pl.BlockSpec((None, d, ADA_TN), lambda l, j: (l, 0, j)),
            pl.BlockSpec((None, 1, ADA_TN), lambda l, j: (l, 0, j)),
        ],
        out_specs=pl.BlockSpec((None, rows, ADA_TN), lambda l, j: (l, 0, j)),
        compiler_params=_cparams(2),
        name="ada_modulation",
    )(cond, ada_w, ada_b.reshape(depth, 1, n))


FFN_FC = 256
FFN_DOWN_GROUP = 6


def _ffn_kernel(xp_ref, x_ref, xn_ref, mod_ref, nw_ref, wg_ref, wu_ref, cw_ref, cb_ref, wd_ref,
                o_ref, hin_ref, act_ref, acc_ref, *, tm, tiles_per_seq):
    sh, sc, gt = mod_ref[3:4, :], mod_ref[4:5, :], mod_ref[5:6, :]
    _fill_hin_ext(hin_ref, xp_ref, x_ref, xn_ref, nw_ref[...], sc, sh, tiles_per_seq, HALO, tm)
    n_ext = tm + 2 * HALO
    n_chunks = FFN_DIM // FFN_FC
    group_start = 0
    for c in range(n_chunks):
        cols = slice(c * FFN_FC, (c + 1) * FFN_FC)
        g_ext = jnp.dot(hin_ref[...], wg_ref[:, cols], preferred_element_type=F32)
        u = jnp.dot(hin_ref[HALO:HALO + tm, :], wu_ref[:, cols], preferred_element_type=F32)
        g = cb_ref[:, cols]
        for k in range(FFN_CONV):
            g = g + cw_ref[k:k + 1, cols] * _shift_rows(g_ext, k - 1, n_ext)[HALO:HALO + tm, :]
        act_ref[:, cols] = (jax.nn.silu(g) * u).astype(BF16)
        if (c + 1) % FFN_DOWN_GROUP == 0 or c == n_chunks - 1:
            ks = slice(group_start * FFN_FC, (c + 1) * FFN_FC)
            y = jnp.dot(act_ref[:, ks], wd_ref[ks, :], preferred_element_type=F32)
            if group_start == 0:
                acc_ref[...] = y
            else:
                acc_ref[...] += y
            group_start = c + 1
    o_ref[...] = x_ref[...] + gt * acc_ref[...]


def conv_ffn(h, mod, nw, wg, wu, cw, cb, wd, *, seq, tm):
    n, d = h.shape
    n_tiles = n // tm
    tps = seq // tm
    prev, main, nxt = _halo_specs(tm, HALO, n_tiles, d)
    return pl.pallas_call(
        functools.partial(_ffn_kernel, tm=tm, tiles_per_seq=tps),
        out_shape=jax.ShapeDtypeStruct((n, d), F32),
        grid=(n_tiles,),
        in_specs=[prev, main, nxt, _mod_spec(tps), _resident((1, d)),
                  _resident(wg.shape), _resident(wu.shape), _resident(cw.shape),
                  _resident(cb.shape), _resident(wd.shape)],
        out_specs=pl.BlockSpec((tm, d), lambda i: (i, 0)),
        scratch_shapes=[pltpu.VMEM((tm + 2 * HALO, d), BF16), pltpu.VMEM((tm, FFN_DIM), BF16),
                        pltpu.VMEM((tm, d), F32)],
        compiler_params=_cparams(),
        name="conv_ffn",
    )(h, h, h, mod, nw, wg, wu, cw, cb, wd)


def _mm_res_kernel(a_ref, res_ref, mod_ref, w_ref, o_ref):
    gt = mod_ref[2:3, :]
    o_ref[...] = res_ref[...] + gt * jnp.dot(a_ref[...], w_ref[...], preferred_element_type=F32)


def matmul_residual(a, res, mod, w, *, seq, tm):
    n, k = a.shape
    d = w.shape[1]
    tps = seq // tm
    return pl.pallas_call(
        _mm_res_kernel,
        out_shape=jax.ShapeDtypeStruct((n, d), F32),
        grid=(n // tm,),
        in_specs=[pl.BlockSpec((tm, k), lambda i: (i, 0)), pl.BlockSpec((tm, d), lambda i: (i, 0)),
                  _mod_spec(tps), _resident(w.shape)],
        out_specs=pl.BlockSpec((tm, d), lambda i: (i, 0)),
        compiler_params=_cparams(),
        name="matmul_residual",
    )(a, res, mod, w)


DN_PC = 256
DN_STRIDE = 4


def _dn_proj_kernel(xp_ref, x_ref, xn_ref, mod_ref, nw_ref, wqkv_ref, cw_ref, wz_ref, wgt_ref,
                    aneg_ref, dtb_ref, q_ref, k_ref, v_ref, z_ref, bg_ref, hin_ref, p_ref, a_ref,
                    *, tm, tiles_per_seq):
    sh, sc = mod_ref[0:1, :], mod_ref[1:2, :]
    _fill_hin_ext(hin_ref, xp_ref, x_ref, xn_ref, nw_ref[...], sc, sh, tiles_per_seq, HALO, tm)
    outs = (q_ref, k_ref, v_ref)
    per_out = DN_DIM // DN_PC
    heads_pc = DN_PC // DN_HEAD_DIM
    rows = tm // DN_STRIDE
    for c in range(3 * per_out):
        cols = slice(c * DN_PC, (c + 1) * DN_PC)
        p_ext = jnp.dot(hin_ref[...], wqkv_ref[:, cols], preferred_element_type=F32)
        which, cc = divmod(c, per_out)
        for hh in range(heads_pc):
            slot = (c % 2) * heads_pc + hh
            lanes = slice(c * DN_PC + hh * DN_HEAD_DIM, c * DN_PC + (hh + 1) * DN_HEAD_DIM)
            p_ref[slot] = p_ext[:, hh * DN_HEAD_DIM:(hh + 1) * DN_HEAD_DIM]
            for v in range(DN_STRIDE):
                acc = None
                for k in range(DN_CONV):
                    start = HALO + v + k - DN_CONV // 2
                    term = cw_ref[k:k + 1, lanes] * p_ref[slot, pl.ds(start, rows, stride=DN_STRIDE), :]
                    acc = term if acc is None else acc + term
                ah = jax.nn.silu(acc)
                if which < 2:
                    ah = ah * lax.rsqrt(jnp.sum(ah * ah, axis=-1, keepdims=True) + EPS)
                    if which == 0:
                        ah = ah * (DN_HEAD_DIM ** -0.5)
                a_ref[slot, pl.ds(v, rows, stride=DN_STRIDE), :] = ah
            lo = cc * DN_PC + hh * DN_HEAD_DIM
            outs[which][:, lo:lo + DN_HEAD_DIM] = a_ref[slot].astype(BF16)
    hm = hin_ref[HALO:HALO + tm, :]
    z_ref[...] = jnp.dot(hm, wz_ref[...], preferred_element_type=F32).astype(BF16)
    gates = jnp.dot(hm, wgt_ref[...], preferred_element_type=F32)
    lane = lax.broadcasted_iota(jnp.int32, gates.shape, 1)
    beta = jax.nn.sigmoid(gates)
    g = aneg_ref[...] * jax.nn.softplus(gates + dtb_ref[...])
    bg_ref[...] = jnp.where(lane < 2 * DN_HEADS, beta, g)


def dn_project(h, mod, nw, wqkv, cw, wz, wgt, aneg, dtb, *, seq, tm):
    n, d = h.shape
    n_tiles = n // tm
    tps = seq // tm
    prev, main, nxt = _halo_specs(tm, HALO, n_tiles, d)
    row = lambda w: pl.BlockSpec((tm, w), lambda i: (i, 0))
    return pl.pallas_call(
        functools.partial(_dn_proj_kernel, tm=tm, tiles_per_seq=tps),
        out_shape=[jax.ShapeDtypeStruct((n, DN_DIM), BF16)] * 4 + [jax.ShapeDtypeStruct((n, LANES), F32)],
        grid=(n_tiles,),
        in_specs=[prev, main, nxt, _mod_spec(tps), _resident((1, d)), _resident(wqkv.shape),
                  _resident(cw.shape), _resident(wz.shape), _resident(wgt.shape),
                  _resident(aneg.shape), _resident(dtb.shape)],
        out_specs=[row(DN_DIM)] * 4 + [row(LANES)],
        scratch_shapes=[pltpu.VMEM((tm + 2 * HALO, d), BF16),
                        pltpu.VMEM((2 * DN_PC // DN_HEAD_DIM, tm + 2 * HALO, DN_HEAD_DIM), F32),
                        pltpu.VMEM((2 * DN_PC // DN_HEAD_DIM, tm, DN_HEAD_DIM), F32)],
        compiler_params=_cparams(),
        name="dn_project",
    )(h, h, h, mod, nw, wqkv, cw, wz, wgt, aneg, dtb)


def _bdot(a, b):
    return jnp.dot(a.astype(BF16), b.astype(BF16), preferred_element_type=F32)


def _unit_tri_inverses(lmats, eye, blk):
    c = lmats[0].shape[0]
    ns = [jnp.where(blk[0], -l, 0.0) for l in lmats]
    xs = [eye + n for n in ns]
    ps = [_bdot(n, n) for n in ns]
    for _ in range(2):
        rs = [_bdot(jnp.concatenate([x, p], axis=0), p) for x, p in zip(xs, ps)]
        xs = [x + r[:c] for x, r in zip(xs, rs)]
        ps = [r[c:] for r in rs]
    xs = [x + _bdot(x, p) for x, p in zip(xs, ps)]
    for lvl in range(1, len(blk)):
        sel = jnp.logical_and(blk[lvl], jnp.logical_not(blk[lvl - 1]))
        ts = [_bdot(jnp.where(sel, l, 0.0), x) for l, x in zip(lmats, xs)]
        xs = [x - _bdot(x, t) for x, t in zip(xs, ts)]
    return xs


def _dn_scan_kernel(qf_ref, kf_ref, vf_ref, bgf_ref, qb_ref, kb_ref, vb_ref, bgb_ref, s0_ref,
                    of_ref, ob_ref, sfin_ref, s_ref, *, n_chunks):
    n = pl.program_id(1)
    c = DN_CHUNK

    @pl.when(n == 0)
    def _():
        s_ref[...] = s0_ref[...]

    ri = lax.broadcasted_iota(jnp.int32, (c, c), 0)
    ci = lax.broadcasted_iota(jnp.int32, (c, c), 1)
    eye = (ri == ci).astype(F32)
    blk = []
    bs = 16
    while bs <= c:
        blk.append((ri // bs) == (ci // bs))
        bs *= 2

    dirs = ((qf_ref, kf_ref, vf_ref, bgf_ref, of_ref, ri >= ci, ri > ci),
            (qb_ref, kb_ref, vb_ref, bgb_ref, ob_ref, ri <= ci, ri < ci))
    inst = []
    for d, (q_ref, k_ref, v_ref, bg_ref, o_ref, incl, strict) in enumerate(dirs):
        bg = bg_ref[...]
        is_g = lax.broadcasted_iota(jnp.int32, bg.shape, 1) >= 2 * DN_HEADS
        bgg = jnp.where(is_g, bg, 0.0)
        cum = jnp.dot(incl.astype(F32), bgg, preferred_element_type=F32, precision=lax.Precision.HIGHEST)
        cum_t = cum.T
        tot = jnp.sum(bgg, axis=0, keepdims=True)
        egc = jnp.exp(cum)
        ekd = jnp.exp(tot - cum)
        etot = jnp.exp(tot)
        for h in range(DN_HEADS):
            lb = d * DN_HEADS + h
            lg = 2 * DN_HEADS + lb
            inst.append(dict(
                d=d, h=h, hs=slice(h * DN_HEAD_DIM, (h + 1) * DN_HEAD_DIM), q=q_ref, k=k_ref, v=v_ref, o=o_ref,
                incl=incl, strict=strict, beta=bg[:, lb:lb + 1], gc=cum[:, lg:lg + 1], gc_row=cum_t[lg:lg + 1, :],
                egc=egc[:, lg:lg + 1], ekd=ekd[:, lg:lg + 1], etot=etot[:, lg:lg + 1]))
    nt = (((1,), (1,)), ((), ()))
    tn = (((0,), (0,)), ((), ()))
    kq = [lax.dot_general(jnp.concatenate([t["k"][:, t["hs"]], t["q"][:, t["hs"]]], axis=0), t["k"][:, t["hs"]],
                          nt, preferred_element_type=F32) for t in inst]
    decays = [jnp.where(t["incl"], jnp.exp(jnp.where(t["incl"], t["gc"] - t["gc_row"], 0.0)), 0.0) for t in inst]
    lmats = [jnp.where(t["strict"], t["beta"] * r[:c] * dc, 0.0) for t, r, dc in zip(inst, kq, decays)]
    qks = [(r[c:] * dc).astype(BF16) for r, dc in zip(kq, decays)]
    tinvs = _unit_tri_inverses(lmats, eye, blk)
    rhss = [jnp.concatenate([t["v"][:, t["hs"]].astype(F32) * t["beta"],
                             t["k"][:, t["hs"]].astype(F32) * (t["beta"] * t["egc"])], axis=1) for t in inst]
    uws = [_bdot(ti, r) for ti, r in zip(tinvs, rhss)]
    wqs = [_bdot(jnp.concatenate([uw[:, DN_HEAD_DIM:], t["q"][:, t["hs"]].astype(F32) * t["egc"]], axis=0),
                 s_ref[t["d"], t["h"]]) for t, uw in zip(inst, uws)]
    v_news = [(uw[:, :DN_HEAD_DIM] - wq[:c]).astype(BF16) for uw, wq in zip(uws, wqs)]
    outs = [wq[c:] + jnp.dot(qk, vn, preferred_element_type=F32) for wq, qk, vn in zip(wqs, qks, v_news)]
    upds = [lax.dot_general((t["k"][:, t["hs"]].astype(F32) * t["ekd"]).astype(BF16), vn, tn,
                            preferred_element_type=F32) for t, vn in zip(inst, v_news)]
    for t, o, upd in zip(inst, outs, upds):
        t["o"][:, t["hs"]] = o.astype(t["o"].dtype)
        s_ref[t["d"], t["h"]] = s_ref[t["d"], t["h"]] * t["etot"] + upd

    @pl.when(n == n_chunks - 1)
    def _():
        sfin_ref[...] = s_ref[...]


def dn_scan(q, k, v, bg, s0, *, batch, seq):
    nc = seq // DN_CHUNK
    fwd = lambda w: pl.BlockSpec((DN_CHUNK, w), lambda b, n: (b * nc + n, 0))
    bwd = lambda w: pl.BlockSpec((DN_CHUNK, w), lambda b, n: (b * nc + nc - 1 - n, 0))
    st = pl.BlockSpec((None, 2, DN_HEADS, DN_HEAD_DIM, DN_HEAD_DIM), lambda b, n: (b, 0, 0, 0, 0))
    return pl.pallas_call(
        functools.partial(_dn_scan_kernel, n_chunks=nc),
        out_shape=[jax.ShapeDtypeStruct(q.shape, BF16), jax.ShapeDtypeStruct(q.shape, BF16),
                   jax.ShapeDtypeStruct(s0.shape, F32)],
        grid=(batch, nc),
        in_specs=[fwd(DN_DIM), fwd(DN_DIM), fwd(DN_DIM), fwd(LANES),
                  bwd(DN_DIM), bwd(DN_DIM), bwd(DN_DIM), bwd(LANES), st],
        out_specs=[fwd(DN_DIM), bwd(DN_DIM), st],
        scratch_shapes=[pltpu.VMEM((2, DN_HEADS, DN_HEAD_DIM, DN_HEAD_DIM), F32)],
        compiler_params=_cparams(2),
        name="dn_scan",
    )(q, k, v, bg, q, k, v, bg, s0)


def _dn_readout_kernel(of_ref, ob_ref, z_ref, res_ref, mod_ref, nw_ref, w_ref, o_ref, y_ref):
    gt = mod_ref[2:3, :]
    for h in range(DN_HEADS):
        hs = slice(h * DN_HEAD_DIM, (h + 1) * DN_HEAD_DIM)
        o = of_ref[:, hs].astype(F32) + ob_ref[:, hs].astype(F32)
        ms = jnp.mean(o * o, axis=-1, keepdims=True)
        y = o * lax.rsqrt(ms + EPS) * nw_ref[...]
        y_ref[:, hs] = (y * jax.nn.silu(z_ref[:, hs].astype(F32))).astype(BF16)
    o_ref[...] = res_ref[...] + gt * jnp.dot(y_ref[...], w_ref[...], preferred_element_type=F32)


def dn_readout(o_f, o_b, z, res, mod, nw, w, *, seq, tm):
    n, d = res.shape
    tps = seq // tm
    row = lambda w_: pl.BlockSpec((tm, w_), lambda i: (i, 0))
    return pl.pallas_call(
        _dn_readout_kernel,
        out_shape=jax.ShapeDtypeStruct((n, d), F32),
        grid=(n // tm,),
        in_specs=[row(DN_DIM), row(DN_DIM), row(DN_DIM), row(d), _mod_spec(tps),
                  _resident(nw.shape), _resident(w.shape)],
        out_specs=row(d),
        scratch_shapes=[pltpu.VMEM((tm, DN_DIM), BF16)],
        compiler_params=_cparams(),
        name="dn_readout",
    )(o_f, o_b, z, res, mod, nw, w)


def _pool_kernel(xp_ref, x_ref, xn_ref, mod_ref, nw_ref, wg_ref, scale_ref, o_ref, hin_ref, d_ref, y_ref,
                 *, tm, tiles_per_seq, seq):
    sh, sc, gt = mod_ref[0:1, :], mod_ref[1:2, :], mod_ref[2:3, :]
    hb = POOL_HALO
    pos = lax.rem(pl.program_id(0), tiles_per_seq)
    hp = jnp.where(pos == 0, 0.0, _rms_mod(xp_ref[...], nw_ref[...], sc, sh))
    hn = jnp.where(pos == tiles_per_seq - 1, 0.0, _rms_mod(xn_ref[...], nw_ref[...], sc, sh))
    hm = _rms_mod(x_ref[...], nw_ref[...], sc, sh)
    n_slab = D_MODEL // LANES
    for j in range(n_slab):
        lanes = slice(j * LANES, (j + 1) * LANES)
        hin_ref[j, 0:hb, :] = hp[:, lanes]
        hin_ref[j, hb:hb + tm, :] = hm[:, lanes]
        hin_ref[j, hb + tm:hb + tm + hb, :] = hn[:, lanes]
    rows = tm // POOL_STRIDE
    slabs_per_group = POOL_GROUP // LANES
    ti = pos * tm + POOL_STRIDE * lax.broadcasted_iota(jnp.int32, (rows, 1), 0)
    for gi, win in enumerate(POOL_WINDOWS):
        half = win // 2
        for v in range(POOL_STRIDE):
            t = ti + v
            cnt = (jnp.minimum(t + (win - half), seq) - jnp.maximum(t - half, 0)).astype(F32)
            for j in range(gi * slabs_per_group, (gi + 1) * slabs_per_group):
                s = None
                for dlt in range(-half, win - half):
                    term = hin_ref[j, pl.ds(hb + v + dlt, rows, stride=POOL_STRIDE), :]
                    s = term if s is None else s + term
                d_ref[j, pl.ds(v, rows, stride=POOL_STRIDE), :] = (
                    s / cnt - hin_ref[j, pl.ds(hb + v, rows, stride=POOL_STRIDE), :])
        cols = slice(gi * POOL_GROUP, (gi + 1) * POOL_GROUP)
        dd = jnp.concatenate([d_ref[j] for j in range(gi * slabs_per_group, (gi + 1) * slabs_per_group)], axis=1)
        y_ref[:, cols] = jnp.dot(dd.astype(BF16), wg_ref[gi], preferred_element_type=F32)
    o_ref[...] = x_ref[...] + gt * (y_ref[...] * scale_ref[...])


def pool_mixer(h, mod, nw, wg, scale, *, seq, tm):
    n, d = h.shape
    n_tiles = n // tm
    tps = seq // tm
    prev, main, nxt = _halo_specs(tm, POOL_HALO, n_tiles, d)
    return pl.pallas_call(
        functools.partial(_pool_kernel, tm=tm, tiles_per_seq=tps, seq=seq),
        out_shape=jax.ShapeDtypeStruct((n, d), F32),
        grid=(n_tiles,),
        in_specs=[prev, main, nxt, _mod_spec(tps), _resident((1, d)), _resident(wg.shape),
                  _resident(scale.shape)],
        out_specs=pl.BlockSpec((tm, d), lambda i: (i, 0)),
        scratch_shapes=[pltpu.VMEM((d // LANES, tm + 2 * POOL_HALO, LANES), F32),
                        pltpu.VMEM((d // LANES, tm, LANES), F32), pltpu.VMEM((tm, d), F32)],
        compiler_params=_cparams(),
        name="pool_mixer",
    )(h, h, h, mod, nw, wg, scale)


SWA_QKV_COLS = SWA_DIM + 4 * SWA_KV_DIM
SWA_NORM_COLS = SWA_DIM + 2 * SWA_KV_DIM
SWA_PC = 256


def _swa_qkv_kernel(x_ref, mod_ref, nw_ref, w_ref, hnw_ref, seg_ref, cos_ref, sin_ref,
                    q_ref, k_ref, v_ref, *, rope):
    sh, sc = mod_ref[0:1, :], mod_ref[1:2, :]
    hin = _rms_mod(x_ref[...], nw_ref[...], sc, sh).astype(BF16)
    pc = SWA_PC
    n_chunks = SWA_QKV_COLS // pc
    q_chunks = SWA_DIM // pc
    k_chunks = 2 * SWA_KV_DIM // pc
    for j in range(n_chunks):
        cols = slice(j * pc, (j + 1) * pc)
        p = jnp.dot(hin, w_ref[:, cols], preferred_element_type=F32)
        if j < q_chunks + k_chunks:
            ms = jnp.dot((p * p).astype(BF16), seg_ref[...], preferred_element_type=F32)
            p = p * lax.rsqrt(ms + EPS) * hnw_ref[:, cols]
            if rope:
                lane = lax.broadcasted_iota(jnp.int32, (p.shape[0], LANES), 1)
                up = lax.rem(lane, 32) < 16
                halves = []
                for t in range(pc // LANES):
                    ph = p[:, t * LANES:(t + 1) * LANES]
                    partner = jnp.where(up, pltpu.roll(ph, LANES - 16, 1), pltpu.roll(ph, 16, 1))
                    halves.append(ph * cos_ref[...] + partner * sin_ref[...])
                p = jnp.concatenate(halves, axis=1)
        if j < q_chunks:
            q_ref[:, cols] = (p * (SWA_HEAD_DIM ** -0.5)).astype(BF16)
        elif j < q_chunks + k_chunks:
            k_ref[:, (j - q_chunks) * pc:(j - q_chunks + 1) * pc] = p.astype(BF16)
        else:
            jj = j - q_chunks - k_chunks
            v_ref[:, jj * pc:(jj + 1) * pc] = p.astype(BF16)


def swa_qkv(h, mod, nw, w, hnw, seg, cos, sin, *, seq, tm, rope):
    n, d = h.shape
    tps = seq // tm
    row = lambda w_: pl.BlockSpec((tm, w_), lambda i: (i, 0))
    tab = pl.BlockSpec((tm, LANES), lambda i: (i % tps, 0))
    kvw = 2 * SWA_KV_DIM
    return pl.pallas_call(
        functools.partial(_swa_qkv_kernel, rope=rope),
        out_shape=[jax.ShapeDtypeStruct((n, SWA_DIM), BF16), jax.ShapeDtypeStruct((n, kvw), BF16),
                   jax.ShapeDtypeStruct((n, kvw), BF16)],
        grid=(n // tm,),
        in_specs=[row(d), _mod_spec(tps), _resident((1, d)), _resident(w.shape), _resident(hnw.shape),
                  _resident(seg.shape), tab, tab],
        out_specs=[row(SWA_DIM), row(kvw), row(kvw)],
        compiler_params=_cparams(),
        name="swa_qkv",
    )(h, mod, nw, w, hnw, seg, cos, sin)


def _swa_attn_kernel(q_ref, kp_ref, kc_ref, kn_ref, vp_ref, vc_ref, vn_ref, kx_ref, vx_ref, sink_ref,
                     o_ref, *, blocks_per_seq):
    i = pl.program_id(1)
    blk = SWA_BLOCK
    ri = lax.broadcasted_iota(jnp.int32, (SWA_GROUP * blk, blk), 0) % blk
    ci = lax.broadcasted_iota(jnp.int32, (SWA_GROUP * blk, blk), 1)
    ok_prev = jnp.logical_and(ci >= ri, i > 0)
    ok_next = jnp.logical_and(ci <= ri, i < blocks_per_seq - 1)
    lane = lax.broadcasted_iota(jnp.int32, (blk, LANES), 1)
    low = lane < SWA_HEAD_DIM
    nt = (((1,), (1,)), ((), ()))
    groups = range(SWA_KV_HEADS)
    gsl = [slice(g * LANES, (g + 1) * LANES) for g in groups]
    lane_tiles = lambda a: [a[:, t * LANES:(t + 1) * LANES] for t in range(a.shape[1] // LANES)]

    def stacked_queries(g):
        qs = []
        for t in range(2):
            qt = q_ref[:, (2 * g + t) * LANES:(2 * g + t + 1) * LANES]
            qs.append(jnp.where(low, qt, jnp.zeros_like(qt)))
            qs.append(jnp.where(low, jnp.zeros_like(qt), qt))
        return jnp.concatenate(qs, axis=0)

    def sink_column(g):
        return jnp.concatenate(
            [jnp.broadcast_to(sink_ref[SWA_GROUP * g + hh:SWA_GROUP * g + hh + 1, 0:1], (blk, 1))
             for hh in range(SWA_GROUP)], axis=0)

    qst = [stacked_queries(g) for g in groups]
    sinks = [sink_column(g) for g in groups]
    scores = [lax.dot_general(q, jnp.concatenate([kp_ref[:, gs], kc_ref[:, gs], kn_ref[:, gs], kx_ref[:, gs]],
                                                 axis=0), nt, preferred_element_type=F32)
              for q, gs in zip(qst, gsl)]
    pieces = []
    for s in scores:
        ts = lane_tiles(s)
        pieces.append([jnp.where(ok_prev, ts[0], NEG_INF), ts[1], jnp.where(ok_next, ts[2], NEG_INF)] + ts[3:])
    ms = [jnp.maximum(jnp.max(functools.reduce(jnp.maximum, ps), axis=-1, keepdims=True), sk)
          for ps, sk in zip(pieces, sinks)]
    es = [[jnp.exp(p - m) for p in ps] for ps, m in zip(pieces, ms)]
    invs = [1.0 / (jnp.sum(functools.reduce(jnp.add, e), axis=-1, keepdims=True) + jnp.exp(sk - m))
            for e, sk, m in zip(es, sinks, ms)]
    pvs = [jnp.dot(jnp.concatenate(e, axis=1).astype(BF16),
                   jnp.concatenate([vp_ref[:, gs], vc_ref[:, gs], vn_ref[:, gs], vx_ref[:, gs]], axis=0),
                   preferred_element_type=F32) * inv
           for e, gs, inv in zip(es, gsl, invs)]
    for g, pv in zip(groups, pvs):
        for t in range(2):
            a = pv[(2 * t) * blk:(2 * t + 1) * blk]
            b = pv[(2 * t + 1) * blk:(2 * t + 2) * blk]
            o_ref[:, (2 * g + t) * LANES:(2 * g + t + 1) * LANES] = jnp.where(low, a, b).astype(BF16)


def swa_attention(q, k2, v2, kx2, vx2, sink, *, batch, seq, ctx_len):
    nb = seq // SWA_BLOCK
    kvw = 2 * SWA_KV_DIM
    qspec = pl.BlockSpec((SWA_BLOCK, SWA_DIM), lambda b, i: (b * nb + i, 0))
    kv = lambda off: pl.BlockSpec(
        (SWA_BLOCK, kvw), lambda b, i: (b * nb + jnp.clip(i + off, 0, nb - 1), 0))
    xspec = pl.BlockSpec((ctx_len, kvw), lambda b, i: (b, 0))
    return pl.pallas_call(
        functools.partial(_swa_attn_kernel, blocks_per_seq=nb),
        out_shape=jax.ShapeDtypeStruct(q.shape, BF16),
        grid=(batch, nb),
        in_specs=[qspec, kv(-1), kv(0), kv(1), kv(-1), kv(0), kv(1), xspec, xspec, _resident(sink.shape)],
        out_specs=qspec,
        compiler_params=_cparams(2),
        name="swa_attention",
    )(q, k2, k2, k2, v2, v2, v2, kx2, vx2, sink)


CNV_RB = 32
CNV_PC = 256
CNV_STRIDE = 4


def _conv_module_kernel(xp_ref, x_ref, xn_ref, mod_ref, nw_ref, w1_ref, dw_ref, lnw_ref, lnb_ref, w2_ref,
                        o_ref, hin_ref, u_ref, c_ref, s_ref, *, tm, tiles_per_seq):
    sh, sc, gt = mod_ref[0:1, :], mod_ref[1:2, :], mod_ref[2:3, :]
    _fill_hin_ext(hin_ref, xp_ref, x_ref, xn_ref, nw_ref[...], sc, sh, tiles_per_seq, HALO, tm)
    d = D_MODEL
    n_slab = d // LANES
    for c in range(d // CNV_PC):
        cols = slice(c * CNV_PC, (c + 1) * CNV_PC)
        gcols = slice(d + c * CNV_PC, d + (c + 1) * CNV_PC)
        a1 = jnp.dot(hin_ref[...], w1_ref[:, cols], preferred_element_type=F32)
        a2 = jnp.dot(hin_ref[...], w1_ref[:, gcols], preferred_element_type=F32)
        u = a1 * jax.nn.sigmoid(a2)
        for t in range(CNV_PC // LANES):
            u_ref[c * (CNV_PC // LANES) + t] = u[:, t * LANES:(t + 1) * LANES]
    pad = CNV_WIDTH // 2
    rows = tm // CNV_STRIDE
    for j in range(n_slab):
        lanes = slice(j * LANES, (j + 1) * LANES)
        for v in range(CNV_STRIDE):
            acc = None
            for k in range(CNV_WIDTH):
                term = dw_ref[k:k + 1, lanes] * u_ref[j, pl.ds(HALO + v + k - pad, rows, stride=CNV_STRIDE), :]
                acc = term if acc is None else acc + term
            c_ref[j, pl.ds(v, rows, stride=CNV_STRIDE), :] = acc
    for rb in range(tm // CNV_RB):
        rsl = slice(rb * CNV_RB, (rb + 1) * CNV_RB)
        xs = [c_ref[j, rsl, :] for j in range(n_slab)]
        tot = xs[0]
        for xj in xs[1:]:
            tot = tot + xj
        mu = jnp.sum(tot, axis=-1, keepdims=True) * (1.0 / d)
        xcs = [xj - mu for xj in xs]
        sq = xcs[0] * xcs[0]
        for xc in xcs[1:]:
            sq = sq + xc * xc
        rstd = lax.rsqrt(jnp.sum(sq, axis=-1, keepdims=True) * (1.0 / d) + EPS)
        for j, xc in enumerate(xcs):
            lanes = slice(j * LANES, (j + 1) * LANES)
            y = xc * rstd * lnw_ref[:, lanes] + lnb_ref[:, lanes]
            s_ref[rsl, lanes] = jax.nn.silu(y).astype(BF16)
    o_ref[...] = x_ref[...] + gt * jnp.dot(s_ref[...], w2_ref[...], preferred_element_type=F32)


def conv_module(h, mod, nw, w1, dw, lnw, lnb, w2, *, seq, tm):
    n, d = h.shape
    n_tiles = n // tm
    tps = seq // tm
    prev, main, nxt = _halo_specs(tm, HALO, n_tiles, d)
    return pl.pallas_call(
        functools.partial(_conv_module_kernel, tm=tm, tiles_per_seq=tps),
        out_shape=jax.ShapeDtypeStruct((n, d), F32),
        grid=(n_tiles,),
        in_specs=[prev, main, nxt, _mod_spec(tps), _resident((1, d)), _resident(w1.shape),
                  _resident(dw.shape), _resident(lnw.shape), _resident(lnb.shape), _resident(w2.shape)],
        out_specs=pl.BlockSpec((tm, d), lambda i: (i, 0)),
        scratch_shapes=[pltpu.VMEM((tm + 2 * HALO, d), BF16),
                        pltpu.VMEM((d // LANES, tm + 2 * HALO, LANES), F32),
                        pltpu.VMEM((d // LANES, tm, LANES), F32),
                        pltpu.VMEM((tm, d), BF16)],
        compiler_params=_cparams(),
        name="conv_module",
    )(h, h, h, mod, nw, w1, dw, lnw, lnb, w2)


def _rope_tables(seq):
    n_freq = SWA_HEAD_DIM // 4
    inv_freq = ROPE_BASE ** (-jnp.arange(n_freq, dtype=F32) / n_freq)
    pos = jnp.arange(seq)
    rows = (pos // GRID_W).astype(F32)
    cols = (pos % GRID_W).astype(F32)
    ang_r = rows[:, None] * inv_freq
    ang_c = cols[:, None] * inv_freq
    ang = jnp.concatenate([ang_r, ang_r, ang_c, ang_c], axis=-1)
    sign = jnp.where((jnp.arange(SWA_HEAD_DIM) % 32) < 16, -1.0, 1.0)
    cos = jnp.tile(jnp.cos(ang), (1, LANES // SWA_HEAD_DIM))
    sin = jnp.tile(jnp.sin(ang) * sign, (1, LANES // SWA_HEAD_DIM))
    return cos, sin


def _tile_rows(seq):
    return min(seq, 512)


def kernel(x, c, ctx, c_ctx, ada_w, ada_b, norm_mix_w, norm_ffn_w, dn_w_in, dn_conv, dn_a_log, dn_dt_bias,
           dn_norm_w, dn_w_out, pool_w_grp, pool_scale, swa_w_qkv, swa_q_norm, swa_k_norm, swa_sink,
           swa_w_out, cnv_w_pw1, cnv_dw, cnv_ln_w, cnv_ln_b, cnv_w_pw2, ffn_w_gate, ffn_w_up, ffn_conv,
           ffn_conv_b, ffn_w_down):
    bsz, seq, d = x.shape
    ctx_len = ctx.shape[1]
    depth = ada_w.shape[0]
    h = x.reshape(bsz * seq, d)
    hc = ctx.reshape(bsz * ctx_len, d)
    p = dict(norm_mix_w=norm_mix_w, norm_ffn_w=norm_ffn_w, dn_w_in=dn_w_in, dn_conv=dn_conv,
             dn_a_log=dn_a_log, dn_dt_bias=dn_dt_bias, dn_norm_w=dn_norm_w, dn_w_out=dn_w_out,
             pool_w_grp=pool_w_grp, pool_scale=pool_scale, swa_w_qkv=swa_w_qkv, swa_q_norm=swa_q_norm,
             swa_k_norm=swa_k_norm, swa_sink=swa_sink, swa_w_out=swa_w_out, cnv_w_pw1=cnv_w_pw1,
             cnv_dw=cnv_dw, cnv_ln_w=cnv_ln_w, cnv_ln_b=cnv_ln_b, cnv_w_pw2=cnv_w_pw2,
             ffn_w_gate=ffn_w_gate, ffn_w_up=ffn_w_up, ffn_conv=ffn_conv, ffn_conv_b=ffn_conv_b,
             ffn_w_down=ffn_w_down)
    dims = dict(bsz=bsz, seq=seq, ctx_len=ctx_len, depth=depth)
    mods = modulation(c, c_ctx, ada_w, ada_b)
    for i in range(depth):
        mod_l, mod_c = layer_modulation(mods, i, bsz)
        h, hc = mixer_layer(i, h, hc, mod_l, mod_c, p, dims)
        h, hc = ffn_layer(i, h, hc, mod_l, mod_c, p, dims)
    return h.reshape(bsz, seq, d)


def modulation(c, c_ctx, ada_w, ada_b):
    bsz, d = c.shape
    n_cond = -(-(bsz + 1) // SUBLANES) * SUBLANES
    cond = jnp.zeros((n_cond, d), F32).at[:bsz].set(c).at[bsz].set(c_ctx)
    return ada_modulation(cond, ada_w, ada_b)


def layer_modulation(mods, i, bsz):
    d = mods.shape[-1] // 6
    mod_l = mods[i, :bsz].reshape(bsz, 6, d)
    mod_c = jnp.broadcast_to(mods[i, bsz].reshape(1, 6, d), (bsz, 6, d))
    return mod_l, mod_c


def _keeps_ctx(i, depth):
    return any((j % N_MIXERS) in CTX_READING_MIXERS for j in range(i + 1, depth))


def ffn_layer(i, h, hc, mod_l, mod_c, p, dims):
    d = h.shape[-1]
    seq, ctx_len = dims["seq"], dims["ctx_len"]
    ffn_args = (p["norm_ffn_w"][i].reshape(1, d), p["ffn_w_gate"][i].astype(BF16),
                p["ffn_w_up"][i].astype(BF16), p["ffn_conv"][i], p["ffn_conv_b"][i].reshape(1, FFN_DIM),
                p["ffn_w_down"][i].astype(BF16))
    h = conv_ffn(h, mod_l, *ffn_args, seq=seq, tm=_tile_rows(seq))
    if _keeps_ctx(i, dims["depth"]):
        hc = conv_ffn(hc, mod_c, *ffn_args, seq=ctx_len, tm=_tile_rows(ctx_len))
    return h, hc


def mixer_layer(i, h, hc, mod_l, mod_c, p, dims):
    d = h.shape[-1]
    bsz, seq, ctx_len = dims["bsz"], dims["seq"], dims["ctx_len"]
    tm_c = _tile_rows(ctx_len)
    kind, slot = i % N_MIXERS, i // N_MIXERS
    keep_ctx = _keeps_ctx(i, dims["depth"])
    nw_mix = p["norm_mix_w"][i].reshape(1, d)
    lat = dict(seq=seq, tm=_tile_rows(seq))
    cx = dict(seq=ctx_len, tm=tm_c)

    if kind == MIX_DELTANET:
        w_in = p["dn_w_in"][slot]
        wqkv = w_in[:, :3 * DN_DIM].astype(BF16)
        wz = w_in[:, 3 * DN_DIM:4 * DN_DIM].astype(BF16)
        wgt = jnp.zeros((d, LANES), F32).at[:, :4 * DN_HEADS].set(w_in[:, 4 * DN_DIM:]).astype(BF16)
        aneg = jnp.zeros((1, LANES), F32).at[0, 2 * DN_HEADS:4 * DN_HEADS].set(
            -jnp.exp(p["dn_a_log"][slot].astype(F32)).reshape(-1))
        dtb = jnp.zeros((1, LANES), F32).at[0, 2 * DN_HEADS:4 * DN_HEADS].set(
            p["dn_dt_bias"][slot].astype(F32).reshape(-1))
        hnw = p["dn_norm_w"][slot].reshape(1, DN_HEAD_DIM)
        wout = p["dn_w_out"][slot].astype(BF16)
        proj = functools.partial(dn_project, nw=nw_mix, wqkv=wqkv, cw=p["dn_conv"][slot], wz=wz, wgt=wgt,
                                 aneg=aneg, dtb=dtb)
        qc, kc, vc, zc, bgc = proj(hc, mod_c, **cx)
        ql, kl, vl, zl, bgl = proj(h, mod_l, **lat)
        s0 = jnp.zeros((bsz, 2, DN_HEADS, DN_HEAD_DIM, DN_HEAD_DIM), F32)
        ocf, ocb, s_ctx = dn_scan(qc, kc, vc, bgc, s0, batch=bsz, seq=ctx_len)
        olf, olb, _ = dn_scan(ql, kl, vl, bgl, s_ctx, batch=bsz, seq=seq)
        h = dn_readout(olf, olb, zl, h, mod_l, hnw, wout, **lat)
        if keep_ctx:
            hc = dn_readout(ocf, ocb, zc, hc, mod_c, hnw, wout, **cx)
    elif kind == MIX_POOL:
        wg = p["pool_w_grp"][slot].astype(BF16)
        scale = p["pool_scale"][slot].reshape(1, d)
        h = pool_mixer(h, mod_l, nw_mix, wg, scale, **lat)
        if keep_ctx:
            hc = pool_mixer(hc, mod_c, nw_mix, wg, scale, **cx)
    elif kind == MIX_SWA:
        assert not keep_ctx
        wq = p["swa_w_qkv"][slot]
        dup = lambda w_: jnp.repeat(w_.reshape(d, SWA_KV_HEADS, 1, SWA_HEAD_DIM), 2, axis=2).reshape(
            d, 2 * SWA_KV_DIM)
        w_all = jnp.concatenate([wq[:, :SWA_DIM], dup(wq[:, SWA_DIM:SWA_DIM + SWA_KV_DIM]),
                                 dup(wq[:, SWA_DIM + SWA_KV_DIM:])], axis=1).astype(BF16)
        hnw = jnp.concatenate([jnp.tile(p["swa_q_norm"][slot], SWA_HEADS),
                               jnp.tile(p["swa_k_norm"][slot], 2 * SWA_KV_HEADS)]).reshape(1, SWA_NORM_COLS)
        li = jnp.arange(SWA_PC)
        seg = jnp.where((li[:, None] // SWA_HEAD_DIM) == (li[None, :] // SWA_HEAD_DIM),
                        1.0 / SWA_HEAD_DIM, 0.0).astype(BF16)
        cos, sin = _rope_tables(seq)
        sink = jnp.broadcast_to(p["swa_sink"][slot].astype(F32).reshape(SWA_HEADS, 1), (SWA_HEADS, LANES))
        ql, k2, v2 = swa_qkv(h, mod_l, nw_mix, w_all, hnw, seg, cos, sin, rope=True, **lat)
        _, kx2, vx2 = swa_qkv(hc, mod_c, nw_mix, w_all, hnw, seg, cos[:tm_c], sin[:tm_c], rope=False, **cx)
        o = swa_attention(ql, k2, v2, kx2, vx2, sink, batch=bsz, seq=seq, ctx_len=ctx_len)
        h = matmul_residual(o, h, mod_l, p["swa_w_out"][slot].astype(BF16), **lat)
    else:
        args = (nw_mix, p["cnv_w_pw1"][slot].astype(BF16), p["cnv_dw"][slot], p["cnv_ln_w"][slot].reshape(1, d),
                p["cnv_ln_b"][slot].reshape(1, d), p["cnv_w_pw2"][slot].astype(BF16))
        h = conv_module(h, mod_l, *args, **lat)
        if keep_ctx:
            hc = conv_module(hc, mod_c, *args, **cx)
    return h, hc
```

```python
import functools

import jax
import jax.numpy as jnp
from jax import lax
from jax.experimental import pallas as pl
from jax.experimental.pallas import tpu as pltpu

D_MODEL = 1024
EPS = 1e-6
NEG_INF = -1e30
GRID_W = 64
ROPE_BASE = 10000.0
N_MIXERS = 4
MIX_DELTANET, MIX_POOL, MIX_SWA, MIX_CONV = 0, 1, 2, 3
CTX_READING_MIXERS = (MIX_DELTANET, MIX_SWA)

DN_HEADS = 8
DN_HEAD_DIM = 128
DN_DIM = DN_HEADS * DN_HEAD_DIM
DN_CONV = 5
DN_CHUNK = 128
DN_INV_BASE = 16
DN_SUB = 2

POOL_WINDOWS = (2, 4, 8, 16)
POOL_GROUP = D_MODEL // len(POOL_WINDOWS)

SWA_HEADS = 16
SWA_KV_HEADS = 4
SWA_HEAD_DIM = 64
SWA_GROUP = SWA_HEADS // SWA_KV_HEADS
SWA_DIM = SWA_HEADS * SWA_HEAD_DIM
SWA_KV_DIM = SWA_KV_HEADS * SWA_HEAD_DIM
SWA_WINDOW = 128
SWA_BLOCK = 128
SWA_SUB = 2

CNV_WIDTH = 31
FFN_DIM = 2816
FFN_CONV = 3

LANES = 128
SUBLANES = 8
VMEM_LIMIT_BYTES = 56 * 1024 * 1024

F32 = jnp.float32
BF16 = jnp.bfloat16
HALO = 2 * SUBLANES
POOL_HALO = SUBLANES
POOL_STRIDE = 4


def _cparams(n_axes=1):
    return pltpu.CompilerParams(
        dimension_semantics=("arbitrary",) * n_axes,
        vmem_limit_bytes=VMEM_LIMIT_BYTES,
    )


def _resident(shape):
    nd = len(shape)
    return pl.BlockSpec(shape, lambda *_: (0,) * nd, pipeline_mode=pl.Buffered(1))


def _halo_specs(tm, hb, n_tiles, width):
    r = tm // hb
    last = n_tiles * r - 1
    prev = pl.BlockSpec((hb, width), lambda i: (jnp.maximum(i * r - 1, 0), 0))
    main = pl.BlockSpec((tm, width), lambda i: (i, 0))
    nxt = pl.BlockSpec((hb, width), lambda i: (jnp.minimum((i + 1) * r, last), 0))
    return prev, main, nxt


def _mod_spec(tiles_per_seq):
    return pl.BlockSpec((None, 6, D_MODEL), lambda i: (i // tiles_per_seq, 0, 0))


def _rms_mod(x, nw, sc, sh):
    ms = jnp.mean(x * x, axis=-1, keepdims=True)
    return (x * lax.rsqrt(ms + EPS) * nw) * (1.0 + sc) + sh


def _fill_hin_ext(hin_ref, xp_ref, x_ref, xn_ref, nw, sc, sh, tiles_per_seq, hb, tm):
    i = pl.program_id(0)
    pos = lax.rem(i, tiles_per_seq)
    first = pos == 0
    last = pos == tiles_per_seq - 1
    hp = jnp.where(first, 0.0, _rms_mod(xp_ref[...], nw, sc, sh))
    hn = jnp.where(last, 0.0, _rms_mod(xn_ref[...], nw, sc, sh))
    hin_ref[0:hb, :] = hp.astype(hin_ref.dtype)
    hin_ref[hb:hb + tm, :] = _rms_mod(x_ref[...], nw, sc, sh).astype(hin_ref.dtype)
    hin_ref[hb + tm:hb + tm + hb, :] = hn.astype(hin_ref.dtype)


def _shift_rows(a, d, n_rows):
    if d == 0:
        return a
    return pltpu.roll(a, (-d) % n_rows, 0)


ADA_TN = 1536


def _ada_kernel(cond_ref, w_ref, b_ref, o_ref):
    s = jax.nn.silu(cond_ref[...])
    o_ref[...] = jnp.dot(s, w_ref[...], preferred_element_type=F32,
                         precision=lax.Precision.HIGHEST) + b_ref[...]


def ada_modulation(cond, ada_w, ada_b):
    depth, d, n = ada_w.shape
    rows = cond.shape[0]
    return pl.pallas_call(
        _ada_kernel,
        out_shape=jax.ShapeDtypeStruct((depth, rows, n), F32),
        grid=(depth, n // ADA_TN),
        in_specs=[
            pl.BlockSpec((rows, d), lambda l, j: (0, 0)),
            pl.BlockSpec((None, d, ADA_TN), lambda l, j: (l, 0, j)),
            pl.BlockSpec((None, 1, ADA_TN), lambda l, j: (l, 0, j)),
        ],
        out_specs=pl.BlockSpec((None, rows, ADA_TN), lambda l, j: (l, 0, j)),
        compiler_params=_cparams(2),
        name="ada_modulation",
    )(cond, ada_w, ada_b.reshape(depth, 1, n))


FFN_FC = 256
FFN_DOWN_GROUP = 6


def _ffn_kernel(xp_ref, x_ref, xn_ref, mod_ref, nw_ref, wg_ref, wu_ref, cw_ref, cb_ref, wd_ref,
                o_ref, hin_ref, act_ref, acc_ref, *, tm, tiles_per_seq):
    sh, sc, gt = mod_ref[3:4, :], mod_ref[4:5, :], mod_ref[5:6, :]
    _fill_hin_ext(hin_ref, xp_ref, x_ref, xn_ref, nw_ref[...], sc, sh, tiles_per_seq, HALO, tm)
    n_ext = tm + 2 * HALO
    n_chunks = FFN_DIM // FFN_FC
    group_start = 0
    for c in range(n_chunks):
        cols = slice(c * FFN_FC, (c + 1) * FFN_FC)
        g_ext = jnp.dot(hin_ref[...], wg_ref[:, cols], preferred_element_type=F32)
        u = jnp.dot(hin_ref[HALO:HALO + tm, :], wu_ref[:, cols], preferred_element_type=F32)
        g = cb_ref[:, cols]
        for k in range(FFN_CONV):
            g = g + cw_ref[k:k + 1, cols] * _shift_rows(g_ext, k - 1, n_ext)[HALO:HALO + tm, :]
        act_ref[:, cols] = (jax.nn.silu(g) * u).astype(BF16)
        if (c + 1) % FFN_DOWN_GROUP == 0 or c == n_chunks - 1:
            ks = slice(group_start * FFN_FC, (c + 1) * FFN_FC)
            y = jnp.dot(act_ref[:, ks], wd_ref[ks, :], preferred_element_type=F32)
            if group_start == 0:
                acc_ref[...] = y
            else:
                acc_ref[...] += y
            group_start = c + 1
    o_ref[...] = x_ref[...] + gt * acc_ref[...]


def conv_ffn(h, mod, nw, wg, wu, cw, cb, wd, *, seq, tm):
    n, d = h.shape
    n_tiles = n // tm
    tps = seq // tm
    prev, main, nxt = _halo_specs(tm, HALO, n_tiles, d)
    return pl.pallas_call(
        functools.partial(_ffn_kernel, tm=tm, tiles_per_seq=tps),
        out_shape=jax.ShapeDtypeStruct((n, d), F32),
        grid=(n_tiles,),
        in_specs=[prev, main, nxt, _mod_spec(tps), _resident((1, d)),
                  _resident(wg.shape), _resident(wu.shape), _resident(cw.shape),
                  _resident(cb.shape), _resident(wd.shape)],
        out_specs=pl.BlockSpec((tm, d), lambda i: (i, 0)),
        scratch_shapes=[pltpu.VMEM((tm + 2 * HALO, d), BF16), pltpu.VMEM((tm, FFN_DIM), BF16),
                        pltpu.VMEM((tm, d), F32)],
        compiler_params=_cparams(),
        name="conv_ffn",
    )(h, h, h, mod, nw, wg, wu, cw, cb, wd)


def _mm_res_kernel(a_ref, res_ref, mod_ref, w_ref, o_ref):
    gt = mod_ref[2:3, :]
    o_ref[...] = res_ref[...] + gt * jnp.dot(a_ref[...], w_ref[...], preferred_element_type=F32)


def matmul_residual(a, res, mod, w, *, seq, tm):
    n, k = a.shape
    d = w.shape[1]
    tps = seq // tm
    return pl.pallas_call(
        _mm_res_kernel,
        out_shape=jax.ShapeDtypeStruct((n, d), F32),
        grid=(n // tm,),
        in_specs=[pl.BlockSpec((tm, k), lambda i: (i, 0)), pl.BlockSpec((tm, d), lambda i: (i, 0)),
                  _mod_spec(tps), _resident(w.shape)],
        out_specs=pl.BlockSpec((tm, d), lambda i: (i, 0)),
        compiler_params=_cparams(),
        name="matmul_residual",
    )(a, res, mod, w)


DN_PC = 256
DN_STRIDE = 4


def _dn_proj_kernel(xp_ref, x_ref, xn_ref, mod_ref, nw_ref, wqkv_ref, cw_ref, wz_ref, wgt_ref,
                    aneg_ref, dtb_ref, q_ref, k_ref, v_ref, z_ref, bg_ref, hin_ref, p_ref, a_ref,
                    *, tm, tiles_per_seq):
    sh, sc = mod_ref[0:1, :], mod_ref[1:2, :]
    _fill_hin_ext(hin_ref, xp_ref, x_ref, xn_ref, nw_ref[...], sc, sh, tiles_per_seq, HALO, tm)
    outs = (q_ref, k_ref, v_ref)
    per_out = DN_DIM // DN_PC
    heads_pc = DN_PC // DN_HEAD_DIM
    rows = tm // DN_STRIDE
    for c in range(3 * per_out):
        cols = slice(c * DN_PC, (c + 1) * DN_PC)
        p_ext = jnp.dot(hin_ref[...], wqkv_ref[:, cols], preferred_element_type=F32)
        which, cc = divmod(c, per_out)
        for hh in range(heads_pc):
            slot = (c % 2) * heads_pc + hh
            lanes = slice(c * DN_PC + hh * DN_HEAD_DIM, c * DN_PC + (hh + 1) * DN_HEAD_DIM)
            p_ref[slot] = p_ext[:, hh * DN_HEAD_DIM:(hh + 1) * DN_HEAD_DIM]
            for v in range(DN_STRIDE):
                acc = None
                for k in range(DN_CONV):
                    start = HALO + v + k - DN_CONV // 2
                    term = cw_ref[k:k + 1, lanes] * p_ref[slot, pl.ds(start, rows, stride=DN_STRIDE), :]
                    acc = term if acc is None else acc + term
                ah = jax.nn.silu(acc)
                if which < 2:
                    ah = ah * lax.rsqrt(jnp.sum(ah * ah, axis=-1, keepdims=True) + EPS)
                    if which == 0:
                        ah = ah * (DN_HEAD_DIM ** -0.5)
                a_ref[slot, pl.ds(v, rows, stride=DN_STRIDE), :] = ah
            lo = cc * DN_PC + hh * DN_HEAD_DIM
            outs[which][:, lo:lo + DN_HEAD_DIM] = a_ref[slot].astype(BF16)
    hm = hin_ref[HALO:HALO + tm, :]
    z_ref[...] = jnp.dot(hm, wz_ref[...], preferred_element_type=F32).astype(BF16)
    gates = jnp.dot(hm, wgt_ref[...], preferred_element_type=F32)
    lane = lax.broadcasted_iota(jnp.int32, gates.shape, 1)
    beta = jax.nn.sigmoid(gates)
    g = aneg_ref[...] * jax.nn.softplus(gates + dtb_ref[...])
    bg_ref[...] = jnp.where(lane < 2 * DN_HEADS, beta, g)


def dn_project(h, mod, nw, wqkv, cw, wz, wgt, aneg, dtb, *, seq, tm):
    n, d = h.shape
    n_tiles = n // tm
    tps = seq // tm
    prev, main, nxt = _halo_specs(tm, HALO, n_tiles, d)
    row = lambda w: pl.BlockSpec((tm, w), lambda i: (i, 0))
    return pl.pallas_call(
        functools.partial(_dn_proj_kernel, tm=tm, tiles_per_seq=tps),
        out_shape=[jax.ShapeDtypeStruct((n, DN_DIM), BF16)] * 4 + [jax.ShapeDtypeStruct((n, LANES), F32)],
        grid=(n_tiles,),
        in_specs=[prev, main, nxt, _mod_spec(tps), _resident((1, d)), _resident(wqkv.shape),
                  _resident(cw.shape), _resident(wz.shape), _resident(wgt.shape),
                  _resident(aneg.shape), _resident(dtb.shape)],
        out_specs=[row(DN_DIM)] * 4 + [row(LANES)],
        scratch_shapes=[pltpu.VMEM((tm + 2 * HALO, d), BF16),
                        pltpu.VMEM((2 * DN_PC // DN_HEAD_DIM, tm + 2 * HALO, DN_HEAD_DIM), F32),
                        pltpu.VMEM((2 * DN_PC // DN_HEAD_DIM, tm, DN_HEAD_DIM), F32)],
        compiler_params=_cparams(),
        name="dn_project",
    )(h, h, h, mod, nw, wqkv, cw, wz, wgt, aneg, dtb)


def _bdot(a, b):
    return jnp.dot(a.astype(BF16), b.astype(BF16), preferred_element_type=F32)


def _unit_tri_inverses(lmats, lowers, eye, blk):
    c = lmats[0].shape[0]
    ns = [jnp.where(blk[0], -l, 0.0) for l in lmats]
    xs = [eye + n for n in ns]
    ps = [_bdot(n, n) for n in ns]
    for _ in range(2):
        rs = [_bdot(jnp.concatenate([x, p], axis=0), p) for x, p in zip(xs, ps)]
        xs = [x + r[:c] for x, r in zip(xs, rs)]
        ps = [r[c:] for r in rs]
    xs = [x + _bdot(x, p) for x, p in zip(xs, ps)]
    size = DN_INV_BASE
    for lvl in range(1, len(blk)):
        sel = jnp.logical_and(blk[lvl], jnp.logical_not(blk[lvl - 1]))
        groups = range(c // size)

        def active(a, lower):
            return jnp.concatenate([a[g * size:(g + 1) * size] for g in groups if (g % 2 == 1) == lower], axis=0)

        def expand(a, lower, rest):
            parts, k = [], 0
            for g in groups:
                if (g % 2 == 1) == lower:
                    parts.append(a[k * size:(k + 1) * size])
                    k += 1
                else:
                    parts.append(rest[g * size:(g + 1) * size])
            return jnp.concatenate(parts, axis=0)

        zero = jnp.zeros_like(eye)
        ts = [_bdot(active(jnp.where(sel, l, 0.0), lo), x) for l, lo, x in zip(lmats, lowers, xs)]
        us = [_bdot(active(x, lo), expand(t, lo, zero)) for x, lo, t in zip(xs, lowers, ts)]
        xs = [expand(active(x, lo) - u, lo, x) for x, lo, u in zip(xs, lowers, us)]
        size *= 2
    return xs


def _dn_scan_kernel(qf_ref, kf_ref, vf_ref, bgf_ref, qb_ref, kb_ref, vb_ref, bgb_ref, s0_ref,
                    of_ref, ob_ref, sfin_ref, s_ref, *, n_steps):
    n = pl.program_id(1)
    c = DN_CHUNK

    @pl.when(n == 0)
    def _():
        s_ref[...] = s0_ref[...]

    ri = lax.broadcasted_iota(jnp.int32, (c, c), 0)
    ci = lax.broadcasted_iota(jnp.int32, (c, c), 1)
    eye = (ri == ci).astype(F32)
    blk = []
    bs = DN_INV_BASE
    while bs <= c:
        blk.append((ri // bs) == (ci // bs))
        bs *= 2

    dirs = ((qf_ref, kf_ref, vf_ref, bgf_ref, of_ref, ri >= ci, ri > ci),
            (qb_ref, kb_ref, vb_ref, bgb_ref, ob_ref, ri <= ci, ri < ci))
    inst = []
    for d, (q_ref, k_ref, v_ref, bg_ref, o_ref, incl, strict) in enumerate(dirs):
        for sub in range(DN_SUB):
            rows = slice(sub * c, (sub + 1) * c)
            bg = bg_ref[rows, :]
            is_g = lax.broadcasted_iota(jnp.int32, bg.shape, 1) >= 2 * DN_HEADS
            bgg = jnp.where(is_g, bg, 0.0)
            cum = jnp.dot(incl.astype(F32), bgg, preferred_element_type=F32, precision=lax.Precision.HIGHEST)
            cum_t = cum.T
            tot = jnp.sum(bgg, axis=0, keepdims=True)
            egc = jnp.exp(cum)
            ekd = jnp.exp(tot - cum)
            etot = jnp.exp(tot)
            for h in range(DN_HEADS):
                lb = d * DN_HEADS + h
                lg = 2 * DN_HEADS + lb
                hs = slice(h * DN_HEAD_DIM, (h + 1) * DN_HEAD_DIM)
                inst.append(dict(
                    d=d, h=h, sub=sub, o=o_ref, rows=rows, hs=hs, incl=incl, strict=strict,
                    q=q_ref[rows, hs], k=k_ref[rows, hs], v=v_ref[rows, hs],
                    beta=bg[:, lb:lb + 1], gc=cum[:, lg:lg + 1], gc_row=cum_t[lg:lg + 1, :],
                    egc=egc[:, lg:lg + 1], ekd=ekd[:, lg:lg + 1], etot=etot[:, lg:lg + 1]))
    nt = (((1,), (1,)), ((), ()))
    tn = (((0,), (0,)), ((), ()))
    kq = [lax.dot_general(jnp.concatenate([t["k"], t["q"]], axis=0), t["k"], nt, preferred_element_type=F32)
          for t in inst]
    decays = [jnp.where(t["incl"], jnp.exp(jnp.where(t["incl"], t["gc"] - t["gc_row"], 0.0)), 0.0) for t in inst]
    lmats = [jnp.where(t["strict"], t["beta"] * r[:c] * dc, 0.0) for t, r, dc in zip(inst, kq, decays)]
    qks = [(r[c:] * dc).astype(BF16) for r, dc in zip(kq, decays)]
    tinvs = _unit_tri_inverses(lmats, [t["d"] == 0 for t in inst], eye, blk)
    rhss = [jnp.concatenate([t["v"].astype(F32) * t["beta"], t["k"].astype(F32) * (t["beta"] * t["egc"])], axis=1)
            for t in inst]
    uws = [_bdot(ti, r) for ti, r in zip(tinvs, rhss)]
    wq_lhs = [jnp.concatenate([uw[:, DN_HEAD_DIM:], t["q"].astype(F32) * t["egc"]], axis=0).astype(BF16)
              for t, uw in zip(inst, uws)]
    k_decs = [(t["k"].astype(F32) * t["ekd"]).astype(BF16) for t in inst]
    for pos in range(DN_SUB):
        cur = [i for i, t in enumerate(inst) if t["sub"] == (pos if t["d"] == 0 else DN_SUB - 1 - pos)]
        wqs = [jnp.dot(wq_lhs[i], s_ref[inst[i]["d"], inst[i]["h"]].astype(BF16), preferred_element_type=F32)
               for i in cur]
        v_news = [(uws[i][:, :DN_HEAD_DIM] - wq[:c]).astype(BF16) for i, wq in zip(cur, wqs)]
        outs = [wq[c:] + jnp.dot(qks[i], vn, preferred_element_type=F32) for i, wq, vn in zip(cur, wqs, v_news)]
        upds = [lax.dot_general(k_decs[i], vn, tn, preferred_element_type=F32) for i, vn in zip(cur, v_news)]
        for i, o, upd in zip(cur, outs, upds):
            t = inst[i]
            t["o"][t["rows"], t["hs"]] = o.astype(t["o"].dtype)
            s_ref[t["d"], t["h"]] = s_ref[t["d"], t["h"]] * t["etot"] + upd

    @pl.when(n == n_steps - 1)
    def _():
        sfin_ref[...] = s_ref[...]


def dn_scan(q, k, v, bg, s0, *, batch, seq):
    rows = DN_SUB * DN_CHUNK
    nc = seq // rows
    fwd = lambda w: pl.BlockSpec((rows, w), lambda b, n: (b * nc + n, 0))
    bwd = lambda w: pl.BlockSpec((rows, w), lambda b, n: (b * nc + nc - 1 - n, 0))
    st = pl.BlockSpec((None, 2, DN_HEADS, DN_HEAD_DIM, DN_HEAD_DIM), lambda b, n: (b, 0, 0, 0, 0))
    return pl.pallas_call(
        functools.partial(_dn_scan_kernel, n_steps=nc),
        out_shape=[jax.ShapeDtypeStruct(q.shape, BF16), jax.ShapeDtypeStruct(q.shape, BF16),
                   jax.ShapeDtypeStruct(s0.shape, F32)],
        grid=(batch, nc),
        in_specs=[fwd(DN_DIM), fwd(DN_DIM), fwd(DN_DIM), fwd(LANES),
                  bwd(DN_DIM), bwd(DN_DIM), bwd(DN_DIM), bwd(LANES), st],
        out_specs=[fwd(DN_DIM), bwd(DN_DIM), st],
        scratch_shapes=[pltpu.VMEM((2, DN_HEADS, DN_HEAD_DIM, DN_HEAD_DIM), F32)],
        compiler_params=_cparams(2),
        name="dn_scan",
    )(q, k, v, bg, q, k, v, bg, s0)


def _dn_readout_kernel(of_ref, ob_ref, z_ref, res_ref, mod_ref, nw_ref, w_ref, o_ref, y_ref):
    gt = mod_ref[2:3, :]
    for h in range(DN_HEADS):
        hs = slice(h * DN_HEAD_DIM, (h + 1) * DN_HEAD_DIM)
        o = of_ref[:, hs].astype(F32) + ob_ref[:, hs].astype(F32)
        ms = jnp.mean(o * o, axis=-1, keepdims=True)
        y = o * lax.rsqrt(ms + EPS) * nw_ref[...]
        y_ref[:, hs] = (y * jax.nn.silu(z_ref[:, hs].astype(F32))).astype(BF16)
    o_ref[...] = res_ref[...] + gt * jnp.dot(y_ref[...], w_ref[...], preferred_element_type=F32)


def dn_readout(o_f, o_b, z, res, mod, nw, w, *, seq, tm):
    n, d = res.shape
    tps = seq // tm
    row = lambda w_: pl.BlockSpec((tm, w_), lambda i: (i, 0))
    return pl.pallas_call(
        _dn_readout_kernel,
        out_shape=jax.ShapeDtypeStruct((n, d), F32),
        grid=(n // tm,),
        in_specs=[row(DN_DIM), row(DN_DIM), row(DN_DIM), row(d), _mod_spec(tps),
                  _resident(nw.shape), _resident(w.shape)],
        out_specs=row(d),
        scratch_shapes=[pltpu.VMEM((tm, DN_DIM), BF16)],
        compiler_params=_cparams(),
        name="dn_readout",
    )(o_f, o_b, z, res, mod, nw, w)


def _pool_kernel(xp_ref, x_ref, xn_ref, mod_ref, nw_ref, wg_ref, scale_ref, o_ref, hin_ref, d_ref, y_ref,
                 *, tm, tiles_per_seq, seq):
    sh, sc, gt = mod_ref[0:1, :], mod_ref[1:2, :], mod_ref[2:3, :]
    hb = POOL_HALO
    pos = lax.rem(pl.program_id(0), tiles_per_seq)
    hp = jnp.where(pos == 0, 0.0, _rms_mod(xp_ref[...], nw_ref[...], sc, sh))
    hn = jnp.where(pos == tiles_per_seq - 1, 0.0, _rms_mod(xn_ref[...], nw_ref[...], sc, sh))
    hm = _rms_mod(x_ref[...], nw_ref[...], sc, sh)
    n_slab = D_MODEL // LANES
    for j in range(n_slab):
        lanes = slice(j * LANES, (j + 1) * LANES)
        hin_ref[j, 0:hb, :] = hp[:, lanes]
        hin_ref[j, hb:hb + tm, :] = hm[:, lanes]
        hin_ref[j, hb + tm:hb + tm + hb, :] = hn[:, lanes]
    rows = tm // POOL_STRIDE
    slabs_per_group = POOL_GROUP // LANES
    ti = pos * tm + POOL_STRIDE * lax.broadcasted_iota(jnp.int32, (rows, 1), 0)
    for gi, win in enumerate(POOL_WINDOWS):
        half = win // 2
        for v in range(POOL_STRIDE):
            t = ti + v
            cnt = (jnp.minimum(t + (win - half), seq) - jnp.maximum(t - half, 0)).astype(F32)
            for j in range(gi * slabs_per_group, (gi + 1) * slabs_per_group):
                s = None
                for dlt in range(-half, win - half):
                    term = hin_ref[j, pl.ds(hb + v + dlt, rows, stride=POOL_STRIDE), :]
                    s = term if s is None else s + term
                d_ref[j, pl.ds(v, rows, stride=POOL_STRIDE), :] = (
                    s / cnt - hin_ref[j, pl.ds(hb + v, rows, stride=POOL_STRIDE), :])
        cols = slice(gi * POOL_GROUP, (gi + 1) * POOL_GROUP)
        dd = jnp.concatenate([d_ref[j] for j in range(gi * slabs_per_group, (gi + 1) * slabs_per_group)], axis=1)
        y_ref[:, cols] = jnp.dot(dd.astype(BF16), wg_ref[gi], preferred_element_type=F32)
    o_ref[...] = x_ref[...] + gt * (y_ref[...] * scale_ref[...])


def pool_mixer(h, mod, nw, wg, scale, *, seq, tm):
    n, d = h.shape
    n_tiles = n // tm
    tps = seq // tm
    prev, main, nxt = _halo_specs(tm, POOL_HALO, n_tiles, d)
    return pl.pallas_call(
        functools.partial(_pool_kernel, tm=tm, tiles_per_seq=tps, seq=seq),
        out_shape=jax.ShapeDtypeStruct((n, d), F32),
        grid=(n_tiles,),
        in_specs=[prev, main, nxt, _mod_spec(tps), _resident((1, d)), _resident(wg.shape),
                  _resident(scale.shape)],
        out_specs=pl.BlockSpec((tm, d), lambda i: (i, 0)),
        scratch_shapes=[pltpu.VMEM((d // LANES, tm + 2 * POOL_HALO, LANES), F32),
                        pltpu.VMEM((d // LANES, tm, LANES), F32), pltpu.VMEM((tm, d), F32)],
        compiler_params=_cparams(),
        name="pool_mixer",
    )(h, h, h, mod, nw, wg, scale)


SWA_QKV_COLS = SWA_DIM + 4 * SWA_KV_DIM
SWA_NORM_COLS = SWA_DIM + 2 * SWA_KV_DIM
SWA_PC = 256
ROPE_QUARTER = SWA_HEAD_DIM // 4


def _swa_qkv_kernel(x_ref, mod_ref, nw_ref, w_ref, hnw_ref, seg_ref, cos_ref, sin_ref,
                    q_ref, k_ref, v_ref, *, rope):
    sh, sc = mod_ref[0:1, :], mod_ref[1:2, :]
    hin = _rms_mod(x_ref[...], nw_ref[...], sc, sh).astype(BF16)
    pc = SWA_PC
    n_chunks = SWA_QKV_COLS // pc
    q_chunks = SWA_DIM // pc
    k_chunks = 2 * SWA_KV_DIM // pc
    for j in range(n_chunks):
        cols = slice(j * pc, (j + 1) * pc)
        p = jnp.dot(hin, w_ref[:, cols], preferred_element_type=F32)
        if j < q_chunks + k_chunks:
            ms = jnp.dot((p * p).astype(BF16), seg_ref[...], preferred_element_type=F32)
            p = p * lax.rsqrt(ms + EPS) * hnw_ref[:, cols]
            if rope:
                lane = lax.broadcasted_iota(jnp.int32, (p.shape[0], LANES), 1)
                up = lax.rem(lane, 2 * ROPE_QUARTER) < ROPE_QUARTER
                halves = []
                for t in range(pc // LANES):
                    ph = p[:, t * LANES:(t + 1) * LANES]
                    partner = jnp.where(up, pltpu.roll(ph, LANES - ROPE_QUARTER, 1),
                                        pltpu.roll(ph, ROPE_QUARTER, 1))
                    halves.append(ph * cos_ref[...] + partner * sin_ref[...])
                p = jnp.concatenate(halves, axis=1)
        if j < q_chunks:
            q_ref[:, cols] = (p * (SWA_HEAD_DIM ** -0.5)).astype(BF16)
        elif j < q_chunks + k_chunks:
            k_ref[:, (j - q_chunks) * pc:(j - q_chunks + 1) * pc] = p.astype(BF16)
        else:
            jj = j - q_chunks - k_chunks
            v_ref[:, jj * pc:(jj + 1) * pc] = p.astype(BF16)


def swa_qkv(h, mod, nw, w, hnw, seg, cos, sin, *, seq, tm, rope):
    n, d = h.shape
    tps = seq // tm
    row = lambda w_: pl.BlockSpec((tm, w_), lambda i: (i, 0))
    tab = pl.BlockSpec((tm, LANES), lambda i: (i % tps, 0))
    kvw = 2 * SWA_KV_DIM
    return pl.pallas_call(
        functools.partial(_swa_qkv_kernel, rope=rope),
        out_shape=[jax.ShapeDtypeStruct((n, SWA_DIM), BF16), jax.ShapeDtypeStruct((n, kvw), BF16),
                   jax.ShapeDtypeStruct((n, kvw), BF16)],
        grid=(n // tm,),
        in_specs=[row(d), _mod_spec(tps), _resident((1, d)), _resident(w.shape), _resident(hnw.shape),
                  _resident(seg.shape), tab, tab],
        out_specs=[row(SWA_DIM), row(kvw), row(kvw)],
        compiler_params=_cparams(),
        name="swa_qkv",
    )(h, mod, nw, w, hnw, seg, cos, sin)


def _swa_attn_kernel(q_ref, kp_ref, kc_ref, kn_ref, vp_ref, vc_ref, vn_ref, kx_ref, vx_ref, sink_ref,
                     o_ref, *, blocks_per_seq):
    i = pl.program_id(1)
    blk = SWA_BLOCK
    ri = lax.broadcasted_iota(jnp.int32, (SWA_GROUP * blk, blk), 0) % blk
    ci = lax.broadcasted_iota(jnp.int32, (SWA_GROUP * blk, blk), 1)
    lane = lax.broadcasted_iota(jnp.int32, (blk, LANES), 1)
    low = lane < SWA_HEAD_DIM
    nt = (((1,), (1,)), ((), ()))
    lane_tiles = lambda a: [a[:, t * LANES:(t + 1) * LANES] for t in range(a.shape[1] // LANES)]
    inst = []
    for sub in range(SWA_SUB):
        rows = slice(sub * blk, (sub + 1) * blk)
        prev = (kp_ref, vp_ref, slice(0, blk)) if sub == 0 else (kc_ref, vc_ref, slice((sub - 1) * blk, sub * blk))
        nxt = ((kn_ref, vn_ref, slice(0, blk)) if sub == SWA_SUB - 1
               else (kc_ref, vc_ref, slice((sub + 1) * blk, (sub + 2) * blk)))
        gblock = i * SWA_SUB + sub
        ok_prev = jnp.logical_and(ci >= ri, gblock > 0)
        ok_next = jnp.logical_and(ci <= ri, gblock < blocks_per_seq - 1)
        for g in range(SWA_KV_HEADS):
            inst.append(dict(g=g, rows=rows, gs=slice(g * LANES, (g + 1) * LANES), prev=prev, nxt=nxt,
                             ok_prev=ok_prev, ok_next=ok_next))

    def stacked_queries(t):
        qs = []
        for half in range(2):
            qt = q_ref[t["rows"], (2 * t["g"] + half) * LANES:(2 * t["g"] + half + 1) * LANES]
            qs.append(jnp.where(low, qt, jnp.zeros_like(qt)))
            qs.append(jnp.where(low, jnp.zeros_like(qt), qt))
        return jnp.concatenate(qs, axis=0)

    def sink_column(g):
        return jnp.concatenate(
            [jnp.broadcast_to(sink_ref[SWA_GROUP * g + hh:SWA_GROUP * g + hh + 1, 0:1], (blk, 1))
             for hh in range(SWA_GROUP)], axis=0)

    def window(t, which, cur_ref, ctx_ref):
        p_ref, n_ref = t["prev"][which], t["nxt"][which]
        return jnp.concatenate([p_ref[t["prev"][2], t["gs"]], cur_ref[t["rows"], t["gs"]],
                                n_ref[t["nxt"][2], t["gs"]], ctx_ref[:, t["gs"]]], axis=0)

    qst = [stacked_queries(t) for t in inst]
    sinks = [sink_column(t["g"]) for t in inst]
    scores = [lax.dot_general(q, window(t, 0, kc_ref, kx_ref), nt, preferred_element_type=F32)
              for q, t in zip(qst, inst)]
    pieces = []
    for s, t in zip(scores, inst):
        ts = lane_tiles(s)
        pieces.append([jnp.where(t["ok_prev"], ts[0], NEG_INF), ts[1], jnp.where(t["ok_next"], ts[2], NEG_INF)]
                      + ts[3:])
    ms = [jnp.maximum(jnp.max(functools.reduce(jnp.maximum, ps), axis=-1, keepdims=True), sk)
          for ps, sk in zip(pieces, sinks)]
    es = [[jnp.exp(p - m) for p in ps] for ps, m in zip(pieces, ms)]
    invs = [1.0 / (jnp.sum(functools.reduce(jnp.add, e), axis=-1, keepdims=True) + jnp.exp(sk - m))
            for e, sk, m in zip(es, sinks, ms)]
    pvs = [jnp.dot(jnp.concatenate(e, axis=1).astype(BF16), window(t, 1, vc_ref, vx_ref),
                   preferred_element_type=F32) * inv
           for e, t, inv in zip(es, inst, invs)]
    for t, pv in zip(inst, pvs):
        for half in range(2):
            a = pv[(2 * half) * blk:(2 * half + 1) * blk]
            b = pv[(2 * half + 1) * blk:(2 * half + 2) * blk]
            o_ref[t["rows"], (2 * t["g"] + half) * LANES:(2 * t["g"] + half + 1) * LANES] = (
                jnp.where(low, a, b).astype(BF16))


def swa_attention(q, k2, v2, kx2, vx2, sink, *, batch, seq, ctx_len):
    nb = seq // SWA_BLOCK
    ns = nb // SWA_SUB
    kvw = 2 * SWA_KV_DIM
    qspec = pl.BlockSpec((SWA_SUB * SWA_BLOCK, SWA_DIM), lambda b, i: (b * ns + i, 0))
    cur = pl.BlockSpec((SWA_SUB * SWA_BLOCK, kvw), lambda b, i: (b * ns + i, 0))
    halo = lambda off: pl.BlockSpec(
        (SWA_BLOCK, kvw), lambda b, i: (b * nb + jnp.clip(i * SWA_SUB + off, 0, nb - 1), 0))
    xspec = pl.BlockSpec((ctx_len, kvw), lambda b, i: (b, 0))
    return pl.pallas_call(
        functools.partial(_swa_attn_kernel, blocks_per_seq=nb),
        out_shape=jax.ShapeDtypeStruct(q.shape, BF16),
        grid=(batch, ns),
        in_specs=[qspec, halo(-1), cur, halo(SWA_SUB), halo(-1), cur, halo(SWA_SUB), xspec, xspec,
                  _resident(sink.shape)],
        out_specs=qspec,
        compiler_params=_cparams(2),
        name="swa_attention",
    )(q, k2, k2, k2, v2, v2, v2, kx2, vx2, sink)


CNV_RB = 32
CNV_PC = 256
CNV_STRIDE = 4


def _conv_module_kernel(xp_ref, x_ref, xn_ref, mod_ref, nw_ref, w1_ref, dw_ref, lnw_ref, lnb_ref, w2_ref,
                        o_ref, hin_ref, u_ref, c_ref, s_ref, *, tm, tiles_per_seq):
    sh, sc, gt = mod_ref[0:1, :], mod_ref[1:2, :], mod_ref[2:3, :]
    _fill_hin_ext(hin_ref, xp_ref, x_ref, xn_ref, nw_ref[...], sc, sh, tiles_per_seq, HALO, tm)
    d = D_MODEL
    n_slab = d // LANES
    for c in range(d // CNV_PC):
        cols = slice(c * CNV_PC, (c + 1) * CNV_PC)
        gcols = slice(d + c * CNV_PC, d + (c + 1) * CNV_PC)
        a1 = jnp.dot(hin_ref[...], w1_ref[:, cols], preferred_element_type=F32)
        a2 = jnp.dot(hin_ref[...], w1_ref[:, gcols], preferred_element_type=F32)
        u = a1 * jax.nn.sigmoid(a2)
        for t in range(CNV_PC // LANES):
            u_ref[c * (CNV_PC // LANES) + t] = u[:, t * LANES:(t + 1) * LANES]
    pad = CNV_WIDTH // 2
    rows = tm // CNV_STRIDE
    for j in range(n_slab):
        lanes = slice(j * LANES, (j + 1) * LANES)
        for v in range(CNV_STRIDE):
            acc = None
            for k in range(CNV_WIDTH):
                term = dw_ref[k:k + 1, lanes] * u_ref[j, pl.ds(HALO + v + k - pad, rows, stride=CNV_STRIDE), :]
                acc = term if acc is None else acc + term
            c_ref[j, pl.ds(v, rows, stride=CNV_STRIDE), :] = acc
    for rb in range(tm // CNV_RB):
        rsl = slice(rb * CNV_RB, (rb + 1) * CNV_RB)
        xs = [c_ref[j, rsl, :] for j in range(n_slab)]
        tot = xs[0]
        for xj in xs[1:]:
            tot = tot + xj
        mu = jnp.sum(tot, axis=-1, keepdims=True) * (1.0 / d)
        xcs = [xj - mu for xj in xs]
        sq = xcs[0] * xcs[0]
        for xc in xcs[1:]:
            sq = sq + xc * xc
        rstd = lax.rsqrt(jnp.sum(sq, axis=-1, keepdims=True) * (1.0 / d) + EPS)
        for j, xc in enumerate(xcs):
            lanes = slice(j * LANES, (j + 1) * LANES)
            y = xc * rstd * lnw_ref[:, lanes] + lnb_ref[:, lanes]
            s_ref[rsl, lanes] = jax.nn.silu(y).astype(BF16)
    o_ref[...] = x_ref[...] + gt * jnp.dot(s_ref[...], w2_ref[...], preferred_element_type=F32)


def conv_module(h, mod, nw, w1, dw, lnw, lnb, w2, *, seq, tm):
    n, d = h.shape
    n_tiles = n // tm
    tps = seq // tm
    prev, main, nxt = _halo_specs(tm, HALO, n_tiles, d)
    return pl.pallas_call(
        functools.partial(_conv_module_kernel, tm=tm, tiles_per_seq=tps),
        out_shape=jax.ShapeDtypeStruct((n, d), F32),
        grid=(n_tiles,),
        in_specs=[prev, main, nxt, _mod_spec(tps), _resident((1, d)), _resident(w1.shape),
                  _resident(dw.shape), _resident(lnw.shape), _resident(lnb.shape), _resident(w2.shape)],
        out_specs=pl.BlockSpec((tm, d), lambda i: (i, 0)),
        scratch_shapes=[pltpu.VMEM((tm + 2 * HALO, d), BF16),
                        pltpu.VMEM((d // LANES, tm + 2 * HALO, LANES), F32),
                        pltpu.VMEM((d // LANES, tm, LANES), F32),
                        pltpu.VMEM((tm, d), BF16)],
        compiler_params=_cparams(),
        name="conv_module",
    )(h, h, h, mod, nw, w1, dw, lnw, lnb, w2)


def _rope_tables(seq):
    n_freq = SWA_HEAD_DIM // 4
    inv_freq = ROPE_BASE ** (-jnp.arange(n_freq, dtype=F32) / n_freq)
    pos = jnp.arange(seq)
    rows = (pos // GRID_W).astype(F32)
    cols = (pos % GRID_W).astype(F32)
    ang_r = rows[:, None] * inv_freq
    ang_c = cols[:, None] * inv_freq
    ang = jnp.concatenate([ang_r, ang_r, ang_c, ang_c], axis=-1)
    sign = jnp.where((jnp.arange(SWA_HEAD_DIM) % (2 * ROPE_QUARTER)) < ROPE_QUARTER, -1.0, 1.0)
    cos = jnp.tile(jnp.cos(ang), (1, LANES // SWA_HEAD_DIM))
    sin = jnp.tile(jnp.sin(ang) * sign, (1, LANES // SWA_HEAD_DIM))
    return cos, sin


def _tile_rows(seq):
    return min(seq, 512)


def kernel(x, c, ctx, c_ctx, ada_w, ada_b, norm_mix_w, norm_ffn_w, dn_w_in, dn_conv, dn_a_log, dn_dt_bias,
           dn_norm_w, dn_w_out, pool_w_grp, pool_scale, swa_w_qkv, swa_q_norm, swa_k_norm, swa_sink,
           swa_w_out, cnv_w_pw1, cnv_dw, cnv_ln_w, cnv_ln_b, cnv_w_pw2, ffn_w_gate, ffn_w_up, ffn_conv,
           ffn_conv_b, ffn_w_down):
    bsz, seq, d = x.shape
    ctx_len = ctx.shape[1]
    depth = ada_w.shape[0]
    h = x.reshape(bsz * seq, d)
    hc = ctx.reshape(bsz * ctx_len, d)
    p = dict(norm_mix_w=norm_mix_w, norm_ffn_w=norm_ffn_w, dn_w_in=dn_w_in, dn_conv=dn_conv,
             dn_a_log=dn_a_log, dn_dt_bias=dn_dt_bias, dn_norm_w=dn_norm_w, dn_w_out=dn_w_out,
             pool_w_grp=pool_w_grp, pool_scale=pool_scale, swa_w_qkv=swa_w_qkv, swa_q_norm=swa_q_norm,
             swa_k_norm=swa_k_norm, swa_sink=swa_sink, swa_w_out=swa_w_out, cnv_w_pw1=cnv_w_pw1,
             cnv_dw=cnv_dw, cnv_ln_w=cnv_ln_w, cnv_ln_b=cnv_ln_b, cnv_w_pw2=cnv_w_pw2,
             ffn_w_gate=ffn_w_gate, ffn_w_up=ffn_w_up, ffn_conv=ffn_conv, ffn_conv_b=ffn_conv_b,
             ffn_w_down=ffn_w_down)
    dims = dict(bsz=bsz, seq=seq, ctx_len=ctx_len, depth=depth)
    mods = modulation(c, c_ctx, ada_w, ada_b)
    for i in range(depth):
        mod_l, mod_c = layer_modulation(mods, i, bsz)
        h, hc = mixer_layer(i, h, hc, mod_l, mod_c, p, dims)
        h, hc = ffn_layer(i, h, hc, mod_l, mod_c, p, dims)
    return h.reshape(bsz, seq, d)


def modulation(c, c_ctx, ada_w, ada_b):
    bsz, d = c.shape
    n_cond = -(-(bsz + 1) // SUBLANES) * SUBLANES
    cond = jnp.zeros((n_cond, d), F32).at[:bsz].set(c).at[bsz].set(c_ctx)
    return ada_modulation(cond, ada_w, ada_b)


def layer_modulation(mods, i, bsz):
    d = mods.shape[-1] // 6
    mod_l = mods[i, :bsz].reshape(bsz, 6, d)
    mod_c = jnp.broadcast_to(mods[i, bsz].reshape(1, 6, d), (bsz, 6, d))
    return mod_l, mod_c


def _keeps_ctx(i, depth):
    return any((j % N_MIXERS) in CTX_READING_MIXERS for j in range(i + 1, depth))


def ffn_layer(i, h, hc, mod_l, mod_c, p, dims):
    d = h.shape[-1]
    seq, ctx_len = dims["seq"], dims["ctx_len"]
    ffn_args = (p["norm_ffn_w"][i].reshape(1, d), p["ffn_w_gate"][i].astype(BF16),
                p["ffn_w_up"][i].astype(BF16), p["ffn_conv"][i], p["ffn_conv_b"][i].reshape(1, FFN_DIM),
                p["ffn_w_down"][i].astype(BF16))
    h = conv_ffn(h, mod_l, *ffn_args, seq=seq, tm=_tile_rows(seq))
    if _keeps_ctx(i, dims["depth"]):
        hc = conv_ffn(hc, mod_c, *ffn_args, seq=ctx_len, tm=_tile_rows(ctx_len))
    return h, hc


def mixer_layer(i, h, hc, mod_l, mod_c, p, dims):
    d = h.shape[-1]
    bsz, seq, ctx_len = dims["bsz"], dims["seq"], dims["ctx_len"]
    tm_c = _tile_rows(ctx_len)
    kind, slot = i % N_MIXERS, i // N_MIXERS
    keep_ctx = _keeps_ctx(i, dims["depth"])
    nw_mix = p["norm_mix_w"][i].reshape(1, d)
    lat = dict(seq=seq, tm=_tile_rows(seq))
    cx = dict(seq=ctx_len, tm=tm_c)

    if kind == MIX_DELTANET:
        w_in = p["dn_w_in"][slot]
        wqkv = w_in[:, :3 * DN_DIM].astype(BF16)
        wz = w_in[:, 3 * DN_DIM:4 * DN_DIM].astype(BF16)
        wgt = jnp.zeros((d, LANES), F32).at[:, :4 * DN_HEADS].set(w_in[:, 4 * DN_DIM:]).astype(BF16)
        aneg = jnp.zeros((1, LANES), F32).at[0, 2 * DN_HEADS:4 * DN_HEADS].set(
            -jnp.exp(p["dn_a_log"][slot].astype(F32)).reshape(-1))
        dtb = jnp.zeros((1, LANES), F32).at[0, 2 * DN_HEADS:4 * DN_HEADS].set(
            p["dn_dt_bias"][slot].astype(F32).reshape(-1))
        hnw = p["dn_norm_w"][slot].reshape(1, DN_HEAD_DIM)
        wout = p["dn_w_out"][slot].astype(BF16)
        proj = functools.partial(dn_project, nw=nw_mix, wqkv=wqkv, cw=p["dn_conv"][slot], wz=wz, wgt=wgt,
                                 aneg=aneg, dtb=dtb)
        qc, kc, vc, zc, bgc = proj(hc, mod_c, **cx)
        ql, kl, vl, zl, bgl = proj(h, mod_l, **lat)
        s0 = jnp.zeros((bsz, 2, DN_HEADS, DN_HEAD_DIM, DN_HEAD_DIM), F32)
        ocf, ocb, s_ctx = dn_scan(qc, kc, vc, bgc, s0, batch=bsz, seq=ctx_len)
        olf, olb, _ = dn_scan(ql, kl, vl, bgl, s_ctx, batch=bsz, seq=seq)
        h = dn_readout(olf, olb, zl, h, mod_l, hnw, wout, **lat)
        if keep_ctx:
            hc = dn_readout(ocf, ocb, zc, hc, mod_c, hnw, wout, **cx)
    elif kind == MIX_POOL:
        wg = p["pool_w_grp"][slot].astype(BF16)
        scale = p["pool_scale"][slot].reshape(1, d)
        h = pool_mixer(h, mod_l, nw_mix, wg, scale, **lat)
        if keep_ctx:
            hc = pool_mixer(hc, mod_c, nw_mix, wg, scale, **cx)
    elif kind == MIX_SWA:
        assert not keep_ctx
        wq = p["swa_w_qkv"][slot]
        dup = lambda w_: jnp.repeat(w_.reshape(d, SWA_KV_HEADS, 1, SWA_HEAD_DIM), 2, axis=2).reshape(
            d, 2 * SWA_KV_DIM)
        w_all = jnp.concatenate([wq[:, :SWA_DIM], dup(wq[:, SWA_DIM:SWA_DIM + SWA_KV_DIM]),
                                 dup(wq[:, SWA_DIM + SWA_KV_DIM:])], axis=1).astype(BF16)
        hnw = jnp.concatenate([jnp.tile(p["swa_q_norm"][slot], SWA_HEADS),
                               jnp.tile(p["swa_k_norm"][slot], 2 * SWA_KV_HEADS)]).reshape(1, SWA_NORM_COLS)
        li = jnp.arange(SWA_PC)
        seg = jnp.where((li[:, None] // SWA_HEAD_DIM) == (li[None, :] // SWA_HEAD_DIM),
                        1.0 / SWA_HEAD_DIM, 0.0).astype(BF16)
        cos, sin = _rope_tables(seq)
        sink = jnp.broadcast_to(p["swa_sink"][slot].astype(F32).reshape(SWA_HEADS, 1), (SWA_HEADS, LANES))
        ql, k2, v2 = swa_qkv(h, mod_l, nw_mix, w_all, hnw, seg, cos, sin, rope=True, **lat)
        _, kx2, vx2 = swa_qkv(hc, mod_c, nw_mix, w_all, hnw, seg, cos[:tm_c], sin[:tm_c], rope=False, **cx)
        o = swa_attention(ql, k2, v2, kx2, vx2, sink, batch=bsz, seq=seq, ctx_len=ctx_len)
        h = matmul_residual(o, h, mod_l, p["swa_w_out"][slot].astype(BF16), **lat)
    else:
        args = (nw_mix, p["cnv_w_pw1"][slot].astype(BF16), p["cnv_dw"][slot], p["cnv_ln_w"][slot].reshape(1, d),
                p["cnv_ln_b"][slot].reshape(1, d), p["cnv_w_pw2"][slot].astype(BF16))
        h = conv_module(h, mod_l, *args, **lat)
        if keep_ctx:
            hc = conv_module(hc, mod_c, *args, **cx)
    return h, hc
```

```python
import functools

import jax
import jax.numpy as jnp
from jax import lax
from jax.experimental import pallas as pl
from jax.experimental.pallas import tpu as pltpu

D_MODEL = 1024
EPS = 1e-6
NEG_INF = -1e30
GRID_W = 64
ROPE_BASE = 10000.0
N_MIXERS = 4
MIX_DELTANET, MIX_POOL, MIX_SWA, MIX_CONV = 0, 1, 2, 3
CTX_READING_MIXERS = (MIX_DELTANET, MIX_SWA)

DN_HEADS = 8
DN_HEAD_DIM = 128
DN_DIM = DN_HEADS * DN_HEAD_DIM
DN_CONV = 5
DN_CHUNK = 128
DN_INV_BASE = 16
DN_SUB = 4

POOL_WINDOWS = (2, 4, 8, 16)
POOL_GROUP = D_MODEL // len(POOL_WINDOWS)

SWA_HEADS = 16
SWA_KV_HEADS = 4
SWA_HEAD_DIM = 64
SWA_GROUP = SWA_HEADS // SWA_KV_HEADS
SWA_DIM = SWA_HEADS * SWA_HEAD_DIM
SWA_KV_DIM = SWA_KV_HEADS * SWA_HEAD_DIM
SWA_WINDOW = 128
SWA_BLOCK = 128
SWA_SUB = 4

CNV_WIDTH = 31
FFN_DIM = 2816
FFN_CONV = 3

LANES = 128
SUBLANES = 8
VMEM_LIMIT_BYTES = 56 * 1024 * 1024

F32 = jnp.float32
BF16 = jnp.bfloat16
HALO = 2 * SUBLANES
POOL_HALO = SUBLANES
POOL_STRIDE = 4


def _cparams(n_axes=1):
    return pltpu.CompilerParams(
        dimension_semantics=("arbitrary",) * n_axes,
        vmem_limit_bytes=VMEM_LIMIT_BYTES,
    )


def _resident(shape):
    nd = len(shape)
    return pl.BlockSpec(shape, lambda *_: (0,) * nd, pipeline_mode=pl.Buffered(1))


def _resident_layer(shape, layer):
    return pl.BlockSpec((None,) + tuple(shape[1:]), lambda *_: (layer,) + (0,) * (len(shape) - 1),
                        pipeline_mode=pl.Buffered(1))


def _halo_specs(tm, hb, n_tiles, width):
    r = tm // hb
    last = n_tiles * r - 1
    prev = pl.BlockSpec((hb, width), lambda i: (jnp.maximum(i * r - 1, 0), 0))
    main = pl.BlockSpec((tm, width), lambda i: (i, 0))
    nxt = pl.BlockSpec((hb, width), lambda i: (jnp.minimum((i + 1) * r, last), 0))
    return prev, main, nxt


def _mod_spec(tiles_per_seq):
    return pl.BlockSpec((None, 6, D_MODEL), lambda i: (i // tiles_per_seq, 0, 0))


def _rms_mod(x, nw, sc, sh):
    ms = jnp.mean(x * x, axis=-1, keepdims=True)
    return (x * lax.rsqrt(ms + EPS) * nw) * (1.0 + sc) + sh


def _fill_hin_ext(hin_ref, xp_ref, x_ref, xn_ref, nw, sc, sh, tiles_per_seq, hb, tm):
    i = pl.program_id(0)
    pos = lax.rem(i, tiles_per_seq)
    first = pos == 0
    last = pos == tiles_per_seq - 1
    hp = jnp.where(first, 0.0, _rms_mod(xp_ref[...], nw, sc, sh))
    hn = jnp.where(last, 0.0, _rms_mod(xn_ref[...], nw, sc, sh))
    hin_ref[0:hb, :] = hp.astype(hin_ref.dtype)
    hin_ref[hb:hb + tm, :] = _rms_mod(x_ref[...], nw, sc, sh).astype(hin_ref.dtype)
    hin_ref[hb + tm:hb + tm + hb, :] = hn.astype(hin_ref.dtype)


def _shift_rows(a, d, n_rows):
    if d == 0:
        return a
    return pltpu.roll(a, (-d) % n_rows, 0)


ADA_TN = 1536


def _ada_kernel(cond_ref, w_ref, b_ref, o_ref):
    s = jax.nn.silu(cond_ref[...])
    o_ref[...] = jnp.dot(s, w_ref[...], preferred_element_type=F32,
                         precision=lax.Precision.HIGHEST) + b_ref[...]


def ada_modulation(cond, ada_w, ada_b):
    depth, d, n = ada_w.shape
    rows = cond.shape[0]
    return pl.pallas_call(
        _ada_kernel,
        out_shape=jax.ShapeDtypeStruct((depth, rows, n), F32),
        grid=(depth, n // ADA_TN),
        in_specs=[
            pl.BlockSpec((rows, d), lambda l, j: (0, 0)),
            pl.BlockSpec((None, d, ADA_TN), lambda l, j: (l, 0, j)),
            pl.BlockSpec((None, 1, ADA_TN), lambda l, j: (l, 0, j)),
        ],
        out_specs=pl.BlockSpec((None, rows, ADA_TN), lambda l, j: (l, 0, j)),
        compiler_params=_cparams(2),
        name="ada_modulation",
    )(cond, ada_w, ada_b.reshape(depth, 1, n))


FFN_FC = 256
FFN_DOWN_GROUP = 6


def _ffn_kernel(xp_ref, x_ref, xn_ref, mod_ref, nw_ref, wg_ref, wu_ref, cw_ref, cb_ref, wd_ref,
                o_ref, hin_ref, act_ref, acc_ref, *, tm, tiles_per_seq):
    sh, sc, gt = mod_ref[3:4, :], mod_ref[4:5, :], mod_ref[5:6, :]
    _fill_hin_ext(hin_ref, xp_ref, x_ref, xn_ref, nw_ref[...], sc, sh, tiles_per_seq, HALO, tm)
    n_ext = tm + 2 * HALO
    n_chunks = FFN_DIM // FFN_FC
    group_start = 0
    for c in range(n_chunks):
        cols = slice(c * FFN_FC, (c + 1) * FFN_FC)
        g_ext = jnp.dot(hin_ref[...], wg_ref[:, cols], preferred_element_type=F32)
        u = jnp.dot(hin_ref[HALO:HALO + tm, :], wu_ref[:, cols], preferred_element_type=F32)
        g = cb_ref[:, cols]
        for k in range(FFN_CONV):
            g = g + cw_ref[k:k + 1, cols] * _shift_rows(g_ext, k - 1, n_ext)[HALO:HALO + tm, :]
        act_ref[:, cols] = (jax.nn.silu(g) * u).astype(BF16)
        if (c + 1) % FFN_DOWN_GROUP == 0 or c == n_chunks - 1:
            ks = slice(group_start * FFN_FC, (c + 1) * FFN_FC)
            y = jnp.dot(act_ref[:, ks], wd_ref[ks, :], preferred_element_type=F32)
            if group_start == 0:
                acc_ref[...] = y
            else:
                acc_ref[...] += y
            group_start = c + 1
    o_ref[...] = x_ref[...] + gt * acc_ref[...]


def conv_ffn(h, mod, nw, wg, wu, cw, cb, wd, *, layer, seq, tm):
    n, d = h.shape
    n_tiles = n // tm
    tps = seq // tm
    prev, main, nxt = _halo_specs(tm, HALO, n_tiles, d)
    return pl.pallas_call(
        functools.partial(_ffn_kernel, tm=tm, tiles_per_seq=tps),
        out_shape=jax.ShapeDtypeStruct((n, d), F32),
        grid=(n_tiles,),
        in_specs=[prev, main, nxt, _mod_spec(tps), _resident((1, d)),
                  _resident_layer(wg.shape, layer), _resident_layer(wu.shape, layer), _resident(cw.shape),
                  _resident(cb.shape), _resident_layer(wd.shape, layer)],
        out_specs=pl.BlockSpec((tm, d), lambda i: (i, 0)),
        scratch_shapes=[pltpu.VMEM((tm + 2 * HALO, d), BF16), pltpu.VMEM((tm, FFN_DIM), BF16),
                        pltpu.VMEM((tm, d), F32)],
        compiler_params=_cparams(),
        name="conv_ffn",
    )(h, h, h, mod, nw, wg, wu, cw, cb, wd)


def _mm_res_kernel(a_ref, res_ref, mod_ref, w_ref, o_ref):
    gt = mod_ref[2:3, :]
    o_ref[...] = res_ref[...] + gt * jnp.dot(a_ref[...], w_ref[...], preferred_element_type=F32)


def matmul_residual(a, res, mod, w, *, seq, tm):
    n, k = a.shape
    d = w.shape[1]
    tps = seq // tm
    return pl.pallas_call(
        _mm_res_kernel,
        out_shape=jax.ShapeDtypeStruct((n, d), F32),
        grid=(n // tm,),
        in_specs=[pl.BlockSpec((tm, k), lambda i: (i, 0)), pl.BlockSpec((tm, d), lambda i: (i, 0)),
                  _mod_spec(tps), _resident(w.shape)],
        out_specs=pl.BlockSpec((tm, d), lambda i: (i, 0)),
        compiler_params=_cparams(),
        name="matmul_residual",
    )(a, res, mod, w)


DN_PC = 256
DN_STRIDE = 4


def _dn_proj_kernel(xp_ref, x_ref, xn_ref, mod_ref, nw_ref, wqkv_ref, cw_ref, wz_ref, wgt_ref,
                    aneg_ref, dtb_ref, q_ref, k_ref, v_ref, z_ref, bg_ref, hin_ref, p_ref, a_ref,
                    *, tm, tiles_per_seq):
    sh, sc = mod_ref[0:1, :], mod_ref[1:2, :]
    _fill_hin_ext(hin_ref, xp_ref, x_ref, xn_ref, nw_ref[...], sc, sh, tiles_per_seq, HALO, tm)
    outs = (q_ref, k_ref, v_ref)
    per_out = DN_DIM // DN_PC
    heads_pc = DN_PC // DN_HEAD_DIM
    rows = tm // DN_STRIDE
    for c in range(3 * per_out):
        cols = slice(c * DN_PC, (c + 1) * DN_PC)
        p_ext = jnp.dot(hin_ref[...], wqkv_ref[:, cols], preferred_element_type=F32)
        which, cc = divmod(c, per_out)
        for hh in range(heads_pc):
            slot = (c % 2) * heads_pc + hh
            lanes = slice(c * DN_PC + hh * DN_HEAD_DIM, c * DN_PC + (hh + 1) * DN_HEAD_DIM)
            p_ref[slot] = p_ext[:, hh * DN_HEAD_DIM:(hh + 1) * DN_HEAD_DIM]
            for v in range(DN_STRIDE):
                acc = None
                for k in range(DN_CONV):
                    start = HALO + v + k - DN_CONV // 2
                    term = cw_ref[k:k + 1, lanes] * p_ref[slot, pl.ds(start, rows, stride=DN_STRIDE), :]
                    acc = term if acc is None else acc + term
                ah = jax.nn.silu(acc)
                if which < 2:
                    ah = ah * lax.rsqrt(jnp.sum(ah * ah, axis=-1, keepdims=True) + EPS)
                    if which == 0:
                        ah = ah * (DN_HEAD_DIM ** -0.5)
                a_ref[slot, pl.ds(v, rows, stride=DN_STRIDE), :] = ah
            lo = cc * DN_PC + hh * DN_HEAD_DIM
            outs[which][:, lo:lo + DN_HEAD_DIM] = a_ref[slot].astype(BF16)
    hm = hin_ref[HALO:HALO + tm, :]
    z_ref[...] = jnp.dot(hm, wz_ref[...], preferred_element_type=F32).astype(BF16)
    gates = jnp.dot(hm, wgt_ref[...], preferred_element_type=F32)
    lane = lax.broadcasted_iota(jnp.int32, gates.shape, 1)
    beta = jax.nn.sigmoid(gates)
    g = aneg_ref[...] * jax.nn.softplus(gates + dtb_ref[...])
    bg_ref[...] = jnp.where(lane < 2 * DN_HEADS, beta, g)


def dn_project(h, mod, nw, wqkv, cw, wz, wgt, aneg, dtb, *, seq, tm):
    n, d = h.shape
    n_tiles = n // tm
    tps = seq // tm
    prev, main, nxt = _halo_specs(tm, HALO, n_tiles, d)
    row = lambda w: pl.BlockSpec((tm, w), lambda i: (i, 0))
    return pl.pallas_call(
        functools.partial(_dn_proj_kernel, tm=tm, tiles_per_seq=tps),
        out_shape=[jax.ShapeDtypeStruct((n, DN_DIM), BF16)] * 4 + [jax.ShapeDtypeStruct((n, LANES), F32)],
        grid=(n_tiles,),
        in_specs=[prev, main, nxt, _mod_spec(tps), _resident((1, d)), _resident(wqkv.shape),
                  _resident(cw.shape), _resident(wz.shape), _resident(wgt.shape),
                  _resident(aneg.shape), _resident(dtb.shape)],
        out_specs=[row(DN_DIM)] * 4 + [row(LANES)],
        scratch_shapes=[pltpu.VMEM((tm + 2 * HALO, d), BF16),
                        pltpu.VMEM((2 * DN_PC // DN_HEAD_DIM, tm + 2 * HALO, DN_HEAD_DIM), F32),
                        pltpu.VMEM((2 * DN_PC // DN_HEAD_DIM, tm, DN_HEAD_DIM), F32)],
        compiler_params=_cparams(),
        name="dn_project",
    )(h, h, h, mod, nw, wqkv, cw, wz, wgt, aneg, dtb)


def _bdot(a, b):
    return jnp.dot(a.astype(BF16), b.astype(BF16), preferred_element_type=F32)


def _exact_mask_dot(mask, x):
    m = mask.astype(BF16)
    hi = x.astype(BF16)
    r1 = x - hi.astype(F32)
    mid = r1.astype(BF16)
    lo = (r1 - mid.astype(F32)).astype(BF16)
    return (jnp.dot(m, hi, preferred_element_type=F32) + jnp.dot(m, mid, preferred_element_type=F32)
            + jnp.dot(m, lo, preferred_element_type=F32))


def _unit_tri_inverses(lmats, lowers, eye, blk):
    c = lmats[0].shape[0]
    ns = [jnp.where(blk[0], -l, 0.0) for l in lmats]
    xs = [eye + n for n in ns]
    ps = [_bdot(n, n) for n in ns]
    for _ in range(2):
        rs = [_bdot(jnp.concatenate([x, p], axis=0), p) for x, p in zip(xs, ps)]
        xs = [x + r[:c] for x, r in zip(xs, rs)]
        ps = [r[c:] for r in rs]
    xs = [x + _bdot(x, p) for x, p in zip(xs, ps)]
    size = DN_INV_BASE
    for lvl in range(1, len(blk)):
        sel = jnp.logical_and(blk[lvl], jnp.logical_not(blk[lvl - 1]))
        groups = range(c // size)

        def active(a, lower):
            return jnp.concatenate([a[g * size:(g + 1) * size] for g in groups if (g % 2 == 1) == lower], axis=0)

        def expand(a, lower, rest):
            parts, k = [], 0
            for g in groups:
                if (g % 2 == 1) == lower:
                    parts.append(a[k * size:(k + 1) * size])
                    k += 1
                else:
                    parts.append(rest[g * size:(g + 1) * size])
            return jnp.concatenate(parts, axis=0)

        zero = jnp.zeros_like(eye)
        ts = [_bdot(active(jnp.where(sel, l, 0.0), lo), x) for l, lo, x in zip(lmats, lowers, xs)]
        us = [_bdot(active(x, lo), expand(t, lo, zero)) for x, lo, t in zip(xs, lowers, ts)]
        xs = [expand(active(x, lo) - u, lo, x) for x, lo, u in zip(xs, lowers, us)]
        size *= 2
    return xs


def _dn_scan_kernel(qf_ref, kf_ref, vf_ref, bgf_ref, qb_ref, kb_ref, vb_ref, bgb_ref, s0_ref,
                    of_ref, ob_ref, sfin_ref, s_ref, *, n_steps, n_sub):
    n = pl.program_id(1)
    c = DN_CHUNK

    @pl.when(n == 0)
    def _():
        s_ref[...] = s0_ref[...]

    ri = lax.broadcasted_iota(jnp.int32, (c, c), 0)
    ci = lax.broadcasted_iota(jnp.int32, (c, c), 1)
    eye = (ri == ci).astype(F32)
    blk = []
    bs = DN_INV_BASE
    while bs <= c:
        blk.append((ri // bs) == (ci // bs))
        bs *= 2

    dirs = ((qf_ref, kf_ref, vf_ref, bgf_ref, of_ref, ri >= ci, ri > ci),
            (qb_ref, kb_ref, vb_ref, bgb_ref, ob_ref, ri <= ci, ri < ci))
    inst = []
    for d, (q_ref, k_ref, v_ref, bg_ref, o_ref, incl, strict) in enumerate(dirs):
        for sub in range(n_sub):
            rows = slice(sub * c, (sub + 1) * c)
            bg = bg_ref[rows, :]
            is_g = lax.broadcasted_iota(jnp.int32, bg.shape, 1) >= 2 * DN_HEADS
            bgg = jnp.where(is_g, bg, 0.0)
            cum = _exact_mask_dot(incl, bgg)
            cum_t = cum.T
            tot = jnp.sum(bgg, axis=0, keepdims=True)
            egc = jnp.exp(cum)
            ekd = jnp.exp(tot - cum)
            etot = jnp.exp(tot)
            for h in range(DN_HEADS):
                lb = d * DN_HEADS + h
                lg = 2 * DN_HEADS + lb
                hs = slice(h * DN_HEAD_DIM, (h + 1) * DN_HEAD_DIM)
                inst.append(dict(
                    d=d, h=h, sub=sub, o=o_ref, rows=rows, hs=hs, incl=incl, strict=strict,
                    q=q_ref[rows, hs], k=k_ref[rows, hs], v=v_ref[rows, hs],
                    beta=bg[:, lb:lb + 1], gc=cum[:, lg:lg + 1], gc_row=cum_t[lg:lg + 1, :],
                    egc=egc[:, lg:lg + 1], ekd=ekd[:, lg:lg + 1], etot=etot[:, lg:lg + 1]))
    nt = (((1,), (1,)), ((), ()))
    tn = (((0,), (0,)), ((), ()))
    kq = [lax.dot_general(jnp.concatenate([t["k"], t["q"]], axis=0), t["k"], nt, preferred_element_type=F32)
          for t in inst]
    decays = [jnp.where(t["incl"], jnp.exp(jnp.where(t["incl"], t["gc"] - t["gc_row"], 0.0)), 0.0) for t in inst]
    lmats = [jnp.where(t["strict"], t["beta"] * r[:c] * dc, 0.0) for t, r, dc in zip(inst, kq, decays)]
    qks = [(r[c:] * dc).astype(BF16) for r, dc in zip(kq, decays)]
    tinvs = _unit_tri_inverses(lmats, [t["d"] == 0 for t in inst], eye, blk)
    rhss = [jnp.concatenate([t["v"].astype(F32) * t["beta"], t["k"].astype(F32) * (t["beta"] * t["egc"])], axis=1)
            for t in inst]
    uws = [_bdot(ti, r) for ti, r in zip(tinvs, rhss)]
    wq_lhs = [jnp.concatenate([uw[:, DN_HEAD_DIM:], t["q"].astype(F32) * t["egc"]], axis=0).astype(BF16)
              for t, uw in zip(inst, uws)]
    k_decs = [(t["k"].astype(F32) * t["ekd"]).astype(BF16) for t in inst]
    for pos in range(n_sub):
        cur = [i for i, t in enumerate(inst) if t["sub"] == (pos if t["d"] == 0 else n_sub - 1 - pos)]
        wqs = [jnp.dot(wq_lhs[i], s_ref[inst[i]["d"], inst[i]["h"]].astype(BF16), preferred_element_type=F32)
               for i in cur]
        v_news = [(uws[i][:, :DN_HEAD_DIM] - wq[:c]).astype(BF16) for i, wq in zip(cur, wqs)]
        outs = [wq[c:] + jnp.dot(qks[i], vn, preferred_element_type=F32) for i, wq, vn in zip(cur, wqs, v_news)]
        upds = [lax.dot_general(k_decs[i], vn, tn, preferred_element_type=F32) for i, vn in zip(cur, v_news)]
        for i, o, upd in zip(cur, outs, upds):
            t = inst[i]
            t["o"][t["rows"], t["hs"]] = o.astype(t["o"].dtype)
            s_ref[t["d"], t["h"]] = s_ref[t["d"], t["h"]] * t["etot"] + upd

    @pl.when(n == n_steps - 1)
    def _():
        sfin_ref[...] = s_ref[...]


def dn_scan(q, k, v, bg, s0, *, batch, seq):
    n_sub = min(DN_SUB, seq // DN_CHUNK)
    rows = n_sub * DN_CHUNK
    nc = seq // rows
    fwd = lambda w: pl.BlockSpec((rows, w), lambda b, n: (b * nc + n, 0))
    bwd = lambda w: pl.BlockSpec((rows, w), lambda b, n: (b * nc + nc - 1 - n, 0))
    st = pl.BlockSpec((None, 2, DN_HEADS, DN_HEAD_DIM, DN_HEAD_DIM), lambda b, n: (b, 0, 0, 0, 0))
    return pl.pallas_call(
        functools.partial(_dn_scan_kernel, n_steps=nc, n_sub=n_sub),
        out_shape=[jax.ShapeDtypeStruct(q.shape, BF16), jax.ShapeDtypeStruct(q.shape, BF16),
                   jax.ShapeDtypeStruct(s0.shape, F32)],
        grid=(batch, nc),
        in_specs=[fwd(DN_DIM), fwd(DN_DIM), fwd(DN_DIM), fwd(LANES),
                  bwd(DN_DIM), bwd(DN_DIM), bwd(DN_DIM), bwd(LANES), st],
        out_specs=[fwd(DN_DIM), bwd(DN_DIM), st],
        scratch_shapes=[pltpu.VMEM((2, DN_HEADS, DN_HEAD_DIM, DN_HEAD_DIM), F32)],
        compiler_params=_cparams(2),
        name="dn_scan",
    )(q, k, v, bg, q, k, v, bg, s0)


def _dn_readout_kernel(of_ref, ob_ref, z_ref, res_ref, mod_ref, nw_ref, w_ref, o_ref, y_ref):
    gt = mod_ref[2:3, :]
    for h in range(DN_HEADS):
        hs = slice(h * DN_HEAD_DIM, (h + 1) * DN_HEAD_DIM)
        o = of_ref[:, hs].astype(F32) + ob_ref[:, hs].astype(F32)
        ms = jnp.mean(o * o, axis=-1, keepdims=True)
        y = o * lax.rsqrt(ms + EPS) * nw_ref[...]
        y_ref[:, hs] = (y * jax.nn.silu(z_ref[:, hs].astype(F32))).astype(BF16)
    o_ref[...] = res_ref[...] + gt * jnp.dot(y_ref[...], w_ref[...], preferred_element_type=F32)


def dn_readout(o_f, o_b, z, res, mod, nw, w, *, seq, tm):
    n, d = res.shape
    tps = seq // tm
    row = lambda w_: pl.BlockSpec((tm, w_), lambda i: (i, 0))
    return pl.pallas_call(
        _dn_readout_kernel,
        out_shape=jax.ShapeDtypeStruct((n, d), F32),
        grid=(n // tm,),
        in_specs=[row(DN_DIM), row(DN_DIM), row(DN_DIM), row(d), _mod_spec(tps),
                  _resident(nw.shape), _resident(w.shape)],
        out_specs=row(d),
        scratch_shapes=[pltpu.VMEM((tm, DN_DIM), BF16)],
        compiler_params=_cparams(),
        name="dn_readout",
    )(o_f, o_b, z, res, mod, nw, w)


def _pool_kernel(xp_ref, x_ref, xn_ref, mod_ref, nw_ref, wg_ref, scale_ref, o_ref, hin_ref, d_ref, y_ref,
                 *, tm, tiles_per_seq, seq):
    sh, sc, gt = mod_ref[0:1, :], mod_ref[1:2, :], mod_ref[2:3, :]
    hb = POOL_HALO
    pos = lax.rem(pl.program_id(0), tiles_per_seq)
    hp = jnp.where(pos == 0, 0.0, _rms_mod(xp_ref[...], nw_ref[...], sc, sh))
    hn = jnp.where(pos == tiles_per_seq - 1, 0.0, _rms_mod(xn_ref[...], nw_ref[...], sc, sh))
    hm = _rms_mod(x_ref[...], nw_ref[...], sc, sh)
    n_slab = D_MODEL // LANES
    for j in range(n_slab):
        lanes = slice(j * LANES, (j + 1) * LANES)
        hin_ref[j, 0:hb, :] = hp[:, lanes]
        hin_ref[j, hb:hb + tm, :] = hm[:, lanes]
        hin_ref[j, hb + tm:hb + tm + hb, :] = hn[:, lanes]
    rows = tm // POOL_STRIDE
    slabs_per_group = POOL_GROUP // LANES
    ti = pos * tm + POOL_STRIDE * lax.broadcasted_iota(jnp.int32, (rows, 1), 0)
    for gi, win in enumerate(POOL_WINDOWS):
        half = win // 2
        for v in range(POOL_STRIDE):
            t = ti + v
            cnt = (jnp.minimum(t + (win - half), seq) - jnp.maximum(t - half, 0)).astype(F32)
            for j in range(gi * slabs_per_group, (gi + 1) * slabs_per_group):
                s = None
                for dlt in range(-half, win - half):
                    term = hin_ref[j, pl.ds(hb + v + dlt, rows, stride=POOL_STRIDE), :]
                    s = term if s is None else s + term
                d_ref[j, pl.ds(v, rows, stride=POOL_STRIDE), :] = (
                    s / cnt - hin_ref[j, pl.ds(hb + v, rows, stride=POOL_STRIDE), :])
        cols = slice(gi * POOL_GROUP, (gi + 1) * POOL_GROUP)
        dd = jnp.concatenate([d_ref[j] for j in range(gi * slabs_per_group, (gi + 1) * slabs_per_group)], axis=1)
        y_ref[:, cols] = jnp.dot(dd.astype(BF16), wg_ref[gi], preferred_element_type=F32)
    o_ref[...] = x_ref[...] + gt * (y_ref[...] * scale_ref[...])


def pool_mixer(h, mod, nw, wg, scale, *, seq, tm):
    n, d = h.shape
    n_tiles = n // tm
    tps = seq // tm
    prev, main, nxt = _halo_specs(tm, POOL_HALO, n_tiles, d)
    return pl.pallas_call(
        functools.partial(_pool_kernel, tm=tm, tiles_per_seq=tps, seq=seq),
        out_shape=jax.ShapeDtypeStruct((n, d), F32),
        grid=(n_tiles,),
        in_specs=[prev, main, nxt, _mod_spec(tps), _resident((1, d)), _resident(wg.shape),
                  _resident(scale.shape)],
        out_specs=pl.BlockSpec((tm, d), lambda i: (i, 0)),
        scratch_shapes=[pltpu.VMEM((d // LANES, tm + 2 * POOL_HALO, LANES), F32),
                        pltpu.VMEM((d // LANES, tm, LANES), F32), pltpu.VMEM((tm, d), F32)],
        compiler_params=_cparams(),
        name="pool_mixer",
    )(h, h, h, mod, nw, wg, scale)


SWA_QKV_COLS = SWA_DIM + 4 * SWA_KV_DIM
SWA_NORM_COLS = SWA_DIM + 2 * SWA_KV_DIM
SWA_PC = 256
ROPE_QUARTER = SWA_HEAD_DIM // 4


def _swa_qkv_kernel(x_ref, mod_ref, nw_ref, w_ref, hnw_ref, seg_ref, cos_ref, sin_ref,
                    q_ref, k_ref, v_ref, *, rope):
    sh, sc = mod_ref[0:1, :], mod_ref[1:2, :]
    hin = _rms_mod(x_ref[...], nw_ref[...], sc, sh).astype(BF16)
    pc = SWA_PC
    n_chunks = SWA_QKV_COLS // pc
    q_chunks = SWA_DIM // pc
    k_chunks = 2 * SWA_KV_DIM // pc
    for j in range(n_chunks):
        cols = slice(j * pc, (j + 1) * pc)
        p = jnp.dot(hin, w_ref[:, cols], preferred_element_type=F32)
        if j < q_chunks + k_chunks:
            ms = jnp.dot((p * p).astype(BF16), seg_ref[...], preferred_element_type=F32)
            p = p * lax.rsqrt(ms + EPS) * hnw_ref[:, cols]
            if rope:
                lane = lax.broadcasted_iota(jnp.int32, (p.shape[0], LANES), 1)
                up = lax.rem(lane, 2 * ROPE_QUARTER) < ROPE_QUARTER
                halves = []
                for t in range(pc // LANES):
                    ph = p[:, t * LANES:(t + 1) * LANES]
                    partner = jnp.where(up, pltpu.roll(ph, LANES - ROPE_QUARTER, 1),
                                        pltpu.roll(ph, ROPE_QUARTER, 1))
                    halves.append(ph * cos_ref[...] + partner * sin_ref[...])
                p = jnp.concatenate(halves, axis=1)
        if j < q_chunks:
            q_ref[:, cols] = (p * (SWA_HEAD_DIM ** -0.5)).astype(BF16)
        elif j < q_chunks + k_chunks:
            k_ref[:, (j - q_chunks) * pc:(j - q_chunks + 1) * pc] = p.astype(BF16)
        else:
            jj = j - q_chunks - k_chunks
            v_ref[:, jj * pc:(jj + 1) * pc] = p.astype(BF16)


def swa_qkv(h, mod, nw, w, hnw, seg, cos, sin, *, seq, tm, rope):
    n, d = h.shape
    tps = seq // tm
    row = lambda w_: pl.BlockSpec((tm, w_), lambda i: (i, 0))
    tab = pl.BlockSpec((tm, LANES), lambda i: (i % tps, 0))
    kvw = 2 * SWA_KV_DIM
    return pl.pallas_call(
        functools.partial(_swa_qkv_kernel, rope=rope),
        out_shape=[jax.ShapeDtypeStruct((n, SWA_DIM), BF16), jax.ShapeDtypeStruct((n, kvw), BF16),
                   jax.ShapeDtypeStruct((n, kvw), BF16)],
        grid=(n // tm,),
        in_specs=[row(d), _mod_spec(tps), _resident((1, d)), _resident(w.shape), _resident(hnw.shape),
                  _resident(seg.shape), tab, tab],
        out_specs=[row(SWA_DIM), row(kvw), row(kvw)],
        compiler_params=_cparams(),
        name="swa_qkv",
    )(h, mod, nw, w, hnw, seg, cos, sin)


def _swa_attn_kernel(q_ref, kp_ref, kc_ref, kn_ref, vp_ref, vc_ref, vn_ref, kx_ref, vx_ref, sink_ref,
                     o_ref, *, blocks_per_seq):
    i = pl.program_id(1)
    blk = SWA_BLOCK
    ri = lax.broadcasted_iota(jnp.int32, (SWA_GROUP * blk, blk), 0) % blk
    ci = lax.broadcasted_iota(jnp.int32, (SWA_GROUP * blk, blk), 1)
    lane = lax.broadcasted_iota(jnp.int32, (blk, LANES), 1)
    low = lane < SWA_HEAD_DIM
    nt = (((1,), (1,)), ((), ()))
    lane_tiles = lambda a: [a[:, t * LANES:(t + 1) * LANES] for t in range(a.shape[1] // LANES)]
    inst = []
    for sub in range(SWA_SUB):
        rows = slice(sub * blk, (sub + 1) * blk)
        prev = (kp_ref, vp_ref, slice(0, blk)) if sub == 0 else (kc_ref, vc_ref, slice((sub - 1) * blk, sub * blk))
        nxt = ((kn_ref, vn_ref, slice(0, blk)) if sub == SWA_SUB - 1
               else (kc_ref, vc_ref, slice((sub + 1) * blk, (sub + 2) * blk)))
        gblock = i * SWA_SUB + sub
        ok_prev = jnp.logical_and(ci >= ri, gblock > 0)
        ok_next = jnp.logical_and(ci <= ri, gblock < blocks_per_seq - 1)
        for g in range(SWA_KV_HEADS):
            inst.append(dict(g=g, rows=rows, gs=slice(g * LANES, (g + 1) * LANES), prev=prev, nxt=nxt,
                             ok_prev=ok_prev, ok_next=ok_next))

    def stacked_queries(t):
        qs = []
        for half in range(2):
            qt = q_ref[t["rows"], (2 * t["g"] + half) * LANES:(2 * t["g"] + half + 1) * LANES]
            qs.append(jnp.where(low, qt, jnp.zeros_like(qt)))
            qs.append(jnp.where(low, jnp.zeros_like(qt), qt))
        return jnp.concatenate(qs, axis=0)

    def sink_column(g):
        return jnp.concatenate(
            [jnp.broadcast_to(sink_ref[SWA_GROUP * g + hh:SWA_GROUP * g + hh + 1, 0:1], (blk, 1))
             for hh in range(SWA_GROUP)], axis=0)

    def window(t, which, cur_ref, ctx_ref):
        p_ref, n_ref = t["prev"][which], t["nxt"][which]
        return jnp.concatenate([p_ref[t["prev"][2], t["gs"]], cur_ref[t["rows"], t["gs"]],
                                n_ref[t["nxt"][2], t["gs"]], ctx_ref[:, t["gs"]]], axis=0)

    qst = [stacked_queries(t) for t in inst]
    sinks = [sink_column(t["g"]) for t in inst]
    scores = [lax.dot_general(q, window(t, 0, kc_ref, kx_ref), nt, preferred_element_type=F32)
              for q, t in zip(qst, inst)]
    pieces = []
    for s, t in zip(scores, inst):
        ts = lane_tiles(s)
        pieces.append([jnp.where(t["ok_prev"], ts[0], NEG_INF), ts[1], jnp.where(t["ok_next"], ts[2], NEG_INF)]
                      + ts[3:])
    ms = [jnp.maximum(jnp.max(functools.reduce(jnp.maximum, ps), axis=-1, keepdims=True), sk)
          for ps, sk in zip(pieces, sinks)]
    es = [[jnp.exp(p - m) for p in ps] for ps, m in zip(pieces, ms)]
    invs = [1.0 / (jnp.sum(functools.reduce(jnp.add, e), axis=-1, keepdims=True) + jnp.exp(sk - m))
            for e, sk, m in zip(es, sinks, ms)]
    pvs = [jnp.dot(jnp.concatenate(e, axis=1).astype(BF16), window(t, 1, vc_ref, vx_ref),
                   preferred_element_type=F32) * inv
           for e, t, inv in zip(es, inst, invs)]
    for t, pv in zip(inst, pvs):
        for half in range(2):
            a = pv[(2 * half) * blk:(2 * half + 1) * blk]
            b = pv[(2 * half + 1) * blk:(2 * half + 2) * blk]
            o_ref[t["rows"], (2 * t["g"] + half) * LANES:(2 * t["g"] + half + 1) * LANES] = (
                jnp.where(low, a, b).astype(BF16))


def swa_attention(q, k2, v2, kx2, vx2, sink, *, batch, seq, ctx_len):
    nb = seq // SWA_BLOCK
    ns = nb // SWA_SUB
    kvw = 2 * SWA_KV_DIM
    qspec = pl.BlockSpec((SWA_SUB * SWA_BLOCK, SWA_DIM), lambda b, i: (b * ns + i, 0))
    cur = pl.BlockSpec((SWA_SUB * SWA_BLOCK, kvw), lambda b, i: (b * ns + i, 0))
    halo = lambda off: pl.BlockSpec(
        (SWA_BLOCK, kvw), lambda b, i: (b * nb + jnp.clip(i * SWA_SUB + off, 0, nb - 1), 0))
    xspec = pl.BlockSpec((ctx_len, kvw), lambda b, i: (b, 0))
    return pl.pallas_call(
        functools.partial(_swa_attn_kernel, blocks_per_seq=nb),
        out_shape=jax.ShapeDtypeStruct(q.shape, BF16),
        grid=(batch, ns),
        in_specs=[qspec, halo(-1), cur, halo(SWA_SUB), halo(-1), cur, halo(SWA_SUB), xspec, xspec,
                  _resident(sink.shape)],
        out_specs=qspec,
        compiler_params=_cparams(2),
        name="swa_attention",
    )(q, k2, k2, k2, v2, v2, v2, kx2, vx2, sink)


CNV_RB = 32
CNV_PC = 256
CNV_STRIDE = 4


def _conv_module_kernel(xp_ref, x_ref, xn_ref, mod_ref, nw_ref, w1_ref, dw_ref, lnw_ref, lnb_ref, w2_ref,
                        o_ref, hin_ref, u_ref, c_ref, s_ref, *, tm, tiles_per_seq):
    sh, sc, gt = mod_ref[0:1, :], mod_ref[1:2, :], mod_ref[2:3, :]
    _fill_hin_ext(hin_ref, xp_ref, x_ref, xn_ref, nw_ref[...], sc, sh, tiles_per_seq, HALO, tm)
    d = D_MODEL
    n_slab = d // LANES
    for c in range(d // CNV_PC):
        cols = slice(c * CNV_PC, (c + 1) * CNV_PC)
        gcols = slice(d + c * CNV_PC, d + (c + 1) * CNV_PC)
        a1 = jnp.dot(hin_ref[...], w1_ref[:, cols], preferred_element_type=F32)
        a2 = jnp.dot(hin_ref[...], w1_ref[:, gcols], preferred_element_type=F32)
        u = a1 * jax.nn.sigmoid(a2)
        for t in range(CNV_PC // LANES):
            u_ref[c * (CNV_PC // LANES) + t] = u[:, t * LANES:(t + 1) * LANES]
    pad = CNV_WIDTH // 2
    rows = tm // CNV_STRIDE
    for j in range(n_slab):
        lanes = slice(j * LANES, (j + 1) * LANES)
        for v in range(CNV_STRIDE):
            acc = None
            for k in range(CNV_WIDTH):
                term = dw_ref[k:k + 1, lanes] * u_ref[j, pl.ds(HALO + v + k - pad, rows, stride=CNV_STRIDE), :]
                acc = term if acc is None else acc + term
            c_ref[j, pl.ds(v, rows, stride=CNV_STRIDE), :] = acc
    for rb in range(tm // CNV_RB):
        rsl = slice(rb * CNV_RB, (rb + 1) * CNV_RB)
        xs = [c_ref[j, rsl, :] for j in range(n_slab)]
        tot = xs[0]
        for xj in xs[1:]:
            tot = tot + xj
        mu = jnp.sum(tot, axis=-1, keepdims=True) * (1.0 / d)
        xcs = [xj - mu for xj in xs]
        sq = xcs[0] * xcs[0]
        for xc in xcs[1:]:
            sq = sq + xc * xc
        rstd = lax.rsqrt(jnp.sum(sq, axis=-1, keepdims=True) * (1.0 / d) + EPS)
        for j, xc in enumerate(xcs):
            lanes = slice(j * LANES, (j + 1) * LANES)
            y = xc * rstd * lnw_ref[:, lanes] + lnb_ref[:, lanes]
            s_ref[rsl, lanes] = jax.nn.silu(y).astype(BF16)
    o_ref[...] = x_ref[...] + gt * jnp.dot(s_ref[...], w2_ref[...], preferred_element_type=F32)


def conv_module(h, mod, nw, w1, dw, lnw, lnb, w2, *, seq, tm):
    n, d = h.shape
    n_tiles = n // tm
    tps = seq // tm
    prev, main, nxt = _halo_specs(tm, HALO, n_tiles, d)
    return pl.pallas_call(
        functools.partial(_conv_module_kernel, tm=tm, tiles_per_seq=tps),
        out_shape=jax.ShapeDtypeStruct((n, d), F32),
        grid=(n_tiles,),
        in_specs=[prev, main, nxt, _mod_spec(tps), _resident((1, d)), _resident(w1.shape),
                  _resident(dw.shape), _resident(lnw.shape), _resident(lnb.shape), _resident(w2.shape)],
        out_specs=pl.BlockSpec((tm, d), lambda i: (i, 0)),
        scratch_shapes=[pltpu.VMEM((tm + 2 * HALO, d), BF16),
                        pltpu.VMEM((d // LANES, tm + 2 * HALO, LANES), F32),
                        pltpu.VMEM((d // LANES, tm, LANES), F32),
                        pltpu.VMEM((tm, d), BF16)],
        compiler_params=_cparams(),
        name="conv_module",
    )(h, h, h, mod, nw, w1, dw, lnw, lnb, w2)


def _rope_tables(seq):
    n_freq = SWA_HEAD_DIM // 4
    inv_freq = ROPE_BASE ** (-jnp.arange(n_freq, dtype=F32) / n_freq)
    pos = jnp.arange(seq)
    rows = (pos // GRID_W).astype(F32)
    cols = (pos % GRID_W).astype(F32)
    ang_r = rows[:, None] * inv_freq
    ang_c = cols[:, None] * inv_freq
    ang = jnp.concatenate([ang_r, ang_r, ang_c, ang_c], axis=-1)
    sign = jnp.where((jnp.arange(SWA_HEAD_DIM) % (2 * ROPE_QUARTER)) < ROPE_QUARTER, -1.0, 1.0)
    cos = jnp.tile(jnp.cos(ang), (1, LANES // SWA_HEAD_DIM))
    sin = jnp.tile(jnp.sin(ang) * sign, (1, LANES // SWA_HEAD_DIM))
    return cos, sin


def _tile_rows(seq):
    return min(seq, 512)


def kernel(x, c, ctx, c_ctx, ada_w, ada_b, norm_mix_w, norm_ffn_w, dn_w_in, dn_conv, dn_a_log, dn_dt_bias,
           dn_norm_w, dn_w_out, pool_w_grp, pool_scale, swa_w_qkv, swa_q_norm, swa_k_norm, swa_sink,
           swa_w_out, cnv_w_pw1, cnv_dw, cnv_ln_w, cnv_ln_b, cnv_w_pw2, ffn_w_gate, ffn_w_up, ffn_conv,
           ffn_conv_b, ffn_w_down):
    bsz, seq, d = x.shape
    ctx_len = ctx.shape[1]
    depth = ada_w.shape[0]
    h = x.reshape(bsz * seq, d)
    hc = ctx.reshape(bsz * ctx_len, d)
    p = dict(norm_mix_w=norm_mix_w, norm_ffn_w=norm_ffn_w, dn_w_in=dn_w_in, dn_conv=dn_conv,
             dn_a_log=dn_a_log, dn_dt_bias=dn_dt_bias, dn_norm_w=dn_norm_w, dn_w_out=dn_w_out,
             pool_w_grp=pool_w_grp, pool_scale=pool_scale, swa_w_qkv=swa_w_qkv, swa_q_norm=swa_q_norm,
             swa_k_norm=swa_k_norm, swa_sink=swa_sink, swa_w_out=swa_w_out, cnv_w_pw1=cnv_w_pw1,
             cnv_dw=cnv_dw, cnv_ln_w=cnv_ln_w, cnv_ln_b=cnv_ln_b, cnv_w_pw2=cnv_w_pw2,
             ffn_w_gate=ffn_w_gate, ffn_w_up=ffn_w_up, ffn_conv=ffn_conv, ffn_conv_b=ffn_conv_b,
             ffn_w_down=ffn_w_down)
    dims = dict(bsz=bsz, seq=seq, ctx_len=ctx_len, depth=depth)
    mods = modulation(c, c_ctx, ada_w, ada_b)
    for i in range(depth):
        mod_l, mod_c = layer_modulation(mods, i, bsz)
        h, hc = mixer_layer(i, h, hc, mod_l, mod_c, p, dims)
        h, hc = ffn_layer(i, h, hc, mod_l, mod_c, p, dims)
    return h.reshape(bsz, seq, d)


def modulation(c, c_ctx, ada_w, ada_b):
    bsz, d = c.shape
    n_cond = -(-(bsz + 1) // SUBLANES) * SUBLANES
    cond = jnp.zeros((n_cond, d), F32).at[:bsz].set(c).at[bsz].set(c_ctx)
    return ada_modulation(cond, ada_w, ada_b)


def layer_modulation(mods, i, bsz):
    d = mods.shape[-1] // 6
    mod_l = mods[i, :bsz].reshape(bsz, 6, d)
    mod_c = jnp.broadcast_to(mods[i, bsz].reshape(1, 6, d), (bsz, 6, d))
    return mod_l, mod_c


def _keeps_ctx(i, depth):
    return any((j % N_MIXERS) in CTX_READING_MIXERS for j in range(i + 1, depth))


def ffn_layer(i, h, hc, mod_l, mod_c, p, dims):
    d = h.shape[-1]
    seq, ctx_len = dims["seq"], dims["ctx_len"]
    ffn_args = (p["norm_ffn_w"][i].reshape(1, d), p["ffn_w_gate"].astype(BF16),
                p["ffn_w_up"].astype(BF16), p["ffn_conv"][i], p["ffn_conv_b"][i].reshape(1, FFN_DIM),
                p["ffn_w_down"].astype(BF16))
    h = conv_ffn(h, mod_l, *ffn_args, layer=i, seq=seq, tm=_tile_rows(seq))
    if _keeps_ctx(i, dims["depth"]):
        hc = conv_ffn(hc, mod_c, *ffn_args, layer=i, seq=ctx_len, tm=_tile_rows(ctx_len))
    return h, hc


def mixer_layer(i, h, hc, mod_l, mod_c, p, dims):
    d = h.shape[-1]
    bsz, seq, ctx_len = dims["bsz"], dims["seq"], dims["ctx_len"]
    tm_c = _tile_rows(ctx_len)
    kind, slot = i % N_MIXERS, i // N_MIXERS
    keep_ctx = _keeps_ctx(i, dims["depth"])
    nw_mix = p["norm_mix_w"][i].reshape(1, d)
    lat = dict(seq=seq, tm=_tile_rows(seq))
    cx = dict(seq=ctx_len, tm=tm_c)

    if kind == MIX_DELTANET:
        w_in = p["dn_w_in"][slot]
        wqkv = w_in[:, :3 * DN_DIM].astype(BF16)
        wz = w_in[:, 3 * DN_DIM:4 * DN_DIM].astype(BF16)
        wgt = jnp.zeros((d, LANES), F32).at[:, :4 * DN_HEADS].set(w_in[:, 4 * DN_DIM:]).astype(BF16)
        aneg = jnp.zeros((1, LANES), F32).at[0, 2 * DN_HEADS:4 * DN_HEADS].set(
            -jnp.exp(p["dn_a_log"][slot].astype(F32)).reshape(-1))
        dtb = jnp.zeros((1, LANES), F32).at[0, 2 * DN_HEADS:4 * DN_HEADS].set(
            p["dn_dt_bias"][slot].astype(F32).reshape(-1))
        hnw = p["dn_norm_w"][slot].reshape(1, DN_HEAD_DIM)
        wout = p["dn_w_out"][slot].astype(BF16)
        proj = functools.partial(dn_project, nw=nw_mix, wqkv=wqkv, cw=p["dn_conv"][slot], wz=wz, wgt=wgt,
                                 aneg=aneg, dtb=dtb)
        qc, kc, vc, zc, bgc = proj(hc, mod_c, **cx)
        ql, kl, vl, zl, bgl = proj(h, mod_l, **lat)
        s0 = jnp.zeros((bsz, 2, DN_HEADS, DN_HEAD_DIM, DN_HEAD_DIM), F32)
        ocf, ocb, s_ctx = dn_scan(qc, kc, vc, bgc, s0, batch=bsz, seq=ctx_len)
        olf, olb, _ = dn_scan(ql, kl, vl, bgl, s_ctx, batch=bsz, seq=seq)
        h = dn_readout(olf, olb, zl, h, mod_l, hnw, wout, **lat)
        if keep_ctx:
            hc = dn_readout(ocf, ocb, zc, hc, mod_c, hnw, wout, **cx)
    elif kind == MIX_POOL:
        wg = p["pool_w_grp"][slot].astype(BF16)
        scale = p["pool_scale"][slot].reshape(1, d)
        h = pool_mixer(h, mod_l, nw_mix, wg, scale, **lat)
        if keep_ctx:
            hc = pool_mixer(hc, mod_c, nw_mix, wg, scale, **cx)
    elif kind == MIX_SWA:
        assert not keep_ctx
        wq = p["swa_w_qkv"][slot]
        dup = lambda w_: jnp.repeat(w_.reshape(d, SWA_KV_HEADS, 1, SWA_HEAD_DIM), 2, axis=2).reshape(
            d, 2 * SWA_KV_DIM)
        w_all = jnp.concatenate([wq[:, :SWA_DIM], dup(wq[:, SWA_DIM:SWA_DIM + SWA_KV_DIM]),
                                 dup(wq[:, SWA_DIM + SWA_KV_DIM:])], axis=1).astype(BF16)
        hnw = jnp.concatenate([jnp.tile(p["swa_q_norm"][slot], SWA_HEADS),
                               jnp.tile(p["swa_k_norm"][slot], 2 * SWA_KV_HEADS)]).reshape(1, SWA_NORM_COLS)
        li = jnp.arange(SWA_PC)
        seg = jnp.where((li[:, None] // SWA_HEAD_DIM) == (li[None, :] // SWA_HEAD_DIM),
                        1.0 / SWA_HEAD_DIM, 0.0).astype(BF16)
        cos, sin = _rope_tables(seq)
        sink = jnp.broadcast_to(p["swa_sink"][slot].astype(F32).reshape(SWA_HEADS, 1), (SWA_HEADS, LANES))
        ql, k2, v2 = swa_qkv(h, mod_l, nw_mix, w_all, hnw, seg, cos, sin, rope=True, **lat)
        _, kx2, vx2 = swa_qkv(hc, mod_c, nw_mix, w_all, hnw, seg, cos[:tm_c], sin[:tm_c], rope=False, **cx)
        o = swa_attention(ql, k2, v2, kx2, vx2, sink, batch=bsz, seq=seq, ctx_len=ctx_len)
        h = matmul_residual(o, h, mod_l, p["swa_w_out"][slot].astype(BF16), **lat)
    else:
        args = (nw_mix, p["cnv_w_pw1"][slot].astype(BF16), p["cnv_dw"][slot], p["cnv_ln_w"][slot].reshape(1, d),
                p["cnv_ln_b"][slot].reshape(1, d), p["cnv_w_pw2"][slot].astype(BF16))
        h = conv_module(h, mod_l, *args, **lat)
        if keep_ctx:
            hc = conv_module(hc, mod_c, *args, **cx)
    return h, hc
```

```python
import functools

import jax
import jax.numpy as jnp
from jax import lax
from jax.experimental import pallas as pl
from jax.experimental.pallas import tpu as pltpu

D_MODEL = 1024
EPS = 1e-6
NEG_INF = -1e30
GRID_W = 64
ROPE_BASE = 10000.0
N_MIXERS = 4
MIX_DELTANET, MIX_POOL, MIX_SWA, MIX_CONV = 0, 1, 2, 3
CTX_READING_MIXERS = (MIX_DELTANET, MIX_SWA)

DN_HEADS = 8
DN_HEAD_DIM = 128
DN_DIM = DN_HEADS * DN_HEAD_DIM
DN_CONV = 5
DN_CHUNK = 128
DN_INV_BASE = 2
DN_SUB = 4

POOL_WINDOWS = (2, 4, 8, 16)
POOL_GROUP = D_MODEL // len(POOL_WINDOWS)

SWA_HEADS = 16
SWA_KV_HEADS = 4
SWA_HEAD_DIM = 64
SWA_GROUP = SWA_HEADS // SWA_KV_HEADS
SWA_DIM = SWA_HEADS * SWA_HEAD_DIM
SWA_KV_DIM = SWA_KV_HEADS * SWA_HEAD_DIM
SWA_WINDOW = 128
SWA_BLOCK = 128
SWA_SUB = 4

CNV_WIDTH = 31
FFN_DIM = 2816
FFN_CONV = 3

LANES = 128
SUBLANES = 8
VMEM_LIMIT_BYTES = 56 * 1024 * 1024

F32 = jnp.float32
BF16 = jnp.bfloat16
HALO = 2 * SUBLANES
POOL_HALO = SUBLANES
POOL_STRIDE = 4


def _cparams(n_axes=1):
    return pltpu.CompilerParams(
        dimension_semantics=("arbitrary",) * n_axes,
        vmem_limit_bytes=VMEM_LIMIT_BYTES,
    )


def _resident(shape):
    nd = len(shape)
    return pl.BlockSpec(shape, lambda *_: (0,) * nd, pipeline_mode=pl.Buffered(1))


def _resident_layer(shape, layer):
    return pl.BlockSpec((None,) + tuple(shape[1:]), lambda *_: (layer,) + (0,) * (len(shape) - 1),
                        pipeline_mode=pl.Buffered(1))


def _halo_specs(tm, hb, n_tiles, width):
    r = tm // hb
    last = n_tiles * r - 1
    prev = pl.BlockSpec((hb, width), lambda i: (jnp.maximum(i * r - 1, 0), 0))
    main = pl.BlockSpec((tm, width), lambda i: (i, 0))
    nxt = pl.BlockSpec((hb, width), lambda i: (jnp.minimum((i + 1) * r, last), 0))
    return prev, main, nxt


def _mod_spec(tiles_per_seq):
    return pl.BlockSpec((None, 6, D_MODEL), lambda i: (i // tiles_per_seq, 0, 0))


def _rms_mod(x, nw, sc, sh):
    ms = jnp.mean(x * x, axis=-1, keepdims=True)
    return (x * lax.rsqrt(ms + EPS) * nw) * (1.0 + sc) + sh


def _fill_hin_ext(hin_ref, xp_ref, x_ref, xn_ref, nw, sc, sh, tiles_per_seq, hb, tm):
    i = pl.program_id(0)
    pos = lax.rem(i, tiles_per_seq)
    first = pos == 0
    last = pos == tiles_per_seq - 1
    hp = jnp.where(first, 0.0, _rms_mod(xp_ref[...], nw, sc, sh))
    hn = jnp.where(last, 0.0, _rms_mod(xn_ref[...], nw, sc, sh))
    hin_ref[0:hb, :] = hp.astype(hin_ref.dtype)
    hin_ref[hb:hb + tm, :] = _rms_mod(x_ref[...], nw, sc, sh).astype(hin_ref.dtype)
    hin_ref[hb + tm:hb + tm + hb, :] = hn.astype(hin_ref.dtype)


def _shift_rows(a, d, n_rows):
    if d == 0:
        return a
    return pltpu.roll(a, (-d) % n_rows, 0)


ADA_TN = 1536


def _ada_kernel(cond_ref, w_ref, b_ref, o_ref):
    s = jax.nn.silu(cond_ref[...])
    o_ref[...] = jnp.dot(s, w_ref[...], preferred_element_type=F32,
                         precision=lax.Precision.HIGHEST) + b_ref[...]


def ada_modulation(cond, ada_w, ada_b):
    depth, d, n = ada_w.shape
    rows = cond.shape[0]
    return pl.pallas_call(
        _ada_kernel,
        out_shape=jax.ShapeDtypeStruct((depth, rows, n), F32),
        grid=(depth, n // ADA_TN),
        in_specs=[
            pl.BlockSpec((rows, d), lambda l, j: (0, 0)),
            pl.BlockSpec((None, d, ADA_TN), lambda l, j: (l, 0, j)),
            pl.BlockSpec((None, 1, ADA_TN), lambda l, j: (l, 0, j)),
        ],
        out_specs=pl.BlockSpec((None, rows, ADA_TN), lambda l, j: (l, 0, j)),
        compiler_params=_cparams(2),
        name="ada_modulation",
    )(cond, ada_w, ada_b.reshape(depth, 1, n))


FFN_FC = 256
FFN_DOWN_GROUP = 6


def _ffn_kernel(xp_ref, x_ref, xn_ref, mod_ref, nw_ref, wg_ref, wu_ref, cw_ref, cb_ref, wd_ref,
                o_ref, hin_ref, act_ref, acc_ref, *, tm, tiles_per_seq):
    sh, sc, gt = mod_ref[3:4, :], mod_ref[4:5, :], mod_ref[5:6, :]
    _fill_hin_ext(hin_ref, xp_ref, x_ref, xn_ref, nw_ref[...], sc, sh, tiles_per_seq, HALO, tm)
    n_ext = tm + 2 * HALO
    n_chunks = FFN_DIM // FFN_FC
    group_start = 0
    for c in range(n_chunks):
        cols = slice(c * FFN_FC, (c + 1) * FFN_FC)
        g_ext = jnp.dot(hin_ref[...], wg_ref[:, cols], preferred_element_type=F32)
        u = jnp.dot(hin_ref[HALO:HALO + tm, :], wu_ref[:, cols], preferred_element_type=F32)
        g = cb_ref[:, cols]
        for k in range(FFN_CONV):
            g = g + cw_ref[k:k + 1, cols] * _shift_rows(g_ext, k - 1, n_ext)[HALO:HALO + tm, :]
        act_ref[:, cols] = (jax.nn.silu(g) * u).astype(BF16)
        if (c + 1) % FFN_DOWN_GROUP == 0 or c == n_chunks - 1:
            ks = slice(group_start * FFN_FC, (c + 1) * FFN_FC)
            y = jnp.dot(act_ref[:, ks], wd_ref[ks, :], preferred_element_type=F32)
            if group_start == 0:
                acc_ref[...] = y
            else:
                acc_ref[...] += y
            group_start = c + 1
    o_ref[...] = x_ref[...] + gt * acc_ref[...]


def conv_ffn(h, mod, nw, wg, wu, cw, cb, wd, *, layer, seq, tm):
    n, d = h.shape
    n_tiles = n // tm
    tps = seq // tm
    prev, main, nxt = _halo_specs(tm, HALO, n_tiles, d)
    return pl.pallas_call(
        functools.partial(_ffn_kernel, tm=tm, tiles_per_seq=tps),
        out_shape=jax.ShapeDtypeStruct((n, d), F32),
        grid=(n_tiles,),
        in_specs=[prev, main, nxt, _mod_spec(tps), _resident((1, d)),
                  _resident_layer(wg.shape, layer), _resident_layer(wu.shape, layer), _resident(cw.shape),
                  _resident(cb.shape), _resident_layer(wd.shape, layer)],
        out_specs=pl.BlockSpec((tm, d), lambda i: (i, 0)),
        scratch_shapes=[pltpu.VMEM((tm + 2 * HALO, d), BF16), pltpu.VMEM((tm, FFN_DIM), BF16),
                        pltpu.VMEM((tm, d), F32)],
        compiler_params=_cparams(),
        name="conv_ffn",
    )(h, h, h, mod, nw, wg, wu, cw, cb, wd)


def _mm_res_kernel(a_ref, res_ref, mod_ref, w_ref, o_ref):
    gt = mod_ref[2:3, :]
    o_ref[...] = res_ref[...] + gt * jnp.dot(a_ref[...], w_ref[...], preferred_element_type=F32)


def matmul_residual(a, res, mod, w, *, seq, tm):
    n, k = a.shape
    d = w.shape[1]
    tps = seq // tm
    return pl.pallas_call(
        _mm_res_kernel,
        out_shape=jax.ShapeDtypeStruct((n, d), F32),
        grid=(n // tm,),
        in_specs=[pl.BlockSpec((tm, k), lambda i: (i, 0)), pl.BlockSpec((tm, d), lambda i: (i, 0)),
                  _mod_spec(tps), _resident(w.shape)],
        out_specs=pl.BlockSpec((tm, d), lambda i: (i, 0)),
        compiler_params=_cparams(),
        name="matmul_residual",
    )(a, res, mod, w)


DN_PC = 256
DN_STRIDE = 4


def _dn_proj_kernel(xp_ref, x_ref, xn_ref, mod_ref, nw_ref, wqkv_ref, cw_ref, wz_ref, wgt_ref,
                    aneg_ref, dtb_ref, q_ref, k_ref, v_ref, z_ref, bg_ref, hin_ref, p_ref, a_ref,
                    *, tm, tiles_per_seq):
    sh, sc = mod_ref[0:1, :], mod_ref[1:2, :]
    _fill_hin_ext(hin_ref, xp_ref, x_ref, xn_ref, nw_ref[...], sc, sh, tiles_per_seq, HALO, tm)
    outs = (q_ref, k_ref, v_ref)
    per_out = DN_DIM // DN_PC
    heads_pc = DN_PC // DN_HEAD_DIM
    rows = tm // DN_STRIDE
    for c in range(3 * per_out):
        cols = slice(c * DN_PC, (c + 1) * DN_PC)
        p_ext = jnp.dot(hin_ref[...], wqkv_ref[:, cols], preferred_element_type=F32)
        which, cc = divmod(c, per_out)
        for hh in range(heads_pc):
            slot = (c % 2) * heads_pc + hh
            lanes = slice(c * DN_PC + hh * DN_HEAD_DIM, c * DN_PC + (hh + 1) * DN_HEAD_DIM)
            p_ref[slot] = p_ext[:, hh * DN_HEAD_DIM:(hh + 1) * DN_HEAD_DIM]
            for v in range(DN_STRIDE):
                acc = None
                for k in range(DN_CONV):
                    start = HALO + v + k - DN_CONV // 2
                    term = cw_ref[k:k + 1, lanes] * p_ref[slot, pl.ds(start, rows, stride=DN_STRIDE), :]
                    acc = term if acc is None else acc + term
                ah = jax.nn.silu(acc)
                if which < 2:
                    ah = ah * lax.rsqrt(jnp.sum(ah * ah, axis=-1, keepdims=True) + EPS)
                    if which == 0:
                        ah = ah * (DN_HEAD_DIM ** -0.5)
                a_ref[slot, pl.ds(v, rows, stride=DN_STRIDE), :] = ah
            lo = cc * DN_PC + hh * DN_HEAD_DIM
            outs[which][:, lo:lo + DN_HEAD_DIM] = a_ref[slot].astype(BF16)
    hm = hin_ref[HALO:HALO + tm, :]
    z_ref[...] = jnp.dot(hm, wz_ref[...], preferred_element_type=F32).astype(BF16)
    gates = jnp.dot(hm, wgt_ref[...], preferred_element_type=F32)
    lane = lax.broadcasted_iota(jnp.int32, gates.shape, 1)
    beta = jax.nn.sigmoid(gates)
    g = aneg_ref[...] * jax.nn.softplus(gates + dtb_ref[...])
    bg_ref[...] = jnp.where(lane < 2 * DN_HEADS, beta, g)


def dn_project(h, mod, nw, wqkv, cw, wz, wgt, aneg, dtb, *, seq, tm):
    n, d = h.shape
    n_tiles = n // tm
    tps = seq // tm
    prev, main, nxt = _halo_specs(tm, HALO, n_tiles, d)
    row = lambda w: pl.BlockSpec((tm, w), lambda i: (i, 0))
    return pl.pallas_call(
        functools.partial(_dn_proj_kernel, tm=tm, tiles_per_seq=tps),
        out_shape=[jax.ShapeDtypeStruct((n, DN_DIM), BF16)] * 4 + [jax.ShapeDtypeStruct((n, LANES), F32)],
        grid=(n_tiles,),
        in_specs=[prev, main, nxt, _mod_spec(tps), _resident((1, d)), _resident(wqkv.shape),
                  _resident(cw.shape), _resident(wz.shape), _resident(wgt.shape),
                  _resident(aneg.shape), _resident(dtb.shape)],
        out_specs=[row(DN_DIM)] * 4 + [row(LANES)],
        scratch_shapes=[pltpu.VMEM((tm + 2 * HALO, d), BF16),
                        pltpu.VMEM((2 * DN_PC // DN_HEAD_DIM, tm + 2 * HALO, DN_HEAD_DIM), F32),
                        pltpu.VMEM((2 * DN_PC // DN_HEAD_DIM, tm, DN_HEAD_DIM), F32)],
        compiler_params=_cparams(),
        name="dn_project",
    )(h, h, h, mod, nw, wqkv, cw, wz, wgt, aneg, dtb)


def _bdot(a, b):
    return jnp.dot(a.astype(BF16), b.astype(BF16), preferred_element_type=F32)


def _exact_mask_dot(mask, x):
    m = mask.astype(BF16)
    hi = x.astype(BF16)
    r1 = x - hi.astype(F32)
    mid = r1.astype(BF16)
    lo = (r1 - mid.astype(F32)).astype(BF16)
    return (jnp.dot(m, hi, preferred_element_type=F32) + jnp.dot(m, mid, preferred_element_type=F32)
            + jnp.dot(m, lo, preferred_element_type=F32))


def _unit_tri_inverses(lmats, lowers, eye, blk):
    c = lmats[0].shape[0]
    xs = [eye - jnp.where(blk[0], l, 0.0) for l in lmats]
    size = DN_INV_BASE
    for lvl in range(1, len(blk)):
        sel = jnp.logical_and(blk[lvl], jnp.logical_not(blk[lvl - 1]))
        if size < SUBLANES:
            ts = [_bdot(jnp.where(sel, l, 0.0), x) for l, x in zip(lmats, xs)]
            xs = [x - _bdot(x, t) for x, t in zip(xs, ts)]
            size *= 2
            continue
        groups = range(c // size)

        def active(a, lower):
            return jnp.concatenate([a[g * size:(g + 1) * size] for g in groups if (g % 2 == 1) == lower], axis=0)

        def expand(a, lower, rest):
            parts, k = [], 0
            for g in groups:
                if (g % 2 == 1) == lower:
                    parts.append(a[k * size:(k + 1) * size])
                    k += 1
                else:
                    parts.append(rest[g * size:(g + 1) * size])
            return jnp.concatenate(parts, axis=0)

        zero = jnp.zeros_like(eye)
        ts = [_bdot(active(jnp.where(sel, l, 0.0), lo), x) for l, lo, x in zip(lmats, lowers, xs)]
        us = [_bdot(active(x, lo), expand(t, lo, zero)) for x, lo, t in zip(xs, lowers, ts)]
        xs = [expand(active(x, lo) - u, lo, x) for x, lo, u in zip(xs, lowers, us)]
        size *= 2
    return xs


def _dn_scan_kernel(qf_ref, kf_ref, vf_ref, bgf_ref, qb_ref, kb_ref, vb_ref, bgb_ref, s0_ref,
                    of_ref, ob_ref, sfin_ref, s_ref, *, n_steps, n_sub):
    n = pl.program_id(1)
    c = DN_CHUNK

    @pl.when(n == 0)
    def _():
        s_ref[...] = s0_ref[...]

    ri = lax.broadcasted_iota(jnp.int32, (c, c), 0)
    ci = lax.broadcasted_iota(jnp.int32, (c, c), 1)
    eye = (ri == ci).astype(F32)
    blk = []
    bs = DN_INV_BASE
    while bs <= c:
        blk.append((ri // bs) == (ci // bs))
        bs *= 2

    dirs = ((qf_ref, kf_ref, vf_ref, bgf_ref, of_ref, ri >= ci, ri > ci),
            (qb_ref, kb_ref, vb_ref, bgb_ref, ob_ref, ri <= ci, ri < ci))
    inst = []
    for d, (q_ref, k_ref, v_ref, bg_ref, o_ref, incl, strict) in enumerate(dirs):
        for sub in range(n_sub):
            rows = slice(sub * c, (sub + 1) * c)
            bg = bg_ref[rows, :]
            is_g = lax.broadcasted_iota(jnp.int32, bg.shape, 1) >= 2 * DN_HEADS
            bgg = jnp.where(is_g, bg, 0.0)
            cum = _exact_mask_dot(incl, bgg)
            cum_t = cum.T
            tot = jnp.sum(bgg, axis=0, keepdims=True)
            egc = jnp.exp(cum)
            ekd = jnp.exp(tot - cum)
            etot = jnp.exp(tot)
            for h in range(DN_HEADS):
                lb = d * DN_HEADS + h
                lg = 2 * DN_HEADS + lb
                hs = slice(h * DN_HEAD_DIM, (h + 1) * DN_HEAD_DIM)
                inst.append(dict(
                    d=d, h=h, sub=sub, o=o_ref, rows=rows, hs=hs, incl=incl, strict=strict,
                    q=q_ref[rows, hs], k=k_ref[rows, hs], v=v_ref[rows, hs],
                    beta=bg[:, lb:lb + 1], gc=cum[:, lg:lg + 1], gc_row=cum_t[lg:lg + 1, :],
                    egc=egc[:, lg:lg + 1], ekd=ekd[:, lg:lg + 1], etot=etot[:, lg:lg + 1]))
    nt = (((1,), (1,)), ((), ()))
    tn = (((0,), (0,)), ((), ()))
    kq = [lax.dot_general(jnp.concatenate([t["k"], t["q"]], axis=0), t["k"], nt, preferred_element_type=F32)
          for t in inst]
    decays = [jnp.where(t["incl"], jnp.exp(jnp.where(t["incl"], t["gc"] - t["gc_row"], 0.0)), 0.0) for t in inst]
    lmats = [jnp.where(t["strict"], t["beta"] * r[:c] * dc, 0.0) for t, r, dc in zip(inst, kq, decays)]
    qks = [(r[c:] * dc).astype(BF16) for r, dc in zip(kq, decays)]
    tinvs = _unit_tri_inverses(lmats, [t["d"] == 0 for t in inst], eye, blk)
    rhss = [jnp.concatenate([t["v"].astype(F32) * t["beta"], t["k"].astype(F32) * (t["beta"] * t["egc"])], axis=1)
            for t in inst]
    uws = [_bdot(ti, r) for ti, r in zip(tinvs, rhss)]
    wq_lhs = [jnp.concatenate([uw[:, DN_HEAD_DIM:], t["q"].astype(F32) * t["egc"]], axis=0).astype(BF16)
              for t, uw in zip(inst, uws)]
    k_decs = [(t["k"].astype(F32) * t["ekd"]).astype(BF16) for t in inst]
    for pos in range(n_sub):
        cur = [i for i, t in enumerate(inst) if t["sub"] == (pos if t["d"] == 0 else n_sub - 1 - pos)]
        wqs = [jnp.dot(wq_lhs[i], s_ref[inst[i]["d"], inst[i]["h"]].astype(BF16), preferred_element_type=F32)
               for i in cur]
        v_news = [(uws[i][:, :DN_HEAD_DIM] - wq[:c]).astype(BF16) for i, wq in zip(cur, wqs)]
        outs = [wq[c:] + jnp.dot(qks[i], vn, preferred_element_type=F32) for i, wq, vn in zip(cur, wqs, v_news)]
        upds = [lax.dot_general(k_decs[i], vn, tn, preferred_element_type=F32) for i, vn in zip(cur, v_news)]
        for i, o, upd in zip(cur, outs, upds):
            t = inst[i]
            t["o"][t["rows"], t["hs"]] = o.astype(t["o"].dtype)
            s_ref[t["d"], t["h"]] = s_ref[t["d"], t["h"]] * t["etot"] + upd

    @pl.when(n == n_steps - 1)
    def _():
        sfin_ref[...] = s_ref[...]


def dn_scan(q, k, v, bg, s0, *, batch, seq):
    n_sub = min(DN_SUB, seq // DN_CHUNK)
    rows = n_sub * DN_CHUNK
    nc = seq // rows
    fwd = lambda w: pl.BlockSpec((rows, w), lambda b, n: (b * nc + n, 0))
    bwd = lambda w: pl.BlockSpec((rows, w), lambda b, n: (b * nc + nc - 1 - n, 0))
    st = pl.BlockSpec((None, 2, DN_HEADS, DN_HEAD_DIM, DN_HEAD_DIM), lambda b, n: (b, 0, 0, 0, 0))
    return pl.pallas_call(
        functools.partial(_dn_scan_kernel, n_steps=nc, n_sub=n_sub),
        out_shape=[jax.ShapeDtypeStruct(q.shape, BF16), jax.ShapeDtypeStruct(q.shape, BF16),
                   jax.ShapeDtypeStruct(s0.shape, F32)],
        grid=(batch, nc),
        in_specs=[fwd(DN_DIM), fwd(DN_DIM), fwd(DN_DIM), fwd(LANES),
                  bwd(DN_DIM), bwd(DN_DIM), bwd(DN_DIM), bwd(LANES), st],
        out_specs=[fwd(DN_DIM), bwd(DN_DIM), st],
        scratch_shapes=[pltpu.VMEM((2, DN_HEADS, DN_HEAD_DIM, DN_HEAD_DIM), F32)],
        compiler_params=_cparams(2),
        name="dn_scan",
    )(q, k, v, bg, q, k, v, bg, s0)


def _dn_readout_kernel(of_ref, ob_ref, z_ref, res_ref, mod_ref, nw_ref, w_ref, o_ref, y_ref):
    gt = mod_ref[2:3, :]
    for h in range(DN_HEADS):
        hs = slice(h * DN_HEAD_DIM, (h + 1) * DN_HEAD_DIM)
        o = of_ref[:, hs].astype(F32) + ob_ref[:, hs].astype(F32)
        ms = jnp.mean(o * o, axis=-1, keepdims=True)
        y = o * lax.rsqrt(ms + EPS) * nw_ref[...]
        y_ref[:, hs] = (y * jax.nn.silu(z_ref[:, hs].astype(F32))).astype(BF16)
    o_ref[...] = res_ref[...] + gt * jnp.dot(y_ref[...], w_ref[...], preferred_element_type=F32)


def dn_readout(o_f, o_b, z, res, mod, nw, w, *, seq, tm):
    n, d = res.shape
    tps = seq // tm
    row = lambda w_: pl.BlockSpec((tm, w_), lambda i: (i, 0))
    return pl.pallas_call(
        _dn_readout_kernel,
        out_shape=jax.ShapeDtypeStruct((n, d), F32),
        grid=(n // tm,),
        in_specs=[row(DN_DIM), row(DN_DIM), row(DN_DIM), row(d), _mod_spec(tps),
                  _resident(nw.shape), _resident(w.shape)],
        out_specs=row(d),
        scratch_shapes=[pltpu.VMEM((tm, DN_DIM), BF16)],
        compiler_params=_cparams(),
        name="dn_readout",
    )(o_f, o_b, z, res, mod, nw, w)


def _pool_kernel(xp_ref, x_ref, xn_ref, mod_ref, nw_ref, wg_ref, scale_ref, o_ref, hin_ref, d_ref, y_ref,
                 *, tm, tiles_per_seq, seq):
    sh, sc, gt = mod_ref[0:1, :], mod_ref[1:2, :], mod_ref[2:3, :]
    hb = POOL_HALO
    pos = lax.rem(pl.program_id(0), tiles_per_seq)
    hp = jnp.where(pos == 0, 0.0, _rms_mod(xp_ref[...], nw_ref[...], sc, sh))
    hn = jnp.where(pos == tiles_per_seq - 1, 0.0, _rms_mod(xn_ref[...], nw_ref[...], sc, sh))
    hm = _rms_mod(x_ref[...], nw_ref[...], sc, sh)
    n_slab = D_MODEL // LANES
    for j in range(n_slab):
        lanes = slice(j * LANES, (j + 1) * LANES)
        hin_ref[j, 0:hb, :] = hp[:, lanes]
        hin_ref[j, hb:hb + tm, :] = hm[:, lanes]
        hin_ref[j, hb + tm:hb + tm + hb, :] = hn[:, lanes]
    rows = tm // POOL_STRIDE
    slabs_per_group = POOL_GROUP // LANES
    ti = pos * tm + POOL_STRIDE * lax.broadcasted_iota(jnp.int32, (rows, 1), 0)
    for gi, win in enumerate(POOL_WINDOWS):
        half = win // 2
        for v in range(POOL_STRIDE):
            t = ti + v
            cnt = (jnp.minimum(t + (win - half), seq) - jnp.maximum(t - half, 0)).astype(F32)
            for j in range(gi * slabs_per_group, (gi + 1) * slabs_per_group):
                s = None
                for dlt in range(-half, win - half):
                    term = hin_ref[j, pl.ds(hb + v + dlt, rows, stride=POOL_STRIDE), :]
                    s = term if s is None else s + term
                d_ref[j, pl.ds(v, rows, stride=POOL_STRIDE), :] = (
                    s / cnt - hin_ref[j, pl.ds(hb + v, rows, stride=POOL_STRIDE), :])
        cols = slice(gi * POOL_GROUP, (gi + 1) * POOL_GROUP)
        dd = jnp.concatenate([d_ref[j] for j in range(gi * slabs_per_group, (gi + 1) * slabs_per_group)], axis=1)
        y_ref[:, cols] = jnp.dot(dd.astype(BF16), wg_ref[gi], preferred_element_type=F32)
    o_ref[...] = x_ref[...] + gt * (y_ref[...] * scale_ref[...])


def pool_mixer(h, mod, nw, wg, scale, *, seq, tm):
    n, d = h.shape
    n_tiles = n // tm
    tps = seq // tm
    prev, main, nxt = _halo_specs(tm, POOL_HALO, n_tiles, d)
    return pl.pallas_call(
        functools.partial(_pool_kernel, tm=tm, tiles_per_seq=tps, seq=seq),
        out_shape=jax.ShapeDtypeStruct((n, d), F32),
        grid=(n_tiles,),
        in_specs=[prev, main, nxt, _mod_spec(tps), _resident((1, d)), _resident(wg.shape),
                  _resident(scale.shape)],
        out_specs=pl.BlockSpec((tm, d), lambda i: (i, 0)),
        scratch_shapes=[pltpu.VMEM((d // LANES, tm + 2 * POOL_HALO, LANES), F32),
                        pltpu.VMEM((d // LANES, tm, LANES), F32), pltpu.VMEM((tm, d), F32)],
        compiler_params=_cparams(),
        name="pool_mixer",
    )(h, h, h, mod, nw, wg, scale)


SWA_QKV_COLS = SWA_DIM + 4 * SWA_KV_DIM
SWA_NORM_COLS = SWA_DIM + 2 * SWA_KV_DIM
SWA_PC = 256
ROPE_QUARTER = SWA_HEAD_DIM // 4


def _swa_qkv_kernel(x_ref, mod_ref, nw_ref, w_ref, hnw_ref, seg_ref, cos_ref, sin_ref,
                    q_ref, k_ref, v_ref, *, rope):
    sh, sc = mod_ref[0:1, :], mod_ref[1:2, :]
    hin = _rms_mod(x_ref[...], nw_ref[...], sc, sh).astype(BF16)
    pc = SWA_PC
    n_chunks = SWA_QKV_COLS // pc
    q_chunks = SWA_DIM // pc
    k_chunks = 2 * SWA_KV_DIM // pc
    for j in range(n_chunks):
        cols = slice(j * pc, (j + 1) * pc)
        p = jnp.dot(hin, w_ref[:, cols], preferred_element_type=F32)
        if j < q_chunks + k_chunks:
            ms = jnp.dot((p * p).astype(BF16), seg_ref[...], preferred_element_type=F32)
            p = p * lax.rsqrt(ms + EPS) * hnw_ref[:, cols]
            if rope:
                lane = lax.broadcasted_iota(jnp.int32, (p.shape[0], LANES), 1)
                up = lax.rem(lane, 2 * ROPE_QUARTER) < ROPE_QUARTER
                halves = []
                for t in range(pc // LANES):
                    ph = p[:, t * LANES:(t + 1) * LANES]
                    partner = jnp.where(up, pltpu.roll(ph, LANES - ROPE_QUARTER, 1),
                                        pltpu.roll(ph, ROPE_QUARTER, 1))
                    halves.append(ph * cos_ref[...] + partner * sin_ref[...])
                p = jnp.concatenate(halves, axis=1)
        if j < q_chunks:
            q_ref[:, cols] = (p * (SWA_HEAD_DIM ** -0.5)).astype(BF16)
        elif j < q_chunks + k_chunks:
            k_ref[:, (j - q_chunks) * pc:(j - q_chunks + 1) * pc] = p.astype(BF16)
        else:
            jj = j - q_chunks - k_chunks
            v_ref[:, jj * pc:(jj + 1) * pc] = p.astype(BF16)


def swa_qkv(h, mod, nw, w, hnw, seg, cos, sin, *, seq, tm, rope):
    n, d = h.shape
    tps = seq // tm
    row = lambda w_: pl.BlockSpec((tm, w_), lambda i: (i, 0))
    tab = pl.BlockSpec((tm, LANES), lambda i: (i % tps, 0))
    kvw = 2 * SWA_KV_DIM
    return pl.pallas_call(
        functools.partial(_swa_qkv_kernel, rope=rope),
        out_shape=[jax.ShapeDtypeStruct((n, SWA_DIM), BF16), jax.ShapeDtypeStruct((n, kvw), BF16),
                   jax.ShapeDtypeStruct((n, kvw), BF16)],
        grid=(n // tm,),
        in_specs=[row(d), _mod_spec(tps), _resident((1, d)), _resident(w.shape), _resident(hnw.shape),
                  _resident(seg.shape), tab, tab],
        out_specs=[row(SWA_DIM), row(kvw), row(kvw)],
        compiler_params=_cparams(),
        name="swa_qkv",
    )(h, mod, nw, w, hnw, seg, cos, sin)


def _swa_attn_kernel(q_ref, kp_ref, kc_ref, kn_ref, vp_ref, vc_ref, vn_ref, kx_ref, vx_ref, sink_ref,
                     o_ref, *, blocks_per_seq):
    i = pl.program_id(1)
    blk = SWA_BLOCK
    ri = lax.broadcasted_iota(jnp.int32, (SWA_GROUP * blk, blk), 0) % blk
    ci = lax.broadcasted_iota(jnp.int32, (SWA_GROUP * blk, blk), 1)
    lane = lax.broadcasted_iota(jnp.int32, (blk, LANES), 1)
    low = lane < SWA_HEAD_DIM
    nt = (((1,), (1,)), ((), ()))
    lane_tiles = lambda a: [a[:, t * LANES:(t + 1) * LANES] for t in range(a.shape[1] // LANES)]
    inst = []
    for sub in range(SWA_SUB):
        rows = slice(sub * blk, (sub + 1) * blk)
        prev = (kp_ref, vp_ref, slice(0, blk)) if sub == 0 else (kc_ref, vc_ref, slice((sub - 1) * blk, sub * blk))
        nxt = ((kn_ref, vn_ref, slice(0, blk)) if sub == SWA_SUB - 1
               else (kc_ref, vc_ref, slice((sub + 1) * blk, (sub + 2) * blk)))
        gblock = i * SWA_SUB + sub
        ok_prev = jnp.logical_and(ci >= ri, gblock > 0)
        ok_next = jnp.logical_and(ci <= ri, gblock < blocks_per_seq - 1)
        for g in range(SWA_KV_HEADS):
            inst.append(dict(g=g, rows=rows, gs=slice(g * LANES, (g + 1) * LANES), prev=prev, nxt=nxt,
                             ok_prev=ok_prev, ok_next=ok_next))

    def stacked_queries(t):
        qs = []
        for half in range(2):
            qt = q_ref[t["rows"], (2 * t["g"] + half) * LANES:(2 * t["g"] + half + 1) * LANES]
            qs.append(jnp.where(low, qt, jnp.zeros_like(qt)))
            qs.append(jnp.where(low, jnp.zeros_like(qt), qt))
        return jnp.concatenate(qs, axis=0)

    def sink_column(g):
        return jnp.concatenate(
            [jnp.broadcast_to(sink_ref[SWA_GROUP * g + hh:SWA_GROUP * g + hh + 1, 0:1], (blk, 1))
             for hh in range(SWA_GROUP)], axis=0)

    def window(t, which, cur_ref, ctx_ref):
        p_ref, n_ref = t["prev"][which], t["nxt"][which]
        return jnp.concatenate([p_ref[t["prev"][2], t["gs"]], cur_ref[t["rows"], t["gs"]],
                                n_ref[t["nxt"][2], t["gs"]], ctx_ref[:, t["gs"]]], axis=0)

    qst = [stacked_queries(t) for t in inst]
    sinks = [sink_column(t["g"]) for t in inst]
    scores = [lax.dot_general(q, window(t, 0, kc_ref, kx_ref), nt, preferred_element_type=F32)
              for q, t in zip(qst, inst)]
    pieces = []
    for s, t in zip(scores, inst):
        ts = lane_tiles(s)
        pieces.append([jnp.where(t["ok_prev"], ts[0], NEG_INF), ts[1], jnp.where(t["ok_next"], ts[2], NEG_INF)]
                      + ts[3:])
    ms = [jnp.maximum(jnp.max(functools.reduce(jnp.maximum, ps), axis=-1, keepdims=True), sk)
          for ps, sk in zip(pieces, sinks)]
    es = [[jnp.exp(p - m) for p in ps] for ps, m in zip(pieces, ms)]
    invs = [1.0 / (jnp.sum(functools.reduce(jnp.add, e), axis=-1, keepdims=True) + jnp.exp(sk - m))
            for e, sk, m in zip(es, sinks, ms)]
    pvs = [jnp.dot(jnp.concatenate(e, axis=1).astype(BF16), window(t, 1, vc_ref, vx_ref),
                   preferred_element_type=F32) * inv
           for e, t, inv in zip(es, inst, invs)]
    for t, pv in zip(inst, pvs):
        for half in range(2):
            a = pv[(2 * half) * blk:(2 * half + 1) * blk]
            b = pv[(2 * half + 1) * blk:(2 * half + 2) * blk]
            o_ref[t["rows"], (2 * t["g"] + half) * LANES:(2 * t["g"] + half + 1) * LANES] = (
                jnp.where(low, a, b).astype(BF16))


def swa_attention(q, k2, v2, kx2, vx2, sink, *, batch, seq, ctx_len):
    nb = seq // SWA_BLOCK
    ns = nb // SWA_SUB
    kvw = 2 * SWA_KV_DIM
    qspec = pl.BlockSpec((SWA_SUB * SWA_BLOCK, SWA_DIM), lambda b, i: (b * ns + i, 0))
    cur = pl.BlockSpec((SWA_SUB * SWA_BLOCK, kvw), lambda b, i: (b * ns + i, 0))
    halo = lambda off: pl.BlockSpec(
        (SWA_BLOCK, kvw), lambda b, i: (b * nb + jnp.clip(i * SWA_SUB + off, 0, nb - 1), 0))
    xspec = pl.BlockSpec((ctx_len, kvw), lambda b, i: (b, 0))
    return pl.pallas_call(
        functools.partial(_swa_attn_kernel, blocks_per_seq=nb),
        out_shape=jax.ShapeDtypeStruct(q.shape, BF16),
        grid=(batch, ns),
        in_specs=[qspec, halo(-1), cur, halo(SWA_SUB), halo(-1), cur, halo(SWA_SUB), xspec, xspec,
                  _resident(sink.shape)],
        out_specs=qspec,
        compiler_params=_cparams(2),
        name="swa_attention",
    )(q, k2, k2, k2, v2, v2, v2, kx2, vx2, sink)


CNV_RB = 32
CNV_PC = 256
CNV_STRIDE = 4


def _conv_module_kernel(xp_ref, x_ref, xn_ref, mod_ref, nw_ref, w1_ref, dw_ref, lnw_ref, lnb_ref, w2_ref,
                        o_ref, hin_ref, u_ref, c_ref, s_ref, *, tm, tiles_per_seq):
    sh, sc, gt = mod_ref[0:1, :], mod_ref[1:2, :], mod_ref[2:3, :]
    _fill_hin_ext(hin_ref, xp_ref, x_ref, xn_ref, nw_ref[...], sc, sh, tiles_per_seq, HALO, tm)
    d = D_MODEL
    n_slab = d // LANES
    for c in range(d // CNV_PC):
        cols = slice(c * CNV_PC, (c + 1) * CNV_PC)
        gcols = slice(d + c * CNV_PC, d + (c + 1) * CNV_PC)
        a1 = jnp.dot(hin_ref[...], w1_ref[:, cols], preferred_element_type=F32)
        a2 = jnp.dot(hin_ref[...], w1_ref[:, gcols], preferred_element_type=F32)
        u = a1 * jax.nn.sigmoid(a2)
        for t in range(CNV_PC // LANES):
            u_ref[c * (CNV_PC // LANES) + t] = u[:, t * LANES:(t + 1) * LANES]
    pad = CNV_WIDTH // 2
    rows = tm // CNV_STRIDE
    for j in range(n_slab):
        lanes = slice(j * LANES, (j + 1) * LANES)
        for v in range(CNV_STRIDE):
            acc = None
            for k in range(CNV_WIDTH):
                term = dw_ref[k:k + 1, lanes] * u_ref[j, pl.ds(HALO + v + k - pad, rows, stride=CNV_STRIDE), :]
                acc = term if acc is None else acc + term
            c_ref[j, pl.ds(v, rows, stride=CNV_STRIDE), :] = acc
    for rb in range(tm // CNV_RB):
        rsl = slice(rb * CNV_RB, (rb + 1) * CNV_RB)
        xs = [c_ref[j, rsl, :] for j in range(n_slab)]
        tot = xs[0]
        for xj in xs[1:]:
            tot = tot + xj
        mu = jnp.sum(tot, axis=-1, keepdims=True) * (1.0 / d)
        xcs = [xj - mu for xj in xs]
        sq = xcs[0] * xcs[0]
        for xc in xcs[1:]:
            sq = sq + xc * xc
        rstd = lax.rsqrt(jnp.sum(sq, axis=-1, keepdims=True) * (1.0 / d) + EPS)
        for j, xc in enumerate(xcs):
            lanes = slice(j * LANES, (j + 1) * LANES)
            y = xc * rstd * lnw_ref[:, lanes] + lnb_ref[:, lanes]
            s_ref[rsl, lanes] = jax.nn.silu(y).astype(BF16)
    o_ref[...] = x_ref[...] + gt * jnp.dot(s_ref[...], w2_ref[...], preferred_element_type=F32)


def conv_module(h, mod, nw, w1, dw, lnw, lnb, w2, *, seq, tm):
    n, d = h.shape
    n_tiles = n // tm
    tps = seq // tm
    prev, main, nxt = _halo_specs(tm, HALO, n_tiles, d)
    return pl.pallas_call(
        functools.partial(_conv_module_kernel, tm=tm, tiles_per_seq=tps),
        out_shape=jax.ShapeDtypeStruct((n, d), F32),
        grid=(n_tiles,),
        in_specs=[prev, main, nxt, _mod_spec(tps), _resident((1, d)), _resident(w1.shape),
                  _resident(dw.shape), _resident(lnw.shape), _resident(lnb.shape), _resident(w2.shape)],
        out_specs=pl.BlockSpec((tm, d), lambda i: (i, 0)),
        scratch_shapes=[pltpu.VMEM((tm + 2 * HALO, d), BF16),
                        pltpu.VMEM((d // LANES, tm + 2 * HALO, LANES), F32),
                        pltpu.VMEM((d // LANES, tm, LANES), F32),
                        pltpu.VMEM((tm, d), BF16)],
        compiler_params=_cparams(),
        name="conv_module",
    )(h, h, h, mod, nw, w1, dw, lnw, lnb, w2)


def _rope_tables(seq):
    n_freq = SWA_HEAD_DIM // 4
    inv_freq = ROPE_BASE ** (-jnp.arange(n_freq, dtype=F32) / n_freq)
    pos = jnp.arange(seq)
    rows = (pos // GRID_W).astype(F32)
    cols = (pos % GRID_W).astype(F32)
    ang_r = rows[:, None] * inv_freq
    ang_c = cols[:, None] * inv_freq
    ang = jnp.concatenate([ang_r, ang_r, ang_c, ang_c], axis=-1)
    sign = jnp.where((jnp.arange(SWA_HEAD_DIM) % (2 * ROPE_QUARTER)) < ROPE_QUARTER, -1.0, 1.0)
    cos = jnp.tile(jnp.cos(ang), (1, LANES // SWA_HEAD_DIM))
    sin = jnp.tile(jnp.sin(ang) * sign, (1, LANES // SWA_HEAD_DIM))
    return cos, sin


def _tile_rows(seq):
    return min(seq, 512)


def kernel(x, c, ctx, c_ctx, ada_w, ada_b, norm_mix_w, norm_ffn_w, dn_w_in, dn_conv, dn_a_log, dn_dt_bias,
           dn_norm_w, dn_w_out, pool_w_grp, pool_scale, swa_w_qkv, swa_q_norm, swa_k_norm, swa_sink,
           swa_w_out, cnv_w_pw1, cnv_dw, cnv_ln_w, cnv_ln_b, cnv_w_pw2, ffn_w_gate, ffn_w_up, ffn_conv,
           ffn_conv_b, ffn_w_down):
    bsz, seq, d = x.shape
    ctx_len = ctx.shape[1]
    depth = ada_w.shape[0]
    h = x.reshape(bsz * seq, d)
    hc = ctx.reshape(bsz * ctx_len, d)
    p = dict(norm_mix_w=norm_mix_w, norm_ffn_w=norm_ffn_w, dn_w_in=dn_w_in, dn_conv=dn_conv,
             dn_a_log=dn_a_log, dn_dt_bias=dn_dt_bias, dn_norm_w=dn_norm_w, dn_w_out=dn_w_out,
             pool_w_grp=pool_w_grp, pool_scale=pool_scale, swa_w_qkv=swa_w_qkv, swa_q_norm=swa_q_norm,
             swa_k_norm=swa_k_norm, swa_sink=swa_sink, swa_w_out=swa_w_out, cnv_w_pw1=cnv_w_pw1,
             cnv_dw=cnv_dw, cnv_ln_w=cnv_ln_w, cnv_ln_b=cnv_ln_b, cnv_w_pw2=cnv_w_pw2,
             ffn_w_gate=ffn_w_gate, ffn_w_up=ffn_w_up, ffn_conv=ffn_conv, ffn_conv_b=ffn_conv_b,
             ffn_w_down=ffn_w_down)
    dims = dict(bsz=bsz, seq=seq, ctx_len=ctx_len, depth=depth)
    mods = modulation(c, c_ctx, ada_w, ada_b)
    for i in range(depth):
        mod_l, mod_c = layer_modulation(mods, i, bsz)
        h, hc = mixer_layer(i, h, hc, mod_l, mod_c, p, dims)
        h, hc = ffn_layer(i, h, hc, mod_l, mod_c, p, dims)
    return h.reshape(bsz, seq, d)


def modulation(c, c_ctx, ada_w, ada_b):
    bsz, d = c.shape
    n_cond = -(-(bsz + 1) // SUBLANES) * SUBLANES
    cond = jnp.zeros((n_cond, d), F32).at[:bsz].set(c).at[bsz].set(c_ctx)
    return ada_modulation(cond, ada_w, ada_b)


def layer_modulation(mods, i, bsz):
    d = mods.shape[-1] // 6
    mod_l = mods[i, :bsz].reshape(bsz, 6, d)
    mod_c = jnp.broadcast_to(mods[i, bsz].reshape(1, 6, d), (bsz, 6, d))
    return mod_l, mod_c


def _keeps_ctx(i, depth):
    return any((j % N_MIXERS) in CTX_READING_MIXERS for j in range(i + 1, depth))


def ffn_layer(i, h, hc, mod_l, mod_c, p, dims):
    d = h.shape[-1]
    seq, ctx_len = dims["seq"], dims["ctx_len"]
    ffn_args = (p["norm_ffn_w"][i].reshape(1, d), p["ffn_w_gate"].astype(BF16),
                p["ffn_w_up"].astype(BF16), p["ffn_conv"][i], p["ffn_conv_b"][i].reshape(1, FFN_DIM),
                p["ffn_w_down"].astype(BF16))
    h = conv_ffn(h, mod_l, *ffn_args, layer=i, seq=seq, tm=_tile_rows(seq))
    if _keeps_ctx(i, dims["depth"]):
        hc = conv_ffn(hc, mod_c, *ffn_args, layer=i, seq=ctx_len, tm=_tile_rows(ctx_len))
    return h, hc


def mixer_layer(i, h, hc, mod_l, mod_c, p, dims):
    d = h.shape[-1]
    bsz, seq, ctx_len = dims["bsz"], dims["seq"], dims["ctx_len"]
    tm_c = _tile_rows(ctx_len)
    kind, slot = i % N_MIXERS, i // N_MIXERS
    keep_ctx = _keeps_ctx(i, dims["depth"])
    nw_mix = p["norm_mix_w"][i].reshape(1, d)
    lat = dict(seq=seq, tm=_tile_rows(seq))
    cx = dict(seq=ctx_len, tm=tm_c)

    if kind == MIX_DELTANET:
        w_in = p["dn_w_in"][slot]
        wqkv = w_in[:, :3 * DN_DIM].astype(BF16)
        wz = w_in[:, 3 * DN_DIM:4 * DN_DIM].astype(BF16)
        wgt = jnp.zeros((d, LANES), F32).at[:, :4 * DN_HEADS].set(w_in[:, 4 * DN_DIM:]).astype(BF16)
        aneg = jnp.zeros((1, LANES), F32).at[0, 2 * DN_HEADS:4 * DN_HEADS].set(
            -jnp.exp(p["dn_a_log"][slot].astype(F32)).reshape(-1))
        dtb = jnp.zeros((1, LANES), F32).at[0, 2 * DN_HEADS:4 * DN_HEADS].set(
            p["dn_dt_bias"][slot].astype(F32).reshape(-1))
        hnw = p["dn_norm_w"][slot].reshape(1, DN_HEAD_DIM)
        wout = p["dn_w_out"][slot].astype(BF16)
        proj = functools.partial(dn_project, nw=nw_mix, wqkv=wqkv, cw=p["dn_conv"][slot], wz=wz, wgt=wgt,
                                 aneg=aneg, dtb=dtb)
        qc, kc, vc, zc, bgc = proj(hc, mod_c, **cx)
        ql, kl, vl, zl, bgl = proj(h, mod_l, **lat)
        s0 = jnp.zeros((bsz, 2, DN_HEADS, DN_HEAD_DIM, DN_HEAD_DIM), F32)
        ocf, ocb, s_ctx = dn_scan(qc, kc, vc, bgc, s0, batch=bsz, seq=ctx_len)
        olf, olb, _ = dn_scan(ql, kl, vl, bgl, s_ctx, batch=bsz, seq=seq)
        h = dn_readout(olf, olb, zl, h, mod_l, hnw, wout, **lat)
        if keep_ctx:
            hc = dn_readout(ocf, ocb, zc, hc, mod_c, hnw, wout, **cx)
    elif kind == MIX_POOL:
        wg = p["pool_w_grp"][slot].astype(BF16)
        scale = p["pool_scale"][slot].reshape(1, d)
        h = pool_mixer(h, mod_l, nw_mix, wg, scale, **lat)
        if keep_ctx:
            hc = pool_mixer(hc, mod_c, nw_mix, wg, scale, **cx)
    elif kind == MIX_SWA:
        assert not keep_ctx
        wq = p["swa_w_qkv"][slot]
        dup = lambda w_: jnp.repeat(w_.reshape(d, SWA_KV_HEADS, 1, SWA_HEAD_DIM), 2, axis=2).reshape(
            d, 2 * SWA_KV_DIM)
        w_all = jnp.concatenate([wq[:, :SWA_DIM], dup(wq[:, SWA_DIM:SWA_DIM + SWA_KV_DIM]),
                                 dup(wq[:, SWA_DIM + SWA_KV_DIM:])], axis=1).astype(BF16)
        hnw = jnp.concatenate([jnp.tile(p["swa_q_norm"][slot], SWA_HEADS),
                               jnp.tile(p["swa_k_norm"][slot], 2 * SWA_KV_HEADS)]).reshape(1, SWA_NORM_COLS)
        li = jnp.arange(SWA_PC)
        seg = jnp.where((li[:, None] // SWA_HEAD_DIM) == (li[None, :] // SWA_HEAD_DIM),
                        1.0 / SWA_HEAD_DIM, 0.0).astype(BF16)
        cos, sin = _rope_tables(seq)
        sink = jnp.broadcast_to(p["swa_sink"][slot].astype(F32).reshape(SWA_HEADS, 1), (SWA_HEADS, LANES))
        ql, k2, v2 = swa_qkv(h, mod_l, nw_mix, w_all, hnw, seg, cos, sin, rope=True, **lat)
        _, kx2, vx2 = swa_qkv(hc, mod_c, nw_mix, w_all, hnw, seg, cos[:tm_c], sin[:tm_c], rope=False, **cx)
        o = swa_attention(ql, k2, v2, kx2, vx2, sink, batch=bsz, seq=seq, ctx_len=ctx_len)
        h = matmul_residual(o, h, mod_l, p["swa_w_out"][slot].astype(BF16), **lat)
    else:
        args = (nw_mix, p["cnv_w_pw1"][slot].astype(BF16), p["cnv_dw"][slot], p["cnv_ln_w"][slot].reshape(1, d),
                p["cnv_ln_b"][slot].reshape(1, d), p["cnv_w_pw2"][slot].astype(BF16))
        h = conv_module(h, mod_l, *args, **lat)
        if keep_ctx:
            hc = conv_module(hc, mod_c, *args, **cx)
    return h, hc
```

```python
import functools

import jax
import jax.numpy as jnp
from jax import lax
from jax.experimental import pallas as pl
from jax.experimental.pallas import tpu as pltpu

D_MODEL = 1024
EPS = 1e-6
NEG_INF = -1e30
GRID_W = 64
ROPE_BASE = 10000.0
N_MIXERS = 4
MIX_DELTANET, MIX_POOL, MIX_SWA, MIX_CONV = 0, 1, 2, 3
CTX_READING_MIXERS = (MIX_DELTANET, MIX_SWA)

DN_HEADS = 8
DN_HEAD_DIM = 128
DN_DIM = DN_HEADS * DN_HEAD_DIM
DN_CONV = 5
DN_CHUNK = 128
DN_INV_BASE = 2
DN_SUB = 4

POOL_WINDOWS = (2, 4, 8, 16)
POOL_GROUP = D_MODEL // len(POOL_WINDOWS)

SWA_HEADS = 16
SWA_KV_HEADS = 4
SWA_HEAD_DIM = 64
SWA_GROUP = SWA_HEADS // SWA_KV_HEADS
SWA_DIM = SWA_HEADS * SWA_HEAD_DIM
SWA_KV_DIM = SWA_KV_HEADS * SWA_HEAD_DIM
SWA_WINDOW = 128
SWA_BLOCK = 128
SWA_SUB = 4

CNV_WIDTH = 31
FFN_DIM = 2816
FFN_CONV = 3

LANES = 128
SUBLANES = 8
VMEM_LIMIT_BYTES = 56 * 1024 * 1024

F32 = jnp.float32
BF16 = jnp.bfloat16
HALO = 2 * SUBLANES
POOL_HALO = SUBLANES
POOL_STRIDE = 4


def _cparams(n_axes=1):
    return pltpu.CompilerParams(
        dimension_semantics=("arbitrary",) * n_axes,
        vmem_limit_bytes=VMEM_LIMIT_BYTES,
    )


def _resident(shape):
    nd = len(shape)
    return pl.BlockSpec(shape, lambda *_: (0,) * nd, pipeline_mode=pl.Buffered(1))


def _resident_layer(shape, layer):
    return pl.BlockSpec((None,) + tuple(shape[1:]), lambda *_: (layer,) + (0,) * (len(shape) - 1),
                        pipeline_mode=pl.Buffered(1))


def _halo_specs(tm, hb, n_tiles, width):
    r = tm // hb
    last = n_tiles * r - 1
    prev = pl.BlockSpec((hb, width), lambda i: (jnp.maximum(i * r - 1, 0), 0))
    main = pl.BlockSpec((tm, width), lambda i: (i, 0))
    nxt = pl.BlockSpec((hb, width), lambda i: (jnp.minimum((i + 1) * r, last), 0))
    return prev, main, nxt


def _mod_spec(tiles_per_seq):
    return pl.BlockSpec((None, 6, D_MODEL), lambda i: (i // tiles_per_seq, 0, 0))


def _rms_mod(x, nw, sc, sh):
    ms = jnp.mean(x * x, axis=-1, keepdims=True)
    return (x * lax.rsqrt(ms + EPS) * nw) * (1.0 + sc) + sh


def _fill_hin_ext(hin_ref, xp_ref, x_ref, xn_ref, nw, sc, sh, tiles_per_seq, hb, tm):
    i = pl.program_id(0)
    pos = lax.rem(i, tiles_per_seq)
    first = pos == 0
    last = pos == tiles_per_seq - 1
    hp = jnp.where(first, 0.0, _rms_mod(xp_ref[...], nw, sc, sh))
    hn = jnp.where(last, 0.0, _rms_mod(xn_ref[...], nw, sc, sh))
    hin_ref[0:hb, :] = hp.astype(hin_ref.dtype)
    hin_ref[hb:hb + tm, :] = _rms_mod(x_ref[...], nw, sc, sh).astype(hin_ref.dtype)
    hin_ref[hb + tm:hb + tm + hb, :] = hn.astype(hin_ref.dtype)


def _shift_rows(a, d, n_rows):
    if d == 0:
        return a
    return pltpu.roll(a, (-d) % n_rows, 0)


ADA_TN = 1536


def _ada_kernel(cond_ref, w_ref, b_ref, o_ref):
    s = jax.nn.silu(cond_ref[...])
    o_ref[...] = jnp.dot(s, w_ref[...], preferred_element_type=F32,
                         precision=lax.Precision.HIGHEST) + b_ref[...]


def ada_modulation(cond, ada_w, ada_b):
    depth, d, n = ada_w.shape
    rows = cond.shape[0]
    return pl.pallas_call(
        _ada_kernel,
        out_shape=jax.ShapeDtypeStruct((depth, rows, n), F32),
        grid=(depth, n // ADA_TN),
        in_specs=[
            pl.BlockSpec((rows, d), lambda l, j: (0, 0)),
            pl.BlockSpec((None, d, ADA_TN), lambda l, j: (l, 0, j)),
            pl.BlockSpec((None, 1, ADA_TN), lambda l, j: (l, 0, j)),
        ],
        out_specs=pl.BlockSpec((None, rows, ADA_TN), lambda l, j: (l, 0, j)),
        compiler_params=_cparams(2),
        name="ada_modulation",
    )(cond, ada_w, ada_b.reshape(depth, 1, n))


FFN_FC = 256
FFN_DOWN_GROUP = 6


def _ffn_kernel(xp_ref, x_ref, xn_ref, mod_ref, nw_ref, wg_ref, wu_ref, cw_ref, cb_ref, wd_ref,
                o_ref, hin_ref, act_ref, acc_ref, *, tm, tiles_per_seq):
    sh, sc, gt = mod_ref[3:4, :], mod_ref[4:5, :], mod_ref[5:6, :]
    _fill_hin_ext(hin_ref, xp_ref, x_ref, xn_ref, nw_ref[...], sc, sh, tiles_per_seq, HALO, tm)
    n_ext = tm + 2 * HALO
    n_chunks = FFN_DIM // FFN_FC
    group_start = 0
    for c in range(n_chunks):
        cols = slice(c * FFN_FC, (c + 1) * FFN_FC)
        g_ext = jnp.dot(hin_ref[...], wg_ref[:, cols], preferred_element_type=F32)
        u = jnp.dot(hin_ref[HALO:HALO + tm, :], wu_ref[:, cols], preferred_element_type=F32)
        g = cb_ref[:, cols]
        for k in range(FFN_CONV):
            g = g + cw_ref[k:k + 1, cols] * _shift_rows(g_ext, k - 1, n_ext)[HALO:HALO + tm, :]
        act_ref[:, cols] = (jax.nn.silu(g) * u).astype(BF16)
        if (c + 1) % FFN_DOWN_GROUP == 0 or c == n_chunks - 1:
            ks = slice(group_start * FFN_FC, (c + 1) * FFN_FC)
            y = jnp.dot(act_ref[:, ks], wd_ref[ks, :], preferred_element_type=F32)
            if group_start == 0:
                acc_ref[...] = y
            else:
                acc_ref[...] += y
            group_start = c + 1
    o_ref[...] = x_ref[...] + gt * acc_ref[...]


def conv_ffn(h, mod, nw, wg, wu, cw, cb, wd, *, layer, seq, tm):
    n, d = h.shape
    n_tiles = n // tm
    tps = seq // tm
    prev, main, nxt = _halo_specs(tm, HALO, n_tiles, d)
    return pl.pallas_call(
        functools.partial(_ffn_kernel, tm=tm, tiles_per_seq=tps),
        out_shape=jax.ShapeDtypeStruct((n, d), F32),
        grid=(n_tiles,),
        in_specs=[prev, main, nxt, _mod_spec(tps), _resident((1, d)),
                  _resident_layer(wg.shape, layer), _resident_layer(wu.shape, layer), _resident(cw.shape),
                  _resident(cb.shape), _resident_layer(wd.shape, layer)],
        out_specs=pl.BlockSpec((tm, d), lambda i: (i, 0)),
        scratch_shapes=[pltpu.VMEM((tm + 2 * HALO, d), BF16), pltpu.VMEM((tm, FFN_DIM), BF16),
                        pltpu.VMEM((tm, d), F32)],
        compiler_params=_cparams(),
        name="conv_ffn",
    )(h, h, h, mod, nw, wg, wu, cw, cb, wd)


def _mm_res_kernel(a_ref, res_ref, mod_ref, w_ref, o_ref):
    gt = mod_ref[2:3, :]
    o_ref[...] = res_ref[...] + gt * jnp.dot(a_ref[...], w_ref[...], preferred_element_type=F32)


def matmul_residual(a, res, mod, w, *, seq, tm):
    n, k = a.shape
    d = w.shape[1]
    tps = seq // tm
    return pl.pallas_call(
        _mm_res_kernel,
        out_shape=jax.ShapeDtypeStruct((n, d), F32),
        grid=(n // tm,),
        in_specs=[pl.BlockSpec((tm, k), lambda i: (i, 0)), pl.BlockSpec((tm, d), lambda i: (i, 0)),
                  _mod_spec(tps), _resident(w.shape)],
        out_specs=pl.BlockSpec((tm, d), lambda i: (i, 0)),
        compiler_params=_cparams(),
        name="matmul_residual",
    )(a, res, mod, w)


DN_PC = 256
DN_STRIDE = 4


def _dn_proj_kernel(xp_ref, x_ref, xn_ref, mod_ref, nw_ref, wqkv_ref, cw_ref, wz_ref, wgt_ref,
                    aneg_ref, dtb_ref, q_ref, k_ref, v_ref, z_ref, bg_ref, hin_ref, p_ref, a_ref,
                    *, tm, tiles_per_seq):
    sh, sc = mod_ref[0:1, :], mod_ref[1:2, :]
    _fill_hin_ext(hin_ref, xp_ref, x_ref, xn_ref, nw_ref[...], sc, sh, tiles_per_seq, HALO, tm)
    outs = (q_ref, k_ref, v_ref)
    per_out = DN_DIM // DN_PC
    heads_pc = DN_PC // DN_HEAD_DIM
    rows = tm // DN_STRIDE
    for c in range(3 * per_out):
        cols = slice(c * DN_PC, (c + 1) * DN_PC)
        p_ext = jnp.dot(hin_ref[...], wqkv_ref[:, cols], preferred_element_type=F32)
        which, cc = divmod(c, per_out)
        for hh in range(heads_pc):
            slot = (c % 2) * heads_pc + hh
            lanes = slice(c * DN_PC + hh * DN_HEAD_DIM, c * DN_PC + (hh + 1) * DN_HEAD_DIM)
            p_ref[slot] = p_ext[:, hh * DN_HEAD_DIM:(hh + 1) * DN_HEAD_DIM]
            for v in range(DN_STRIDE):
                acc = None
                for k in range(DN_CONV):
                    start = HALO + v + k - DN_CONV // 2
                    term = cw_ref[k:k + 1, lanes] * p_ref[slot, pl.ds(start, rows, stride=DN_STRIDE), :]
                    acc = term if acc is None else acc + term
                ah = jax.nn.silu(acc)
                if which < 2:
                    ah = ah * lax.rsqrt(jnp.sum(ah * ah, axis=-1, keepdims=True) + EPS)
                    if which == 0:
                        ah = ah * (DN_HEAD_DIM ** -0.5)
                a_ref[slot, pl.ds(v, rows, stride=DN_STRIDE), :] = ah
            lo = cc * DN_PC + hh * DN_HEAD_DIM
            outs[which][:, lo:lo + DN_HEAD_DIM] = a_ref[slot].astype(BF16)
    hm = hin_ref[HALO:HALO + tm, :]
    z_ref[...] = jnp.dot(hm, wz_ref[...], preferred_element_type=F32).astype(BF16)
    gates = jnp.dot(hm, wgt_ref[...], preferred_element_type=F32)
    lane = lax.broadcasted_iota(jnp.int32, gates.shape, 1)
    beta = jax.nn.sigmoid(gates)
    g = aneg_ref[...] * jax.nn.softplus(gates + dtb_ref[...])
    bg_ref[...] = jnp.where(lane < 2 * DN_HEADS, beta, g)


def dn_project(h, mod, nw, wqkv, cw, wz, wgt, aneg, dtb, *, seq, tm):
    n, d = h.shape
    n_tiles = n // tm
    tps = seq // tm
    prev, main, nxt = _halo_specs(tm, HALO, n_tiles, d)
    row = lambda w: pl.BlockSpec((tm, w), lambda i: (i, 0))
    return pl.pallas_call(
        functools.partial(_dn_proj_kernel, tm=tm, tiles_per_seq=tps),
        out_shape=[jax.ShapeDtypeStruct((n, DN_DIM), BF16)] * 4 + [jax.ShapeDtypeStruct((n, LANES), F32)],
        grid=(n_tiles,),
        in_specs=[prev, main, nxt, _mod_spec(tps), _resident((1, d)), _resident(wqkv.shape),
                  _resident(cw.shape), _resident(wz.shape), _resident(wgt.shape),
                  _resident(aneg.shape), _resident(dtb.shape)],
        out_specs=[row(DN_DIM)] * 4 + [row(LANES)],
        scratch_shapes=[pltpu.VMEM((tm + 2 * HALO, d), BF16),
                        pltpu.VMEM((2 * DN_PC // DN_HEAD_DIM, tm + 2 * HALO, DN_HEAD_DIM), F32),
                        pltpu.VMEM((2 * DN_PC // DN_HEAD_DIM, tm, DN_HEAD_DIM), F32)],
        compiler_params=_cparams(),
        name="dn_project",
    )(h, h, h, mod, nw, wqkv, cw, wz, wgt, aneg, dtb)


def _bdot(a, b):
    return jnp.dot(a.astype(BF16), b.astype(BF16), preferred_element_type=F32)


def _exact_mask_dot(mask, x):
    m = mask.astype(BF16)
    hi = x.astype(BF16)
    r1 = x - hi.astype(F32)
    mid = r1.astype(BF16)
    lo = (r1 - mid.astype(F32)).astype(BF16)
    return (jnp.dot(m, hi, preferred_element_type=F32) + jnp.dot(m, mid, preferred_element_type=F32)
            + jnp.dot(m, lo, preferred_element_type=F32))


def _unit_tri_inverses(lmats, lowers, eye, blk):
    c = lmats[0].shape[0]
    xs = [eye - jnp.where(blk[0], l, 0.0) for l in lmats]
    size = DN_INV_BASE
    for lvl in range(1, len(blk)):
        sel = jnp.logical_and(blk[lvl], jnp.logical_not(blk[lvl - 1]))
        if size < SUBLANES:
            ts = [_bdot(jnp.where(sel, l, 0.0), x) for l, x in zip(lmats, xs)]
            xs = [x - _bdot(x, t) for x, t in zip(xs, ts)]
            size *= 2
            continue
        groups = range(c // size)

        def active(a, lower):
            return jnp.concatenate([a[g * size:(g + 1) * size] for g in groups if (g % 2 == 1) == lower], axis=0)

        def expand(a, lower, rest):
            parts, k = [], 0
            for g in groups:
                if (g % 2 == 1) == lower:
                    parts.append(a[k * size:(k + 1) * size])
                    k += 1
                else:
                    parts.append(rest[g * size:(g + 1) * size])
            return jnp.concatenate(parts, axis=0)

        zero = jnp.zeros_like(eye)
        ts = [_bdot(active(jnp.where(sel, l, 0.0), lo), x) for l, lo, x in zip(lmats, lowers, xs)]
        us = [_bdot(active(x, lo), expand(t, lo, zero)) for x, lo, t in zip(xs, lowers, ts)]
        xs = [expand(active(x, lo) - u, lo, x) for x, lo, u in zip(xs, lowers, us)]
        size *= 2
    return xs


def _dn_scan_kernel(qf_ref, kf_ref, vf_ref, bgf_ref, qb_ref, kb_ref, vb_ref, bgb_ref, s0_ref,
                    of_ref, ob_ref, sfin_ref, s_ref, *, n_steps, n_sub):
    n = pl.program_id(1)
    c = DN_CHUNK

    @pl.when(n == 0)
    def _():
        s_ref[...] = s0_ref[...]

    ri = lax.broadcasted_iota(jnp.int32, (c, c), 0)
    ci = lax.broadcasted_iota(jnp.int32, (c, c), 1)
    eye = (ri == ci).astype(F32)
    blk = []
    bs = DN_INV_BASE
    while bs <= c:
        blk.append((ri // bs) == (ci // bs))
        bs *= 2

    dirs = ((qf_ref, kf_ref, vf_ref, bgf_ref, of_ref, ri >= ci, ri > ci),
            (qb_ref, kb_ref, vb_ref, bgb_ref, ob_ref, ri <= ci, ri < ci))
    inst = []
    for d, (q_ref, k_ref, v_ref, bg_ref, o_ref, incl, strict) in enumerate(dirs):
        for sub in range(n_sub):
            rows = slice(sub * c, (sub + 1) * c)
            bg = bg_ref[rows, :]
            is_g = lax.broadcasted_iota(jnp.int32, bg.shape, 1) >= 2 * DN_HEADS
            bgg = jnp.where(is_g, bg, 0.0)
            cum = _exact_mask_dot(incl, bgg)
            cum_t = cum.T
            tot = jnp.sum(bgg, axis=0, keepdims=True)
            egc = jnp.exp(cum)
            ekd = jnp.exp(tot - cum)
            etot = jnp.exp(tot)
            for h in range(DN_HEADS):
                lb = d * DN_HEADS + h
                lg = 2 * DN_HEADS + lb
                hs = slice(h * DN_HEAD_DIM, (h + 1) * DN_HEAD_DIM)
                inst.append(dict(
                    d=d, h=h, sub=sub, o=o_ref, rows=rows, hs=hs, incl=incl, strict=strict,
                    q=q_ref[rows, hs], k=k_ref[rows, hs], v=v_ref[rows, hs],
                    beta=bg[:, lb:lb + 1], gc=cum[:, lg:lg + 1], gc_row=cum_t[lg:lg + 1, :],
                    egc=egc[:, lg:lg + 1], ekd=ekd[:, lg:lg + 1], etot=etot[:, lg:lg + 1]))
    nt = (((1,), (1,)), ((), ()))
    tn = (((0,), (0,)), ((), ()))
    kq = [lax.dot_general(jnp.concatenate([t["k"], t["q"]], axis=0), t["k"], nt, preferred_element_type=F32)
          for t in inst]
    decays = [jnp.where(t["incl"], jnp.exp(jnp.where(t["incl"], t["gc"] - t["gc_row"], 0.0)), 0.0) for t in inst]
    lmats = [jnp.where(t["strict"], t["beta"] * r[:c] * dc, 0.0) for t, r, dc in zip(inst, kq, decays)]
    qks = [(r[c:] * dc).astype(BF16) for r, dc in zip(kq, decays)]
    tinvs = _unit_tri_inverses(lmats, [t["d"] == 0 for t in inst], eye, blk)
    rhss = [jnp.concatenate([t["v"].astype(F32) * t["beta"], t["k"].astype(F32) * (t["beta"] * t["egc"])], axis=1)
            for t in inst]
    uws = [_bdot(ti, r) for ti, r in zip(tinvs, rhss)]
    wq_lhs = [jnp.concatenate([uw[:, DN_HEAD_DIM:], t["q"].astype(F32) * t["egc"]], axis=0).astype(BF16)
              for t, uw in zip(inst, uws)]
    k_decs = [(t["k"].astype(F32) * t["ekd"]).astype(BF16) for t in inst]
    for pos in range(n_sub):
        cur = [i for i, t in enumerate(inst) if t["sub"] == (pos if t["d"] == 0 else n_sub - 1 - pos)]
        wqs = [jnp.dot(wq_lhs[i], s_ref[inst[i]["d"], inst[i]["h"]].astype(BF16), preferred_element_type=F32)
               for i in cur]
        v_news = [(uws[i][:, :DN_HEAD_DIM] - wq[:c]).astype(BF16) for i, wq in zip(cur, wqs)]
        outs = [wq[c:] + jnp.dot(qks[i], vn, preferred_element_type=F32) for i, wq, vn in zip(cur, wqs, v_news)]
        upds = [lax.dot_general(k_decs[i], vn, tn, preferred_element_type=F32) for i, vn in zip(cur, v_news)]
        for i, o, upd in zip(cur, outs, upds):
            t = inst[i]
            t["o"][t["rows"], t["hs"]] = o.astype(t["o"].dtype)
            s_ref[t["d"], t["h"]] = s_ref[t["d"], t["h"]] * t["etot"] + upd

    @pl.when(n == n_steps - 1)
    def _():
        sfin_ref[...] = s_ref[...]


def dn_scan(q, k, v, bg, s0, *, batch, seq):
    n_sub = min(DN_SUB, seq // DN_CHUNK)
    rows = n_sub * DN_CHUNK
    nc = seq // rows
    fwd = lambda w: pl.BlockSpec((rows, w), lambda b, n: (b * nc + n, 0))
    bwd = lambda w: pl.BlockSpec((rows, w), lambda b, n: (b * nc + nc - 1 - n, 0))
    st = pl.BlockSpec((None, 2, DN_HEADS, DN_HEAD_DIM, DN_HEAD_DIM), lambda b, n: (b, 0, 0, 0, 0))
    return pl.pallas_call(
        functools.partial(_dn_scan_kernel, n_steps=nc, n_sub=n_sub),
        out_shape=[jax.ShapeDtypeStruct(q.shape, BF16), jax.ShapeDtypeStruct(q.shape, BF16),
                   jax.ShapeDtypeStruct(s0.shape, F32)],
        grid=(batch, nc),
        in_specs=[fwd(DN_DIM), fwd(DN_DIM), fwd(DN_DIM), fwd(LANES),
                  bwd(DN_DIM), bwd(DN_DIM), bwd(DN_DIM), bwd(LANES), st],
        out_specs=[fwd(DN_DIM), bwd(DN_DIM), st],
        scratch_shapes=[pltpu.VMEM((2, DN_HEADS, DN_HEAD_DIM, DN_HEAD_DIM), F32)],
        compiler_params=_cparams(2),
        name="dn_scan",
    )(q, k, v, bg, q, k, v, bg, s0)


def _dn_readout_kernel(of_ref, ob_ref, z_ref, res_ref, mod_ref, nw_ref, w_ref, o_ref, y_ref):
    gt = mod_ref[2:3, :]
    for h in range(DN_HEADS):
        hs = slice(h * DN_HEAD_DIM, (h + 1) * DN_HEAD_DIM)
        o = of_ref[:, hs].astype(F32) + ob_ref[:, hs].astype(F32)
        ms = jnp.mean(o * o, axis=-1, keepdims=True)
        y = o * lax.rsqrt(ms + EPS) * nw_ref[...]
        y_ref[:, hs] = (y * jax.nn.silu(z_ref[:, hs].astype(F32))).astype(BF16)
    o_ref[...] = res_ref[...] + gt * jnp.dot(y_ref[...], w_ref[...], preferred_element_type=F32)


def dn_readout(o_f, o_b, z, res, mod, nw, w, *, seq, tm):
    n, d = res.shape
    tps = seq // tm
    row = lambda w_: pl.BlockSpec((tm, w_), lambda i: (i, 0))
    return pl.pallas_call(
        _dn_readout_kernel,
        out_shape=jax.ShapeDtypeStruct((n, d), F32),
        grid=(n // tm,),
        in_specs=[row(DN_DIM), row(DN_DIM), row(DN_DIM), row(d), _mod_spec(tps),
                  _resident(nw.shape), _resident(w.shape)],
        out_specs=row(d),
        scratch_shapes=[pltpu.VMEM((tm, DN_DIM), BF16)],
        compiler_params=_cparams(),
        name="dn_readout",
    )(o_f, o_b, z, res, mod, nw, w)


def _pool_kernel(xp_ref, x_ref, xn_ref, mod_ref, nw_ref, wg_ref, scale_ref, o_ref, hin_ref, d_ref, y_ref,
                 *, tm, tiles_per_seq, seq):
    sh, sc, gt = mod_ref[0:1, :], mod_ref[1:2, :], mod_ref[2:3, :]
    hb = POOL_HALO
    pos = lax.rem(pl.program_id(0), tiles_per_seq)
    hp = jnp.where(pos == 0, 0.0, _rms_mod(xp_ref[...], nw_ref[...], sc, sh))
    hn = jnp.where(pos == tiles_per_seq - 1, 0.0, _rms_mod(xn_ref[...], nw_ref[...], sc, sh))
    hm = _rms_mod(x_ref[...], nw_ref[...], sc, sh)
    n_slab = D_MODEL // LANES
    for j in range(n_slab):
        lanes = slice(j * LANES, (j + 1) * LANES)
        hin_ref[j, 0:hb, :] = hp[:, lanes]
        hin_ref[j, hb:hb + tm, :] = hm[:, lanes]
        hin_ref[j, hb + tm:hb + tm + hb, :] = hn[:, lanes]
    rows = tm // POOL_STRIDE
    slabs_per_group = POOL_GROUP // LANES
    ti = pos * tm + POOL_STRIDE * lax.broadcasted_iota(jnp.int32, (rows, 1), 0)
    for gi, win in enumerate(POOL_WINDOWS):
        half = win // 2
        for v in range(POOL_STRIDE):
            t = ti + v
            cnt = (jnp.minimum(t + (win - half), seq) - jnp.maximum(t - half, 0)).astype(F32)
            for j in range(gi * slabs_per_group, (gi + 1) * slabs_per_group):
                s = None
                for dlt in range(-half, win - half):
                    term = hin_ref[j, pl.ds(hb + v + dlt, rows, stride=POOL_STRIDE), :]
                    s = term if s is None else s + term
                d_ref[j, pl.ds(v, rows, stride=POOL_STRIDE), :] = (
                    s / cnt - hin_ref[j, pl.ds(hb + v, rows, stride=POOL_STRIDE), :])
        cols = slice(gi * POOL_GROUP, (gi + 1) * POOL_GROUP)
        dd = jnp.concatenate([d_ref[j] for j in range(gi * slabs_per_group, (gi + 1) * slabs_per_group)], axis=1)
        y_ref[:, cols] = jnp.dot(dd.astype(BF16), wg_ref[gi], preferred_element_type=F32)
    o_ref[...] = x_ref[...] + gt * (y_ref[...] * scale_ref[...])


def pool_mixer(h, mod, nw, wg, scale, *, seq, tm):
    n, d = h.shape
    n_tiles = n // tm
    tps = seq // tm
    prev, main, nxt = _halo_specs(tm, POOL_HALO, n_tiles, d)
    return pl.pallas_call(
        functools.partial(_pool_kernel, tm=tm, tiles_per_seq=tps, seq=seq),
        out_shape=jax.ShapeDtypeStruct((n, d), F32),
        grid=(n_tiles,),
        in_specs=[prev, main, nxt, _mod_spec(tps), _resident((1, d)), _resident(wg.shape),
                  _resident(scale.shape)],
        out_specs=pl.BlockSpec((tm, d), lambda i: (i, 0)),
        scratch_shapes=[pltpu.VMEM((d // LANES, tm + 2 * POOL_HALO, LANES), F32),
                        pltpu.VMEM((d // LANES, tm, LANES), F32), pltpu.VMEM((tm, d), F32)],
        compiler_params=_cparams(),
        name="pool_mixer",
    )(h, h, h, mod, nw, wg, scale)


SWA_QKV_COLS = SWA_DIM + 4 * SWA_KV_DIM
SWA_NORM_COLS = SWA_DIM + 2 * SWA_KV_DIM
SWA_PC = 256
ROPE_QUARTER = SWA_HEAD_DIM // 4


def _swa_qkv_kernel(x_ref, mod_ref, nw_ref, w_ref, hnw_ref, seg_ref, cos_ref, sin_ref,
                    q_ref, k_ref, v_ref, *, rope):
    sh, sc = mod_ref[0:1, :], mod_ref[1:2, :]
    hin = _rms_mod(x_ref[...], nw_ref[...], sc, sh).astype(BF16)
    pc = SWA_PC
    n_chunks = SWA_QKV_COLS // pc
    q_chunks = SWA_DIM // pc
    k_chunks = 2 * SWA_KV_DIM // pc
    for j in range(n_chunks):
        cols = slice(j * pc, (j + 1) * pc)
        p = jnp.dot(hin, w_ref[:, cols], preferred_element_type=F32)
        if j < q_chunks + k_chunks:
            ms = jnp.dot((p * p).astype(BF16), seg_ref[...], preferred_element_type=F32)
            p = p * lax.rsqrt(ms + EPS) * hnw_ref[:, cols]
            if rope:
                lane = lax.broadcasted_iota(jnp.int32, (p.shape[0], LANES), 1)
                up = lax.rem(lane, 2 * ROPE_QUARTER) < ROPE_QUARTER
                halves = []
                for t in range(pc // LANES):
                    ph = p[:, t * LANES:(t + 1) * LANES]
                    partner = jnp.where(up, pltpu.roll(ph, LANES - ROPE_QUARTER, 1),
                                        pltpu.roll(ph, ROPE_QUARTER, 1))
                    halves.append(ph * cos_ref[...] + partner * sin_ref[...])
                p = jnp.concatenate(halves, axis=1)
        if j < q_chunks:
            q_ref[:, cols] = (p * (SWA_HEAD_DIM ** -0.5)).astype(BF16)
        elif j < q_chunks + k_chunks:
            k_ref[:, (j - q_chunks) * pc:(j - q_chunks + 1) * pc] = p.astype(BF16)
        else:
            jj = j - q_chunks - k_chunks
            v_ref[:, jj * pc:(jj + 1) * pc] = p.astype(BF16)


def swa_qkv(h, mod, nw, w, hnw, seg, cos, sin, *, seq, tm, rope):
    n, d = h.shape
    tps = seq // tm
    row = lambda w_: pl.BlockSpec((tm, w_), lambda i: (i, 0))
    tab = pl.BlockSpec((tm, LANES), lambda i: (i % tps, 0))
    kvw = 2 * SWA_KV_DIM
    return pl.pallas_call(
        functools.partial(_swa_qkv_kernel, rope=rope),
        out_shape=[jax.ShapeDtypeStruct((n, SWA_DIM), BF16), jax.ShapeDtypeStruct((n, kvw), BF16),
                   jax.ShapeDtypeStruct((n, kvw), BF16)],
        grid=(n // tm,),
        in_specs=[row(d), _mod_spec(tps), _resident((1, d)), _resident(w.shape), _resident(hnw.shape),
                  _resident(seg.shape), tab, tab],
        out_specs=[row(SWA_DIM), row(kvw), row(kvw)],
        compiler_params=_cparams(),
        name="swa_qkv",
    )(h, mod, nw, w, hnw, seg, cos, sin)


def _swa_attn_kernel(q_ref, kp_ref, kc_ref, kn_ref, vp_ref, vc_ref, vn_ref, kx_ref, vx_ref, sink_ref,
                     o_ref, *, blocks_per_seq):
    i = pl.program_id(1)
    blk = SWA_BLOCK
    ri = lax.broadcasted_iota(jnp.int32, (SWA_GROUP * blk, blk), 0) % blk
    ci = lax.broadcasted_iota(jnp.int32, (SWA_GROUP * blk, blk), 1)
    lane = lax.broadcasted_iota(jnp.int32, (blk, LANES), 1)
    low = lane < SWA_HEAD_DIM
    nt = (((1,), (1,)), ((), ()))
    lane_tiles = lambda a: [a[:, t * LANES:(t + 1) * LANES] for t in range(a.shape[1] // LANES)]
    inst = []
    for sub in range(SWA_SUB):
        rows = slice(sub * blk, (sub + 1) * blk)
        prev = (kp_ref, vp_ref, slice(0, blk)) if sub == 0 else (kc_ref, vc_ref, slice((sub - 1) * blk, sub * blk))
        nxt = ((kn_ref, vn_ref, slice(0, blk)) if sub == SWA_SUB - 1
               else (kc_ref, vc_ref, slice((sub + 1) * blk, (sub + 2) * blk)))
        gblock = i * SWA_SUB + sub
        ok_prev = jnp.logical_and(ci >= ri, gblock > 0)
        ok_next = jnp.logical_and(ci <= ri, gblock < blocks_per_seq - 1)
        for g in range(SWA_KV_HEADS):
            inst.append(dict(g=g, rows=rows, gs=slice(g * LANES, (g + 1) * LANES), prev=prev, nxt=nxt,
                             ok_prev=ok_prev, ok_next=ok_next))

    def stacked_queries(t):
        qs = []
        for half in range(2):
            qt = q_ref[t["rows"], (2 * t["g"] + half) * LANES:(2 * t["g"] + half + 1) * LANES]
            qs.append(jnp.where(low, qt, jnp.zeros_like(qt)))
            qs.append(jnp.where(low, jnp.zeros_like(qt), qt))
        return jnp.concatenate(qs, axis=0)

    def sink_column(g):
        return jnp.concatenate(
            [jnp.broadcast_to(sink_ref[SWA_GROUP * g + hh:SWA_GROUP * g + hh + 1, 0:1], (blk, 1))
             for hh in range(SWA_GROUP)], axis=0)

    def window(t, which, cur_ref, ctx_ref):
        p_ref, n_ref = t["prev"][which], t["nxt"][which]
        return jnp.concatenate([p_ref[t["prev"][2], t["gs"]], cur_ref[t["rows"], t["gs"]],
                                n_ref[t["nxt"][2], t["gs"]], ctx_ref[:, t["gs"]]], axis=0)

    qst = [stacked_queries(t) for t in inst]
    sinks = [sink_column(t["g"]) for t in inst]
    scores = [lax.dot_general(q, window(t, 0, kc_ref, kx_ref), nt, preferred_element_type=F32)
              for q, t in zip(qst, inst)]
    pieces = []
    for s, t in zip(scores, inst):
        ts = lane_tiles(s)
        pieces.append([jnp.where(t["ok_prev"], ts[0], NEG_INF), ts[1], jnp.where(t["ok_next"], ts[2], NEG_INF)]
                      + ts[3:])
    ms = [jnp.maximum(jnp.max(functools.reduce(jnp.maximum, ps), axis=-1, keepdims=True), sk)
          for ps, sk in zip(pieces, sinks)]
    es = [[jnp.exp(p - m) for p in ps] for ps, m in zip(pieces, ms)]
    invs = [1.0 / (jnp.sum(functools.reduce(jnp.add, e), axis=-1, keepdims=True) + jnp.exp(sk - m))
            for e, sk, m in zip(es, sinks, ms)]
    pvs = [jnp.dot(jnp.concatenate(e, axis=1).astype(BF16), window(t, 1, vc_ref, vx_ref),
                   preferred_element_type=F32) * inv
           for e, t, inv in zip(es, inst, invs)]
    for t, pv in zip(inst, pvs):
        for half in range(2):
            a = pv[(2 * half) * blk:(2 * half + 1) * blk]
            b = pv[(2 * half + 1) * blk:(2 * half + 2) * blk]
            o_ref[t["rows"], (2 * t["g"] + half) * LANES:(2 * t["g"] + half + 1) * LANES] = (
                jnp.where(low, a, b).astype(BF16))


def swa_attention(q, k2, v2, kx2, vx2, sink, *, batch, seq, ctx_len):
    nb = seq // SWA_BLOCK
    ns = nb // SWA_SUB
    kvw = 2 * SWA_KV_DIM
    qspec = pl.BlockSpec((SWA_SUB * SWA_BLOCK, SWA_DIM), lambda b, i: (b * ns + i, 0))
    cur = pl.BlockSpec((SWA_SUB * SWA_BLOCK, kvw), lambda b, i: (b * ns + i, 0))
    halo = lambda off: pl.BlockSpec(
        (SWA_BLOCK, kvw), lambda b, i: (b * nb + jnp.clip(i * SWA_SUB + off, 0, nb - 1), 0))
    xspec = pl.BlockSpec((ctx_len, kvw), lambda b, i: (b, 0))
    return pl.pallas_call(
        functools.partial(_swa_attn_kernel, blocks_per_seq=nb),
        out_shape=jax.ShapeDtypeStruct(q.shape, BF16),
        grid=(batch, ns),
        in_specs=[qspec, halo(-1), cur, halo(SWA_SUB), halo(-1), cur, halo(SWA_SUB), xspec, xspec,
                  _resident(sink.shape)],
        out_specs=qspec,
        compiler_params=_cparams(2),
        name="swa_attention",
    )(q, k2, k2, k2, v2, v2, v2, kx2, vx2, sink)


CNV_RB = 32
CNV_PC = 256
CNV_STRIDE = 4


def _conv_module_kernel(xp_ref, x_ref, xn_ref, mod_ref, nw_ref, w1_ref, dw_ref, lnw_ref, lnb_ref, w2_ref,
                        o_ref, hin_ref, u_ref, c_ref, s_ref, *, tm, tiles_per_seq):
    sh, sc, gt = mod_ref[0:1, :], mod_ref[1:2, :], mod_ref[2:3, :]
    _fill_hin_ext(hin_ref, xp_ref, x_ref, xn_ref, nw_ref[...], sc, sh, tiles_per_seq, HALO, tm)
    d = D_MODEL
    n_slab = d // LANES
    for c in range(d // CNV_PC):
        cols = slice(c * CNV_PC, (c + 1) * CNV_PC)
        gcols = slice(d + c * CNV_PC, d + (c + 1) * CNV_PC)
        a1 = jnp.dot(hin_ref[...], w1_ref[:, cols], preferred_element_type=F32)
        a2 = jnp.dot(hin_ref[...], w1_ref[:, gcols], preferred_element_type=F32)
        u = a1 * jax.nn.sigmoid(a2)
        for t in range(CNV_PC // LANES):
            u_ref[c * (CNV_PC // LANES) + t] = u[:, t * LANES:(t + 1) * LANES]
    pad = CNV_WIDTH // 2
    rows = tm // CNV_STRIDE
    for j in range(n_slab):
        lanes = slice(j * LANES, (j + 1) * LANES)
        for v in range(CNV_STRIDE):
            acc = None
            for k in range(CNV_WIDTH):
                term = dw_ref[k:k + 1, lanes] * u_ref[j, pl.ds(HALO + v + k - pad, rows, stride=CNV_STRIDE), :]
                acc = term if acc is None else acc + term
            c_ref[j, pl.ds(v, rows, stride=CNV_STRIDE), :] = acc
    for rb in range(tm // CNV_RB):
        rsl = slice(rb * CNV_RB, (rb + 1) * CNV_RB)
        xs = [c_ref[j, rsl, :] for j in range(n_slab)]
        tot = xs[0]
        for xj in xs[1:]:
            tot = tot + xj
        mu = jnp.sum(tot, axis=-1, keepdims=True) * (1.0 / d)
        xcs = [xj - mu for xj in xs]
        sq = xcs[0] * xcs[0]
        for xc in xcs[1:]:
            sq = sq + xc * xc
        rstd = lax.rsqrt(jnp.sum(sq, axis=-1, keepdims=True) * (1.0 / d) + EPS)
        for j, xc in enumerate(xcs):
            lanes = slice(j * LANES, (j + 1) * LANES)
            y = xc * rstd * lnw_ref[:, lanes] + lnb_ref[:, lanes]
            s_ref[rsl, lanes] = jax.nn.silu(y).astype(BF16)
    o_ref[...] = x_ref[...] + gt * jnp.dot(s_ref[...], w2_ref[...], preferred_element_type=F32)


def conv_module(h, mod, nw, w1, dw, lnw, lnb, w2, *, seq, tm):
    n, d = h.shape
    n_tiles = n // tm
    tps = seq // tm
    prev, main, nxt = _halo_specs(tm, HALO, n_tiles, d)
    return pl.pallas_call(
        functools.partial(_conv_module_kernel, tm=tm, tiles_per_seq=tps),
        out_shape=jax.ShapeDtypeStruct((n, d), F32),
        grid=(n_tiles,),
        in_specs=[prev, main, nxt, _mod_spec(tps), _resident((1, d)), _resident(w1.shape),
                  _resident(dw.shape), _resident(lnw.shape), _resident(lnb.shape), _resident(w2.shape)],
        out_specs=pl.BlockSpec((tm, d), lambda i: (i, 0)),
        scratch_shapes=[pltpu.VMEM((tm + 2 * HALO, d), BF16),
                        pltpu.VMEM((d // LANES, tm + 2 * HALO, LANES), F32),
                        pltpu.VMEM((d // LANES, tm, LANES), F32),
                        pltpu.VMEM((tm, d), BF16)],
        compiler_params=_cparams(),
        name="conv_module",
    )(h, h, h, mod, nw, w1, dw, lnw, lnb, w2)


def _rope_tables(seq):
    n_freq = SWA_HEAD_DIM // 4
    inv_freq = ROPE_BASE ** (-jnp.arange(n_freq, dtype=F32) / n_freq)
    pos = jnp.arange(seq)
    rows = (pos // GRID_W).astype(F32)
    cols = (pos % GRID_W).astype(F32)
    ang_r = rows[:, None] * inv_freq
    ang_c = cols[:, None] * inv_freq
    ang = jnp.concatenate([ang_r, ang_r, ang_c, ang_c], axis=-1)
    sign = jnp.where((jnp.arange(SWA_HEAD_DIM) % (2 * ROPE_QUARTER)) < ROPE_QUARTER, -1.0, 1.0)
    cos = jnp.tile(jnp.cos(ang), (1, LANES // SWA_HEAD_DIM))
    sin = jnp.tile(jnp.sin(ang) * sign, (1, LANES // SWA_HEAD_DIM))
    return cos, sin


TILE_ROWS = 1024


def _tile_rows(seq):
    return min(seq, TILE_ROWS)


def kernel(x, c, ctx, c_ctx, ada_w, ada_b, norm_mix_w, norm_ffn_w, dn_w_in, dn_conv, dn_a_log, dn_dt_bias,
           dn_norm_w, dn_w_out, pool_w_grp, pool_scale, swa_w_qkv, swa_q_norm, swa_k_norm, swa_sink,
           swa_w_out, cnv_w_pw1, cnv_dw, cnv_ln_w, cnv_ln_b, cnv_w_pw2, ffn_w_gate, ffn_w_up, ffn_conv,
           ffn_conv_b, ffn_w_down):
    bsz, seq, d = x.shape
    ctx_len = ctx.shape[1]
    depth = ada_w.shape[0]
    h = x.reshape(bsz * seq, d)
    hc = ctx.reshape(bsz * ctx_len, d)
    p = dict(norm_mix_w=norm_mix_w, norm_ffn_w=norm_ffn_w, dn_w_in=dn_w_in, dn_conv=dn_conv,
             dn_a_log=dn_a_log, dn_dt_bias=dn_dt_bias, dn_norm_w=dn_norm_w, dn_w_out=dn_w_out,
             pool_w_grp=pool_w_grp, pool_scale=pool_scale, swa_w_qkv=swa_w_qkv, swa_q_norm=swa_q_norm,
             swa_k_norm=swa_k_norm, swa_sink=swa_sink, swa_w_out=swa_w_out, cnv_w_pw1=cnv_w_pw1,
             cnv_dw=cnv_dw, cnv_ln_w=cnv_ln_w, cnv_ln_b=cnv_ln_b, cnv_w_pw2=cnv_w_pw2,
             ffn_w_gate=ffn_w_gate, ffn_w_up=ffn_w_up, ffn_conv=ffn_conv, ffn_conv_b=ffn_conv_b,
             ffn_w_down=ffn_w_down)
    dims = dict(bsz=bsz, seq=seq, ctx_len=ctx_len, depth=depth)
    mods = modulation(c, c_ctx, ada_w, ada_b)
    for i in range(depth):
        mod_l, mod_c = layer_modulation(mods, i, bsz)
        h, hc = mixer_layer(i, h, hc, mod_l, mod_c, p, dims)
        h, hc = ffn_layer(i, h, hc, mod_l, mod_c, p, dims)
    return h.reshape(bsz, seq, d)


def modulation(c, c_ctx, ada_w, ada_b):
    bsz, d = c.shape
    n_cond = -(-(bsz + 1) // SUBLANES) * SUBLANES
    cond = jnp.zeros((n_cond, d), F32).at[:bsz].set(c).at[bsz].set(c_ctx)
    return ada_modulation(cond, ada_w, ada_b)


def layer_modulation(mods, i, bsz):
    d = mods.shape[-1] // 6
    mod_l = mods[i, :bsz].reshape(bsz, 6, d)
    mod_c = jnp.broadcast_to(mods[i, bsz].reshape(1, 6, d), (bsz, 6, d))
    return mod_l, mod_c


def _keeps_ctx(i, depth):
    return any((j % N_MIXERS) in CTX_READING_MIXERS for j in range(i + 1, depth))


def ffn_layer(i, h, hc, mod_l, mod_c, p, dims):
    d = h.shape[-1]
    seq, ctx_len = dims["seq"], dims["ctx_len"]
    ffn_args = (p["norm_ffn_w"][i].reshape(1, d), p["ffn_w_gate"].astype(BF16),
                p["ffn_w_up"].astype(BF16), p["ffn_conv"][i], p["ffn_conv_b"][i].reshape(1, FFN_DIM),
                p["ffn_w_down"].astype(BF16))
    h = conv_ffn(h, mod_l, *ffn_args, layer=i, seq=seq, tm=_tile_rows(seq))
    if _keeps_ctx(i, dims["depth"]):
        hc = conv_ffn(hc, mod_c, *ffn_args, layer=i, seq=ctx_len, tm=_tile_rows(ctx_len))
    return h, hc


def mixer_layer(i, h, hc, mod_l, mod_c, p, dims):
    d = h.shape[-1]
    bsz, seq, ctx_len = dims["bsz"], dims["seq"], dims["ctx_len"]
    tm_c = _tile_rows(ctx_len)
    kind, slot = i % N_MIXERS, i // N_MIXERS
    keep_ctx = _keeps_ctx(i, dims["depth"])
    nw_mix = p["norm_mix_w"][i].reshape(1, d)
    lat = dict(seq=seq, tm=_tile_rows(seq))
    cx = dict(seq=ctx_len, tm=tm_c)

    if kind == MIX_DELTANET:
        w_in = p["dn_w_in"][slot]
        wqkv = w_in[:, :3 * DN_DIM].astype(BF16)
        wz = w_in[:, 3 * DN_DIM:4 * DN_DIM].astype(BF16)
        wgt = jnp.zeros((d, LANES), F32).at[:, :4 * DN_HEADS].set(w_in[:, 4 * DN_DIM:]).astype(BF16)
        aneg = jnp.zeros((1, LANES), F32).at[0, 2 * DN_HEADS:4 * DN_HEADS].set(
            -jnp.exp(p["dn_a_log"][slot].astype(F32)).reshape(-1))
        dtb = jnp.zeros((1, LANES), F32).at[0, 2 * DN_HEADS:4 * DN_HEADS].set(
            p["dn_dt_bias"][slot].astype(F32).reshape(-1))
        hnw = p["dn_norm_w"][slot].reshape(1, DN_HEAD_DIM)
        wout = p["dn_w_out"][slot].astype(BF16)
        proj = functools.partial(dn_project, nw=nw_mix, wqkv=wqkv, cw=p["dn_conv"][slot], wz=wz, wgt=wgt,
                                 aneg=aneg, dtb=dtb)
        qc, kc, vc, zc, bgc = proj(hc, mod_c, **cx)
        ql, kl, vl, zl, bgl = proj(h, mod_l, **lat)
        s0 = jnp.zeros((bsz, 2, DN_HEADS, DN_HEAD_DIM, DN_HEAD_DIM), F32)
        ocf, ocb, s_ctx = dn_scan(qc, kc, vc, bgc, s0, batch=bsz, seq=ctx_len)
        olf, olb, _ = dn_scan(ql, kl, vl, bgl, s_ctx, batch=bsz, seq=seq)
        h = dn_readout(olf, olb, zl, h, mod_l, hnw, wout, **lat)
        if keep_ctx:
            hc = dn_readout(ocf, ocb, zc, hc, mod_c, hnw, wout, **cx)
    elif kind == MIX_POOL:
        wg = p["pool_w_grp"][slot].astype(BF16)
        scale = p["pool_scale"][slot].reshape(1, d)
        h = pool_mixer(h, mod_l, nw_mix, wg, scale, **lat)
        if keep_ctx:
            hc = pool_mixer(hc, mod_c, nw_mix, wg, scale, **cx)
    elif kind == MIX_SWA:
        assert not keep_ctx
        wq = p["swa_w_qkv"][slot]
        dup = lambda w_: jnp.repeat(w_.reshape(d, SWA_KV_HEADS, 1, SWA_HEAD_DIM), 2, axis=2).reshape(
            d, 2 * SWA_KV_DIM)
        w_all = jnp.concatenate([wq[:, :SWA_DIM], dup(wq[:, SWA_DIM:SWA_DIM + SWA_KV_DIM]),
                                 dup(wq[:, SWA_DIM + SWA_KV_DIM:])], axis=1).astype(BF16)
        hnw = jnp.concatenate([jnp.tile(p["swa_q_norm"][slot], SWA_HEADS),
                               jnp.tile(p["swa_k_norm"][slot], 2 * SWA_KV_HEADS)]).reshape(1, SWA_NORM_COLS)
        li = jnp.arange(SWA_PC)
        seg = jnp.where((li[:, None] // SWA_HEAD_DIM) == (li[None, :] // SWA_HEAD_DIM),
                        1.0 / SWA_HEAD_DIM, 0.0).astype(BF16)
        cos, sin = _rope_tables(seq)
        sink = jnp.broadcast_to(p["swa_sink"][slot].astype(F32).reshape(SWA_HEADS, 1), (SWA_HEADS, LANES))
        ql, k2, v2 = swa_qkv(h, mod_l, nw_mix, w_all, hnw, seg, cos, sin, rope=True, **lat)
        _, kx2, vx2 = swa_qkv(hc, mod_c, nw_mix, w_all, hnw, seg, cos[:tm_c], sin[:tm_c], rope=False, **cx)
        o = swa_attention(ql, k2, v2, kx2, vx2, sink, batch=bsz, seq=seq, ctx_len=ctx_len)
        h = matmul_residual(o, h, mod_l, p["swa_w_out"][slot].astype(BF16), **lat)
    else:
        args = (nw_mix, p["cnv_w_pw1"][slot].astype(BF16), p["cnv_dw"][slot], p["cnv_ln_w"][slot].reshape(1, d),
                p["cnv_ln_b"][slot].reshape(1, d), p["cnv_w_pw2"][slot].astype(BF16))
        h = conv_module(h, mod_l, *args, **lat)
        if keep_ctx:
            hc = conv_module(hc, mod_c, *args, **cx)
    return h, hc
```

```python
import functools

import jax
import jax.numpy as jnp
from jax import lax
from jax.experimental import pallas as pl
from jax.experimental.pallas import tpu as pltpu

D_MODEL = 1024
EPS = 1e-6
NEG_INF = -1e30
GRID_W = 64
ROPE_BASE = 10000.0
N_MIXERS = 4
MIX_DELTANET, MIX_POOL, MIX_SWA, MIX_CONV = 0, 1, 2, 3
CTX_READING_MIXERS = (MIX_DELTANET, MIX_SWA)

DN_HEADS = 8
DN_HEAD_DIM = 128
DN_DIM = DN_HEADS * DN_HEAD_DIM
DN_CONV = 5
DN_CHUNK = 128
DN_INV_BASE = 2
DN_SUB = 4

POOL_WINDOWS = (2, 4, 8, 16)
POOL_GROUP = D_MODEL // len(POOL_WINDOWS)

SWA_HEADS = 16
SWA_KV_HEADS = 4
SWA_HEAD_DIM = 64
SWA_GROUP = SWA_HEADS // SWA_KV_HEADS
SWA_DIM = SWA_HEADS * SWA_HEAD_DIM
SWA_KV_DIM = SWA_KV_HEADS * SWA_HEAD_DIM
SWA_WINDOW = 128
SWA_BLOCK = 128
SWA_SUB = 4

CNV_WIDTH = 31
FFN_DIM = 2816
FFN_CONV = 3

LANES = 128
SUBLANES = 8
VMEM_LIMIT_BYTES = 56 * 1024 * 1024

F32 = jnp.float32
BF16 = jnp.bfloat16
HALO = 2 * SUBLANES
POOL_HALO = SUBLANES
POOL_STRIDE = 4


def _cparams(n_axes=1):
    return pltpu.CompilerParams(
        dimension_semantics=("arbitrary",) * n_axes,
        vmem_limit_bytes=VMEM_LIMIT_BYTES,
    )


def _resident(shape):
    nd = len(shape)
    return pl.BlockSpec(shape, lambda *_: (0,) * nd, pipeline_mode=pl.Buffered(1))


def _resident_layer(shape, layer):
    return pl.BlockSpec((None,) + tuple(shape[1:]), lambda *_: (layer,) + (0,) * (len(shape) - 1),
                        pipeline_mode=pl.Buffered(1))


def _halo_specs(tm, hb, n_tiles, width):
    r = tm // hb
    last = n_tiles * r - 1
    prev = pl.BlockSpec((hb, width), lambda i: (jnp.maximum(i * r - 1, 0), 0))
    main = pl.BlockSpec((tm, width), lambda i: (i, 0))
    nxt = pl.BlockSpec((hb, width), lambda i: (jnp.minimum((i + 1) * r, last), 0))
    return prev, main, nxt


def _mod_spec(tiles_per_seq):
    return pl.BlockSpec((None, 6, D_MODEL), lambda i: (i // tiles_per_seq, 0, 0))


def _rms_mod(x, nw, sc, sh):
    ms = jnp.mean(x * x, axis=-1, keepdims=True)
    return (x * lax.rsqrt(ms + EPS) * nw) * (1.0 + sc) + sh


def _fill_hin_ext(hin_ref, xp_ref, x_ref, xn_ref, nw, sc, sh, tiles_per_seq, hb, tm):
    i = pl.program_id(0)
    pos = lax.rem(i, tiles_per_seq)
    first = pos == 0
    last = pos == tiles_per_seq - 1
    hp = jnp.where(first, 0.0, _rms_mod(xp_ref[...], nw, sc, sh))
    hn = jnp.where(last, 0.0, _rms_mod(xn_ref[...], nw, sc, sh))
    hin_ref[0:hb, :] = hp.astype(hin_ref.dtype)
    hin_ref[hb:hb + tm, :] = _rms_mod(x_ref[...], nw, sc, sh).astype(hin_ref.dtype)
    hin_ref[hb + tm:hb + tm + hb, :] = hn.astype(hin_ref.dtype)


def _shift_rows(a, d, n_rows):
    if d == 0:
        return a
    return pltpu.roll(a, (-d) % n_rows, 0)


ADA_TN = 1536


def _ada_kernel(cond_ref, w_ref, b_ref, o_ref):
    s = jax.nn.silu(cond_ref[...])
    o_ref[...] = jnp.dot(s, w_ref[...], preferred_element_type=F32,
                         precision=lax.Precision.HIGHEST) + b_ref[...]


def ada_modulation(cond, ada_w, ada_b):
    depth, d, n = ada_w.shape
    rows = cond.shape[0]
    return pl.pallas_call(
        _ada_kernel,
        out_shape=jax.ShapeDtypeStruct((depth, rows, n), F32),
        grid=(depth, n // ADA_TN),
        in_specs=[
            pl.BlockSpec((rows, d), lambda l, j: (0, 0)),
            pl.BlockSpec((None, d, ADA_TN), lambda l, j: (l, 0, j)),
            pl.BlockSpec((None, 1, ADA_TN), lambda l, j: (l, 0, j)),
        ],
        out_specs=pl.BlockSpec((None, rows, ADA_TN), lambda l, j: (l, 0, j)),
        compiler_params=_cparams(2),
        name="ada_modulation",
    )(cond, ada_w, ada_b.reshape(depth, 1, n))


FFN_FC = 256
FFN_DOWN_GROUP = 6


def _ffn_kernel(xp_ref, x_ref, xn_ref, mod_ref, nw_ref, wg_ref, wu_ref, cw_ref, cb_ref, wd_ref,
                o_ref, hin_ref, act_ref, acc_ref, *, tm, tiles_per_seq):
    sh, sc, gt = mod_ref[3:4, :], mod_ref[4:5, :], mod_ref[5:6, :]
    _fill_hin_ext(hin_ref, xp_ref, x_ref, xn_ref, nw_ref[...], sc, sh, tiles_per_seq, HALO, tm)
    n_ext = tm + 2 * HALO
    n_chunks = FFN_DIM // FFN_FC
    group_start = 0
    for c in range(n_chunks):
        cols = slice(c * FFN_FC, (c + 1) * FFN_FC)
        g_ext = jnp.dot(hin_ref[...], wg_ref[:, cols], preferred_element_type=F32)
        u = jnp.dot(hin_ref[HALO:HALO + tm, :], wu_ref[:, cols], preferred_element_type=F32)
        g = cb_ref[:, cols]
        for k in range(FFN_CONV):
            g = g + cw_ref[k:k + 1, cols] * _shift_rows(g_ext, k - 1, n_ext)[HALO:HALO + tm, :]
        act_ref[:, cols] = (jax.nn.silu(g) * u).astype(BF16)
        if (c + 1) % FFN_DOWN_GROUP == 0 or c == n_chunks - 1:
            ks = slice(group_start * FFN_FC, (c + 1) * FFN_FC)
            y = jnp.dot(act_ref[:, ks], wd_ref[ks, :], preferred_element_type=F32)
            if group_start == 0:
                acc_ref[...] = y
            else:
                acc_ref[...] += y
            group_start = c + 1
    o_ref[...] = x_ref[...] + gt * acc_ref[...]


def conv_ffn(h, mod, nw, wg, wu, cw, cb, wd, *, layer, seq, tm):
    n, d = h.shape
    n_tiles = n // tm
    tps = seq // tm
    prev, main, nxt = _halo_specs(tm, HALO, n_tiles, d)
    return pl.pallas_call(
        functools.partial(_ffn_kernel, tm=tm, tiles_per_seq=tps),
        out_shape=jax.ShapeDtypeStruct((n, d), F32),
        grid=(n_tiles,),
        in_specs=[prev, main, nxt, _mod_spec(tps), _resident((1, d)),
                  _resident_layer(wg.shape, layer), _resident_layer(wu.shape, layer), _resident(cw.shape),
                  _resident(cb.shape), _resident_layer(wd.shape, layer)],
        out_specs=pl.BlockSpec((tm, d), lambda i: (i, 0)),
        scratch_shapes=[pltpu.VMEM((tm + 2 * HALO, d), BF16), pltpu.VMEM((tm, FFN_DIM), BF16),
                        pltpu.VMEM((tm, d), F32)],
        compiler_params=_cparams(),
        name="conv_ffn",
    )(h, h, h, mod, nw, wg, wu, cw, cb, wd)


def _mm_res_kernel(a_ref, res_ref, mod_ref, w_ref, o_ref):
    gt = mod_ref[2:3, :]
    o_ref[...] = res_ref[...] + gt * jnp.dot(a_ref[...], w_ref[...], preferred_element_type=F32)


def matmul_residual(a, res, mod, w, *, seq, tm):
    n, k = a.shape
    d = w.shape[1]
    tps = seq // tm
    return pl.pallas_call(
        _mm_res_kernel,
        out_shape=jax.ShapeDtypeStruct((n, d), F32),
        grid=(n // tm,),
        in_specs=[pl.BlockSpec((tm, k), lambda i: (i, 0)), pl.BlockSpec((tm, d), lambda i: (i, 0)),
                  _mod_spec(tps), _resident(w.shape)],
        out_specs=pl.BlockSpec((tm, d), lambda i: (i, 0)),
        compiler_params=_cparams(),
        name="matmul_residual",
    )(a, res, mod, w)


DN_PC = 256
DN_STRIDE = 4


def _dn_proj_kernel(xp_ref, x_ref, xn_ref, mod_ref, nw_ref, wqkv_ref, cw_ref, wz_ref, wgt_ref,
                    aneg_ref, dtb_ref, q_ref, k_ref, v_ref, z_ref, bg_ref, hin_ref, p_ref, a_ref,
                    *, tm, tiles_per_seq):
    sh, sc = mod_ref[0:1, :], mod_ref[1:2, :]
    _fill_hin_ext(hin_ref, xp_ref, x_ref, xn_ref, nw_ref[...], sc, sh, tiles_per_seq, HALO, tm)
    outs = (q_ref, k_ref, v_ref)
    per_out = DN_DIM // DN_PC
    heads_pc = DN_PC // DN_HEAD_DIM
    rows = tm // DN_STRIDE
    for c in range(3 * per_out):
        cols = slice(c * DN_PC, (c + 1) * DN_PC)
        p_ext = jnp.dot(hin_ref[...], wqkv_ref[:, cols], preferred_element_type=F32)
        which, cc = divmod(c, per_out)
        for hh in range(heads_pc):
            slot = (c % 2) * heads_pc + hh
            lanes = slice(c * DN_PC + hh * DN_HEAD_DIM, c * DN_PC + (hh + 1) * DN_HEAD_DIM)
            p_ref[slot] = p_ext[:, hh * DN_HEAD_DIM:(hh + 1) * DN_HEAD_DIM]
            for v in range(DN_STRIDE):
                acc = None
                for k in range(DN_CONV):
                    start = HALO + v + k - DN_CONV // 2
                    term = cw_ref[k:k + 1, lanes] * p_ref[slot, pl.ds(start, rows, stride=DN_STRIDE), :]
                    acc = term if acc is None else acc + term
                ah = jax.nn.silu(acc)
                if which < 2:
                    ah = ah * lax.rsqrt(jnp.sum(ah * ah, axis=-1, keepdims=True) + EPS)
                    if which == 0:
                        ah = ah * (DN_HEAD_DIM ** -0.5)
                a_ref[slot, pl.ds(v, rows, stride=DN_STRIDE), :] = ah
            lo = cc * DN_PC + hh * DN_HEAD_DIM
            outs[which][:, lo:lo + DN_HEAD_DIM] = a_ref[slot].astype(BF16)
    hm = hin_ref[HALO:HALO + tm, :]
    z_ref[...] = jnp.dot(hm, wz_ref[...], preferred_element_type=F32).astype(BF16)
    gates = jnp.dot(hm, wgt_ref[...], preferred_element_type=F32)
    lane = lax.broadcasted_iota(jnp.int32, gates.shape, 1)
    beta = jax.nn.sigmoid(gates)
    g = aneg_ref[...] * jax.nn.softplus(gates + dtb_ref[...])
    bg_ref[...] = jnp.where(lane < 2 * DN_HEADS, beta, g)


def dn_project(h, mod, nw, wqkv, cw, wz, wgt, aneg, dtb, *, seq, tm):
    n, d = h.shape
    n_tiles = n // tm
    tps = seq // tm
    prev, main, nxt = _halo_specs(tm, HALO, n_tiles, d)
    row = lambda w: pl.BlockSpec((tm, w), lambda i: (i, 0))
    return pl.pallas_call(
        functools.partial(_dn_proj_kernel, tm=tm, tiles_per_seq=tps),
        out_shape=[jax.ShapeDtypeStruct((n, DN_DIM), BF16)] * 4 + [jax.ShapeDtypeStruct((n, LANES), F32)],
        grid=(n_tiles,),
        in_specs=[prev, main, nxt, _mod_spec(tps), _resident((1, d)), _resident(wqkv.shape),
                  _resident(cw.shape), _resident(wz.shape), _resident(wgt.shape),
                  _resident(aneg.shape), _resident(dtb.shape)],
        out_specs=[row(DN_DIM)] * 4 + [row(LANES)],
        scratch_shapes=[pltpu.VMEM((tm + 2 * HALO, d), BF16),
                        pltpu.VMEM((2 * DN_PC // DN_HEAD_DIM, tm + 2 * HALO, DN_HEAD_DIM), F32),
                        pltpu.VMEM((2 * DN_PC // DN_HEAD_DIM, tm, DN_HEAD_DIM), F32)],
        compiler_params=_cparams(),
        name="dn_project",
    )(h, h, h, mod, nw, wqkv, cw, wz, wgt, aneg, dtb)


def _bdot(a, b):
    return jnp.dot(a.astype(BF16), b.astype(BF16), preferred_element_type=F32)


def _exact_mask_dot(mask, x):
    m = mask.astype(BF16)
    hi = x.astype(BF16)
    r1 = x - hi.astype(F32)
    mid = r1.astype(BF16)
    lo = (r1 - mid.astype(F32)).astype(BF16)
    return (jnp.dot(m, hi, preferred_element_type=F32) + jnp.dot(m, mid, preferred_element_type=F32)
            + jnp.dot(m, lo, preferred_element_type=F32))


def _unit_tri_inverses(lmats, lowers, eye, blk):
    c = lmats[0].shape[0]
    xs = [eye - jnp.where(blk[0], l, 0.0) for l in lmats]
    size = DN_INV_BASE
    for lvl in range(1, len(blk)):
        sel = jnp.logical_and(blk[lvl], jnp.logical_not(blk[lvl - 1]))
        if size < SUBLANES:
            ts = [_bdot(jnp.where(sel, l, 0.0), x) for l, x in zip(lmats, xs)]
            xs = [x - _bdot(x, t) for x, t in zip(xs, ts)]
            size *= 2
            continue
        groups = range(c // size)

        def active(a, lower):
            return jnp.concatenate([a[g * size:(g + 1) * size] for g in groups if (g % 2 == 1) == lower], axis=0)

        def expand(a, lower, rest):
            parts, k = [], 0
            for g in groups:
                if (g % 2 == 1) == lower:
                    parts.append(a[k * size:(k + 1) * size])
                    k += 1
                else:
                    parts.append(rest[g * size:(g + 1) * size])
            return jnp.concatenate(parts, axis=0)

        zero = jnp.zeros_like(eye)
        ts = [_bdot(active(jnp.where(sel, l, 0.0), lo), x) for l, lo, x in zip(lmats, lowers, xs)]
        us = [_bdot(active(x, lo), expand(t, lo, zero)) for x, lo, t in zip(xs, lowers, ts)]
        xs = [expand(active(x, lo) - u, lo, x) for x, lo, u in zip(xs, lowers, us)]
        size *= 2
    return xs


def _dn_scan_kernel(qf_ref, kf_ref, vf_ref, bgf_ref, qb_ref, kb_ref, vb_ref, bgb_ref, s0_ref,
                    of_ref, ob_ref, sfin_ref, s_ref, *, n_steps, n_sub):
    n = pl.program_id(1)
    c = DN_CHUNK

    @pl.when(n == 0)
    def _():
        s_ref[...] = s0_ref[...]

    ri = lax.broadcasted_iota(jnp.int32, (c, c), 0)
    ci = lax.broadcasted_iota(jnp.int32, (c, c), 1)
    eye = (ri == ci).astype(F32)
    blk = []
    bs = DN_INV_BASE
    while bs <= c:
        blk.append((ri // bs) == (ci // bs))
        bs *= 2

    dirs = ((qf_ref, kf_ref, vf_ref, bgf_ref, of_ref, ri >= ci, ri > ci),
            (qb_ref, kb_ref, vb_ref, bgb_ref, ob_ref, ri <= ci, ri < ci))
    inst = []
    for d, (q_ref, k_ref, v_ref, bg_ref, o_ref, incl, strict) in enumerate(dirs):
        for sub in range(n_sub):
            rows = slice(sub * c, (sub + 1) * c)
            bg = bg_ref[rows, :]
            is_g = lax.broadcasted_iota(jnp.int32, bg.shape, 1) >= 2 * DN_HEADS
            bgg = jnp.where(is_g, bg, 0.0)
            cum = _exact_mask_dot(incl, bgg)
            cum_t = cum.T
            tot = jnp.sum(bgg, axis=0, keepdims=True)
            egc = jnp.exp(cum)
            ekd = jnp.exp(tot - cum)
            etot = jnp.exp(tot)
            for h in range(DN_HEADS):
                lb = d * DN_HEADS + h
                lg = 2 * DN_HEADS + lb
                hs = slice(h * DN_HEAD_DIM, (h + 1) * DN_HEAD_DIM)
                inst.append(dict(
                    d=d, h=h, sub=sub, o=o_ref, rows=rows, hs=hs, incl=incl, strict=strict,
                    q=q_ref[rows, hs], k=k_ref[rows, hs], v=v_ref[rows, hs],
                    beta=bg[:, lb:lb + 1], gc=cum[:, lg:lg + 1], gc_row=cum_t[lg:lg + 1, :],
                    egc=egc[:, lg:lg + 1], ekd=ekd[:, lg:lg + 1], etot=etot[:, lg:lg + 1]))
    nt = (((1,), (1,)), ((), ()))
    tn = (((0,), (0,)), ((), ()))
    kq = [lax.dot_general(jnp.concatenate([t["k"], t["q"]], axis=0), t["k"], nt, preferred_element_type=F32)
          for t in inst]
    decays = [jnp.where(t["incl"], jnp.exp(jnp.where(t["incl"], t["gc"] - t["gc_row"], 0.0)), 0.0) for t in inst]
    lmats = [jnp.where(t["strict"], t["beta"] * r[:c] * dc, 0.0) for t, r, dc in zip(inst, kq, decays)]
    qks = [(r[c:] * dc).astype(BF16) for r, dc in zip(kq, decays)]
    tinvs = _unit_tri_inverses(lmats, [t["d"] == 0 for t in inst], eye, blk)
    rhss = [jnp.concatenate([t["v"].astype(F32) * t["beta"], t["k"].astype(F32) * (t["beta"] * t["egc"])], axis=1)
            for t in inst]
    uws = [_bdot(ti, r) for ti, r in zip(tinvs, rhss)]
    wq_lhs = [jnp.concatenate([uw[:, DN_HEAD_DIM:], t["q"].astype(F32) * t["egc"]], axis=0).astype(BF16)
              for t, uw in zip(inst, uws)]
    k_decs = [(t["k"].astype(F32) * t["ekd"]).astype(BF16) for t in inst]
    for pos in range(n_sub):
        cur = [i for i, t in enumerate(inst) if t["sub"] == (pos if t["d"] == 0 else n_sub - 1 - pos)]
        wqs = [jnp.dot(wq_lhs[i], s_ref[inst[i]["d"], inst[i]["h"]].astype(BF16), preferred_element_type=F32)
               for i in cur]
        v_news = [(uws[i][:, :DN_HEAD_DIM] - wq[:c]).astype(BF16) for i, wq in zip(cur, wqs)]
        outs = [wq[c:] + jnp.dot(qks[i], vn, preferred_element_type=F32) for i, wq, vn in zip(cur, wqs, v_news)]
        upds = [lax.dot_general(k_decs[i], vn, tn, preferred_element_type=F32) for i, vn in zip(cur, v_news)]
        for i, o, upd in zip(cur, outs, upds):
            t = inst[i]
            t["o"][t["rows"], t["hs"]] = o.astype(t["o"].dtype)
            s_ref[t["d"], t["h"]] = s_ref[t["d"], t["h"]] * t["etot"] + upd

    @pl.when(n == n_steps - 1)
    def _():
        sfin_ref[...] = s_ref[...]


def dn_scan(q, k, v, bg, s0, *, batch, seq):
    n_sub = min(DN_SUB, seq // DN_CHUNK)
    rows = n_sub * DN_CHUNK
    nc = seq // rows
    fwd = lambda w: pl.BlockSpec((rows, w), lambda b, n: (b * nc + n, 0))
    bwd = lambda w: pl.BlockSpec((rows, w), lambda b, n: (b * nc + nc - 1 - n, 0))
    st = pl.BlockSpec((None, 2, DN_HEADS, DN_HEAD_DIM, DN_HEAD_DIM), lambda b, n: (b, 0, 0, 0, 0))
    return pl.pallas_call(
        functools.partial(_dn_scan_kernel, n_steps=nc, n_sub=n_sub),
        out_shape=[jax.ShapeDtypeStruct(q.shape, BF16), jax.ShapeDtypeStruct(q.shape, BF16),
                   jax.ShapeDtypeStruct(s0.shape, F32)],
        grid=(batch, nc),
        in_specs=[fwd(DN_DIM), fwd(DN_DIM), fwd(DN_DIM), fwd(LANES),
                  bwd(DN_DIM), bwd(DN_DIM), bwd(DN_DIM), bwd(LANES), st],
        out_specs=[fwd(DN_DIM), bwd(DN_DIM), st],
        scratch_shapes=[pltpu.VMEM((2, DN_HEADS, DN_HEAD_DIM, DN_HEAD_DIM), F32)],
        compiler_params=_cparams(2),
        name="dn_scan",
    )(q, k, v, bg, q, k, v, bg, s0)


def _dn_readout_kernel(of_ref, ob_ref, z_ref, res_ref, mod_ref, nw_ref, w_ref, o_ref, y_ref):
    gt = mod_ref[2:3, :]
    for h in range(DN_HEADS):
        hs = slice(h * DN_HEAD_DIM, (h + 1) * DN_HEAD_DIM)
        o = of_ref[:, hs].astype(F32) + ob_ref[:, hs].astype(F32)
        ms = jnp.mean(o * o, axis=-1, keepdims=True)
        y = o * lax.rsqrt(ms + EPS) * nw_ref[...]
        y_ref[:, hs] = (y * jax.nn.silu(z_ref[:, hs].astype(F32))).astype(BF16)
    o_ref[...] = res_ref[...] + gt * jnp.dot(y_ref[...], w_ref[...], preferred_element_type=F32)


def dn_readout(o_f, o_b, z, res, mod, nw, w, *, seq, tm):
    n, d = res.shape
    tps = seq // tm
    row = lambda w_: pl.BlockSpec((tm, w_), lambda i: (i, 0))
    return pl.pallas_call(
        _dn_readout_kernel,
        out_shape=jax.ShapeDtypeStruct((n, d), F32),
        grid=(n // tm,),
        in_specs=[row(DN_DIM), row(DN_DIM), row(DN_DIM), row(d), _mod_spec(tps),
                  _resident(nw.shape), _resident(w.shape)],
        out_specs=row(d),
        scratch_shapes=[pltpu.VMEM((tm, DN_DIM), BF16)],
        compiler_params=_cparams(),
        name="dn_readout",
    )(o_f, o_b, z, res, mod, nw, w)


def _pool_kernel(xp_ref, x_ref, xn_ref, mod_ref, nw_ref, wg_ref, scale_ref, o_ref, hin_ref, d_ref, y_ref,
                 *, tm, tiles_per_seq, seq):
    sh, sc, gt = mod_ref[0:1, :], mod_ref[1:2, :], mod_ref[2:3, :]
    hb = POOL_HALO
    pos = lax.rem(pl.program_id(0), tiles_per_seq)
    hp = jnp.where(pos == 0, 0.0, _rms_mod(xp_ref[...], nw_ref[...], sc, sh))
    hn = jnp.where(pos == tiles_per_seq - 1, 0.0, _rms_mod(xn_ref[...], nw_ref[...], sc, sh))
    hm = _rms_mod(x_ref[...], nw_ref[...], sc, sh)
    n_slab = D_MODEL // LANES
    for j in range(n_slab):
        lanes = slice(j * LANES, (j + 1) * LANES)
        hin_ref[j, 0:hb, :] = hp[:, lanes]
        hin_ref[j, hb:hb + tm, :] = hm[:, lanes]
        hin_ref[j, hb + tm:hb + tm + hb, :] = hn[:, lanes]
    rows = tm // POOL_STRIDE
    slabs_per_group = POOL_GROUP // LANES
    ti = pos * tm + POOL_STRIDE * lax.broadcasted_iota(jnp.int32, (rows, 1), 0)
    for gi, win in enumerate(POOL_WINDOWS):
        half = win // 2
        for v in range(POOL_STRIDE):
            t = ti + v
            cnt = (jnp.minimum(t + (win - half), seq) - jnp.maximum(t - half, 0)).astype(F32)
            for j in range(gi * slabs_per_group, (gi + 1) * slabs_per_group):
                s = None
                for dlt in range(-half, win - half):
                    term = hin_ref[j, pl.ds(hb + v + dlt, rows, stride=POOL_STRIDE), :]
                    s = term if s is None else s + term
                d_ref[j, pl.ds(v, rows, stride=POOL_STRIDE), :] = (
                    s / cnt - hin_ref[j, pl.ds(hb + v, rows, stride=POOL_STRIDE), :])
        cols = slice(gi * POOL_GROUP, (gi + 1) * POOL_GROUP)
        dd = jnp.concatenate([d_ref[j] for j in range(gi * slabs_per_group, (gi + 1) * slabs_per_group)], axis=1)
        y_ref[:, cols] = jnp.dot(dd.astype(BF16), wg_ref[gi], preferred_element_type=F32)
    o_ref[...] = x_ref[...] + gt * (y_ref[...] * scale_ref[...])


def pool_mixer(h, mod, nw, wg, scale, *, seq, tm):
    n, d = h.shape
    n_tiles = n // tm
    tps = seq // tm
    prev, main, nxt = _halo_specs(tm, POOL_HALO, n_tiles, d)
    return pl.pallas_call(
        functools.partial(_pool_kernel, tm=tm, tiles_per_seq=tps, seq=seq),
        out_shape=jax.ShapeDtypeStruct((n, d), F32),
        grid=(n_tiles,),
        in_specs=[prev, main, nxt, _mod_spec(tps), _resident((1, d)), _resident(wg.shape),
                  _resident(scale.shape)],
        out_specs=pl.BlockSpec((tm, d), lambda i: (i, 0)),
        scratch_shapes=[pltpu.VMEM((d // LANES, tm + 2 * POOL_HALO, LANES), F32),
                        pltpu.VMEM((d // LANES, tm, LANES), F32), pltpu.VMEM((tm, d), F32)],
        compiler_params=_cparams(),
        name="pool_mixer",
    )(h, h, h, mod, nw, wg, scale)


SWA_QKV_COLS = SWA_DIM + 4 * SWA_KV_DIM
SWA_NORM_COLS = SWA_DIM + 2 * SWA_KV_DIM
SWA_PC = 256
ROPE_QUARTER = SWA_HEAD_DIM // 4


def _swa_qkv_kernel(x_ref, mod_ref, nw_ref, w_ref, hnw_ref, seg_ref, cos_ref, sin_ref,
                    q_ref, k_ref, v_ref, *, rope):
    sh, sc = mod_ref[0:1, :], mod_ref[1:2, :]
    hin = _rms_mod(x_ref[...], nw_ref[...], sc, sh).astype(BF16)
    pc = SWA_PC
    n_chunks = SWA_QKV_COLS // pc
    q_chunks = SWA_DIM // pc
    k_chunks = 2 * SWA_KV_DIM // pc
    for j in range(n_chunks):
        cols = slice(j * pc, (j + 1) * pc)
        p = jnp.dot(hin, w_ref[:, cols], preferred_element_type=F32)
        if j < q_chunks + k_chunks:
            ms = jnp.dot((p * p).astype(BF16), seg_ref[...], preferred_element_type=F32)
            p = p * lax.rsqrt(ms + EPS) * hnw_ref[:, cols]
            if rope:
                lane = lax.broadcasted_iota(jnp.int32, (p.shape[0], LANES), 1)
                up = lax.rem(lane, 2 * ROPE_QUARTER) < ROPE_QUARTER
                halves = []
                for t in range(pc // LANES):
                    ph = p[:, t * LANES:(t + 1) * LANES]
                    partner = jnp.where(up, pltpu.roll(ph, LANES - ROPE_QUARTER, 1),
                                        pltpu.roll(ph, ROPE_QUARTER, 1))
                    halves.append(ph * cos_ref[...] + partner * sin_ref[...])
                p = jnp.concatenate(halves, axis=1)
        if j < q_chunks:
            q_ref[:, cols] = (p * (SWA_HEAD_DIM ** -0.5)).astype(BF16)
        elif j < q_chunks + k_chunks:
            k_ref[:, (j - q_chunks) * pc:(j - q_chunks + 1) * pc] = p.astype(BF16)
        else:
            jj = j - q_chunks - k_chunks
            v_ref[:, jj * pc:(jj + 1) * pc] = p.astype(BF16)


def swa_qkv(h, mod, nw, w, hnw, seg, cos, sin, *, seq, tm, rope):
    n, d = h.shape
    tps = seq // tm
    row = lambda w_: pl.BlockSpec((tm, w_), lambda i: (i, 0))
    tab = pl.BlockSpec((tm, LANES), lambda i: (i % tps, 0))
    kvw = 2 * SWA_KV_DIM
    return pl.pallas_call(
        functools.partial(_swa_qkv_kernel, rope=rope),
        out_shape=[jax.ShapeDtypeStruct((n, SWA_DIM), BF16), jax.ShapeDtypeStruct((n, kvw), BF16),
                   jax.ShapeDtypeStruct((n, kvw), BF16)],
        grid=(n // tm,),
        in_specs=[row(d), _mod_spec(tps), _resident((1, d)), _resident(w.shape), _resident(hnw.shape),
                  _resident(seg.shape), tab, tab],
        out_specs=[row(SWA_DIM), row(kvw), row(kvw)],
        compiler_params=_cparams(),
        name="swa_qkv",
    )(h, mod, nw, w, hnw, seg, cos, sin)


def _swa_attn_kernel(q_ref, kp_ref, kc_ref, kn_ref, vp_ref, vc_ref, vn_ref, kx_ref, vx_ref, sink_ref,
                     o_ref, *, blocks_per_seq):
    i = pl.program_id(1)
    blk = SWA_BLOCK
    ri = lax.broadcasted_iota(jnp.int32, (SWA_GROUP * blk, blk), 0) % blk
    ci = lax.broadcasted_iota(jnp.int32, (SWA_GROUP * blk, blk), 1)
    lane = lax.broadcasted_iota(jnp.int32, (blk, LANES), 1)
    low = lane < SWA_HEAD_DIM
    nt = (((1,), (1,)), ((), ()))
    lane_tiles = lambda a: [a[:, t * LANES:(t + 1) * LANES] for t in range(a.shape[1] // LANES)]
    inst = []
    for sub in range(SWA_SUB):
        rows = slice(sub * blk, (sub + 1) * blk)
        prev = (kp_ref, vp_ref, slice(0, blk)) if sub == 0 else (kc_ref, vc_ref, slice((sub - 1) * blk, sub * blk))
        nxt = ((kn_ref, vn_ref, slice(0, blk)) if sub == SWA_SUB - 1
               else (kc_ref, vc_ref, slice((sub + 1) * blk, (sub + 2) * blk)))
        gblock = i * SWA_SUB + sub
        ok_prev = jnp.logical_and(ci >= ri, gblock > 0)
        ok_next = jnp.logical_and(ci <= ri, gblock < blocks_per_seq - 1)
        for g in range(SWA_KV_HEADS):
            inst.append(dict(g=g, rows=rows, gs=slice(g * LANES, (g + 1) * LANES), prev=prev, nxt=nxt,
                             ok_prev=ok_prev, ok_next=ok_next))

    def stacked_queries(t):
        qs = []
        for half in range(2):
            qt = q_ref[t["rows"], (2 * t["g"] + half) * LANES:(2 * t["g"] + half + 1) * LANES]
            qs.append(jnp.where(low, qt, jnp.zeros_like(qt)))
            qs.append(jnp.where(low, jnp.zeros_like(qt), qt))
        return jnp.concatenate(qs, axis=0)

    def sink_column(g):
        return jnp.concatenate(
            [jnp.broadcast_to(sink_ref[SWA_GROUP * g + hh:SWA_GROUP * g + hh + 1, 0:1], (blk, 1))
             for hh in range(SWA_GROUP)], axis=0)

    def window(t, which, cur_ref, ctx_ref):
        p_ref, n_ref = t["prev"][which], t["nxt"][which]
        return jnp.concatenate([p_ref[t["prev"][2], t["gs"]], cur_ref[t["rows"], t["gs"]],
                                n_ref[t["nxt"][2], t["gs"]], ctx_ref[:, t["gs"]]], axis=0)

    qst = [stacked_queries(t) for t in inst]
    sinks = [sink_column(t["g"]) for t in inst]
    scores = [lax.dot_general(q, window(t, 0, kc_ref, kx_ref), nt, preferred_element_type=F32)
              for q, t in zip(qst, inst)]
    pieces = []
    for s, t in zip(scores, inst):
        ts = lane_tiles(s)
        pieces.append([jnp.where(t["ok_prev"], ts[0], NEG_INF), ts[1], jnp.where(t["ok_next"], ts[2], NEG_INF)]
                      + ts[3:])
    ms = [jnp.maximum(jnp.max(functools.reduce(jnp.maximum, ps), axis=-1, keepdims=True), sk)
          for ps, sk in zip(pieces, sinks)]
    es = [[jnp.exp(p - m) for p in ps] for ps, m in zip(pieces, ms)]
    invs = [1.0 / (jnp.sum(functools.reduce(jnp.add, e), axis=-1, keepdims=True) + jnp.exp(sk - m))
            for e, sk, m in zip(es, sinks, ms)]
    pvs = [jnp.dot(jnp.concatenate(e, axis=1).astype(BF16), window(t, 1, vc_ref, vx_ref),
                   preferred_element_type=F32) * inv
           for e, t, inv in zip(es, inst, invs)]
    for t, pv in zip(inst, pvs):
        for half in range(2):
            a = pv[(2 * half) * blk:(2 * half + 1) * blk]
            b = pv[(2 * half + 1) * blk:(2 * half + 2) * blk]
            o_ref[t["rows"], (2 * t["g"] + half) * LANES:(2 * t["g"] + half + 1) * LANES] = (
                jnp.where(low, a, b).astype(BF16))


def swa_attention(q, k2, v2, kx2, vx2, sink, *, batch, seq, ctx_len):
    nb = seq // SWA_BLOCK
    ns = nb // SWA_SUB
    kvw = 2 * SWA_KV_DIM
    qspec = pl.BlockSpec((SWA_SUB * SWA_BLOCK, SWA_DIM), lambda b, i: (b * ns + i, 0))
    cur = pl.BlockSpec((SWA_SUB * SWA_BLOCK, kvw), lambda b, i: (b * ns + i, 0))
    halo = lambda off: pl.BlockSpec(
        (SWA_BLOCK, kvw), lambda b, i: (b * nb + jnp.clip(i * SWA_SUB + off, 0, nb - 1), 0))
    xspec = pl.BlockSpec((ctx_len, kvw), lambda b, i: (b, 0))
    return pl.pallas_call(
        functools.partial(_swa_attn_kernel, blocks_per_seq=nb),
        out_shape=jax.ShapeDtypeStruct(q.shape, BF16),
        grid=(batch, ns),
        in_specs=[qspec, halo(-1), cur, halo(SWA_SUB), halo(-1), cur, halo(SWA_SUB), xspec, xspec,
                  _resident(sink.shape)],
        out_specs=qspec,
        compiler_params=_cparams(2),
        name="swa_attention",
    )(q, k2, k2, k2, v2, v2, v2, kx2, vx2, sink)


CNV_RB = 32
CNV_PC = 256
CNV_STRIDE = 4


def _conv_module_kernel(xp_ref, x_ref, xn_ref, mod_ref, nw_ref, w1_ref, dw_ref, lnw_ref, lnb_ref, w2_ref,
                        o_ref, hin_ref, u_ref, c_ref, s_ref, *, tm, tiles_per_seq):
    sh, sc, gt = mod_ref[0:1, :], mod_ref[1:2, :], mod_ref[2:3, :]
    _fill_hin_ext(hin_ref, xp_ref, x_ref, xn_ref, nw_ref[...], sc, sh, tiles_per_seq, HALO, tm)
    d = D_MODEL
    n_slab = d // LANES
    for c in range(d // CNV_PC):
        cols = slice(c * CNV_PC, (c + 1) * CNV_PC)
        gcols = slice(d + c * CNV_PC, d + (c + 1) * CNV_PC)
        a1 = jnp.dot(hin_ref[...], w1_ref[:, cols], preferred_element_type=F32)
        a2 = jnp.dot(hin_ref[...], w1_ref[:, gcols], preferred_element_type=F32)
        u = a1 * jax.nn.sigmoid(a2)
        for t in range(CNV_PC // LANES):
            u_ref[c * (CNV_PC // LANES) + t] = u[:, t * LANES:(t + 1) * LANES]
    pad = CNV_WIDTH // 2
    rows = tm // CNV_STRIDE
    for j in range(n_slab):
        lanes = slice(j * LANES, (j + 1) * LANES)
        for v in range(CNV_STRIDE):
            acc = None
            for k in range(CNV_WIDTH):
                term = dw_ref[k:k + 1, lanes] * u_ref[j, pl.ds(HALO + v + k - pad, rows, stride=CNV_STRIDE), :]
                acc = term if acc is None else acc + term
            c_ref[j, pl.ds(v, rows, stride=CNV_STRIDE), :] = acc
    for rb in range(tm // CNV_RB):
        rsl = slice(rb * CNV_RB, (rb + 1) * CNV_RB)
        xs = [c_ref[j, rsl, :] for j in range(n_slab)]
        tot = xs[0]
        for xj in xs[1:]:
            tot = tot + xj
        mu = jnp.sum(tot, axis=-1, keepdims=True) * (1.0 / d)
        xcs = [xj - mu for xj in xs]
        sq = xcs[0] * xcs[0]
        for xc in xcs[1:]:
            sq = sq + xc * xc
        rstd = lax.rsqrt(jnp.sum(sq, axis=-1, keepdims=True) * (1.0 / d) + EPS)
        for j, xc in enumerate(xcs):
            lanes = slice(j * LANES, (j + 1) * LANES)
            y = xc * rstd * lnw_ref[:, lanes] + lnb_ref[:, lanes]
            s_ref[rsl, lanes] = jax.nn.silu(y).astype(BF16)
    o_ref[...] = x_ref[...] + gt * jnp.dot(s_ref[...], w2_ref[...], preferred_element_type=F32)


def conv_module(h, mod, nw, w1, dw, lnw, lnb, w2, *, seq, tm):
    n, d = h.shape
    n_tiles = n // tm
    tps = seq // tm
    prev, main, nxt = _halo_specs(tm, HALO, n_tiles, d)
    return pl.pallas_call(
        functools.partial(_conv_module_kernel, tm=tm, tiles_per_seq=tps),
        out_shape=jax.ShapeDtypeStruct((n, d), F32),
        grid=(n_tiles,),
        in_specs=[prev, main, nxt, _mod_spec(tps), _resident((1, d)), _resident(w1.shape),
                  _resident(dw.shape), _resident(lnw.shape), _resident(lnb.shape), _resident(w2.shape)],
        out_specs=pl.BlockSpec((tm, d), lambda i: (i, 0)),
        scratch_shapes=[pltpu.VMEM((tm + 2 * HALO, d), BF16),
                        pltpu.VMEM((d // LANES, tm + 2 * HALO, LANES), F32),
                        pltpu.VMEM((d // LANES, tm, LANES), F32),
                        pltpu.VMEM((tm, d), BF16)],
        compiler_params=_cparams(),
        name="conv_module",
    )(h, h, h, mod, nw, w1, dw, lnw, lnb, w2)


def _rope_tables(seq):
    n_freq = SWA_HEAD_DIM // 4
    inv_freq = ROPE_BASE ** (-jnp.arange(n_freq, dtype=F32) / n_freq)
    pos = jnp.arange(seq)
    rows = (pos // GRID_W).astype(F32)
    cols = (pos % GRID_W).astype(F32)
    ang_r = rows[:, None] * inv_freq
    ang_c = cols[:, None] * inv_freq
    ang = jnp.concatenate([ang_r, ang_r, ang_c, ang_c], axis=-1)
    sign = jnp.where((jnp.arange(SWA_HEAD_DIM) % (2 * ROPE_QUARTER)) < ROPE_QUARTER, -1.0, 1.0)
    cos = jnp.tile(jnp.cos(ang), (1, LANES // SWA_HEAD_DIM))
    sin = jnp.tile(jnp.sin(ang) * sign, (1, LANES // SWA_HEAD_DIM))
    return cos, sin


TILE_ROWS = 1024
CNV_TILE_ROWS = 512


def _tile_rows(seq, rows=TILE_ROWS):
    return min(seq, rows)


def kernel(x, c, ctx, c_ctx, ada_w, ada_b, norm_mix_w, norm_ffn_w, dn_w_in, dn_conv, dn_a_log, dn_dt_bias,
           dn_norm_w, dn_w_out, pool_w_grp, pool_scale, swa_w_qkv, swa_q_norm, swa_k_norm, swa_sink,
           swa_w_out, cnv_w_pw1, cnv_dw, cnv_ln_w, cnv_ln_b, cnv_w_pw2, ffn_w_gate, ffn_w_up, ffn_conv,
           ffn_conv_b, ffn_w_down):
    bsz, seq, d = x.shape
    ctx_len = ctx.shape[1]
    depth = ada_w.shape[0]
    assert d == D_MODEL and seq % GRID_W == 0 and ctx_len % LANES == 0
    for n in (seq, ctx_len):
        assert n % _tile_rows(n) == 0 and n % _tile_rows(n, CNV_TILE_ROWS) == 0 and n % DN_CHUNK == 0
    assert seq % (SWA_SUB * SWA_BLOCK) == 0 and (seq // DN_CHUNK) % DN_SUB == 0
    h = x.reshape(bsz * seq, d)
    hc = ctx.reshape(bsz * ctx_len, d)
    p = dict(norm_mix_w=norm_mix_w, norm_ffn_w=norm_ffn_w, dn_w_in=dn_w_in, dn_conv=dn_conv,
             dn_a_log=dn_a_log, dn_dt_bias=dn_dt_bias, dn_norm_w=dn_norm_w, dn_w_out=dn_w_out,
             pool_w_grp=pool_w_grp, pool_scale=pool_scale, swa_w_qkv=swa_w_qkv, swa_q_norm=swa_q_norm,
             swa_k_norm=swa_k_norm, swa_sink=swa_sink, swa_w_out=swa_w_out, cnv_w_pw1=cnv_w_pw1,
             cnv_dw=cnv_dw, cnv_ln_w=cnv_ln_w, cnv_ln_b=cnv_ln_b, cnv_w_pw2=cnv_w_pw2,
             ffn_w_gate=ffn_w_gate, ffn_w_up=ffn_w_up, ffn_conv=ffn_conv, ffn_conv_b=ffn_conv_b,
             ffn_w_down=ffn_w_down)
    dims = dict(bsz=bsz, seq=seq, ctx_len=ctx_len, depth=depth)
    mods = modulation(c, c_ctx, ada_w, ada_b)
    for i in range(depth):
        mod_l, mod_c = layer_modulation(mods, i, bsz)
        h, hc = mixer_layer(i, h, hc, mod_l, mod_c, p, dims)
        h, hc = ffn_layer(i, h, hc, mod_l, mod_c, p, dims)
    return h.reshape(bsz, seq, d)


def modulation(c, c_ctx, ada_w, ada_b):
    bsz, d = c.shape
    n_cond = -(-(bsz + 1) // SUBLANES) * SUBLANES
    cond = jnp.zeros((n_cond, d), F32).at[:bsz].set(c).at[bsz].set(c_ctx)
    return ada_modulation(cond, ada_w, ada_b)


def layer_modulation(mods, i, bsz):
    d = mods.shape[-1] // 6
    mod_l = mods[i, :bsz].reshape(bsz, 6, d)
    mod_c = jnp.broadcast_to(mods[i, bsz].reshape(1, 6, d), (bsz, 6, d))
    return mod_l, mod_c


def _keeps_ctx(i, depth):
    return any((j % N_MIXERS) in CTX_READING_MIXERS for j in range(i + 1, depth))


def ffn_layer(i, h, hc, mod_l, mod_c, p, dims):
    d = h.shape[-1]
    seq, ctx_len = dims["seq"], dims["ctx_len"]
    ffn_args = (p["norm_ffn_w"][i].reshape(1, d), p["ffn_w_gate"].astype(BF16),
                p["ffn_w_up"].astype(BF16), p["ffn_conv"][i], p["ffn_conv_b"][i].reshape(1, FFN_DIM),
                p["ffn_w_down"].astype(BF16))
    h = conv_ffn(h, mod_l, *ffn_args, layer=i, seq=seq, tm=_tile_rows(seq))
    if _keeps_ctx(i, dims["depth"]):
        hc = conv_ffn(hc, mod_c, *ffn_args, layer=i, seq=ctx_len, tm=_tile_rows(ctx_len))
    return h, hc


def mixer_layer(i, h, hc, mod_l, mod_c, p, dims):
    d = h.shape[-1]
    bsz, seq, ctx_len = dims["bsz"], dims["seq"], dims["ctx_len"]
    tm_c = _tile_rows(ctx_len)
    kind, slot = i % N_MIXERS, i // N_MIXERS
    keep_ctx = _keeps_ctx(i, dims["depth"])
    nw_mix = p["norm_mix_w"][i].reshape(1, d)
    lat = dict(seq=seq, tm=_tile_rows(seq))
    cx = dict(seq=ctx_len, tm=tm_c)

    if kind == MIX_DELTANET:
        w_in = p["dn_w_in"][slot]
        wqkv = w_in[:, :3 * DN_DIM].astype(BF16)
        wz = w_in[:, 3 * DN_DIM:4 * DN_DIM].astype(BF16)
        wgt = jnp.zeros((d, LANES), F32).at[:, :4 * DN_HEADS].set(w_in[:, 4 * DN_DIM:]).astype(BF16)
        aneg = jnp.zeros((1, LANES), F32).at[0, 2 * DN_HEADS:4 * DN_HEADS].set(
            -jnp.exp(p["dn_a_log"][slot].astype(F32)).reshape(-1))
        dtb = jnp.zeros((1, LANES), F32).at[0, 2 * DN_HEADS:4 * DN_HEADS].set(
            p["dn_dt_bias"][slot].astype(F32).reshape(-1))
        hnw = p["dn_norm_w"][slot].reshape(1, DN_HEAD_DIM)
        wout = p["dn_w_out"][slot].astype(BF16)
        proj = functools.partial(dn_project, nw=nw_mix, wqkv=wqkv, cw=p["dn_conv"][slot], wz=wz, wgt=wgt,
                                 aneg=aneg, dtb=dtb)
        qc, kc, vc, zc, bgc = proj(hc, mod_c, **cx)
        ql, kl, vl, zl, bgl = proj(h, mod_l, **lat)
        s0 = jnp.zeros((bsz, 2, DN_HEADS, DN_HEAD_DIM, DN_HEAD_DIM), F32)
        ocf, ocb, s_ctx = dn_scan(qc, kc, vc, bgc, s0, batch=bsz, seq=ctx_len)
        olf, olb, _ = dn_scan(ql, kl, vl, bgl, s_ctx, batch=bsz, seq=seq)
        h = dn_readout(olf, olb, zl, h, mod_l, hnw, wout, **lat)
        if keep_ctx:
            hc = dn_readout(ocf, ocb, zc, hc, mod_c, hnw, wout, **cx)
    elif kind == MIX_POOL:
        wg = p["pool_w_grp"][slot].astype(BF16)
        scale = p["pool_scale"][slot].reshape(1, d)
        h = pool_mixer(h, mod_l, nw_mix, wg, scale, **lat)
        if keep_ctx:
            hc = pool_mixer(hc, mod_c, nw_mix, wg, scale, **cx)
    elif kind == MIX_SWA:
        assert not keep_ctx
        wq = p["swa_w_qkv"][slot]
        dup = lambda w_: jnp.repeat(w_.reshape(d, SWA_KV_HEADS, 1, SWA_HEAD_DIM), 2, axis=2).reshape(
            d, 2 * SWA_KV_DIM)
        w_all = jnp.concatenate([wq[:, :SWA_DIM], dup(wq[:, SWA_DIM:SWA_DIM + SWA_KV_DIM]),
                                 dup(wq[:, SWA_DIM + SWA_KV_DIM:])], axis=1).astype(BF16)
        hnw = jnp.concatenate([jnp.tile(p["swa_q_norm"][slot], SWA_HEADS),
                               jnp.tile(p["swa_k_norm"][slot], 2 * SWA_KV_HEADS)]).reshape(1, SWA_NORM_COLS)
        li = jnp.arange(SWA_PC)
        seg = jnp.where((li[:, None] // SWA_HEAD_DIM) == (li[None, :] // SWA_HEAD_DIM),
                        1.0 / SWA_HEAD_DIM, 0.0).astype(BF16)
        cos, sin = _rope_tables(seq)
        sink = jnp.broadcast_to(p["swa_sink"][slot].astype(F32).reshape(SWA_HEADS, 1), (SWA_HEADS, LANES))
        ql, k2, v2 = swa_qkv(h, mod_l, nw_mix, w_all, hnw, seg, cos, sin, rope=True, **lat)
        _, kx2, vx2 = swa_qkv(hc, mod_c, nw_mix, w_all, hnw, seg, cos[:tm_c], sin[:tm_c], rope=False, **cx)
        o = swa_attention(ql, k2, v2, kx2, vx2, sink, batch=bsz, seq=seq, ctx_len=ctx_len)
        h = matmul_residual(o, h, mod_l, p["swa_w_out"][slot].astype(BF16), **lat)
    else:
        args = (nw_mix, p["cnv_w_pw1"][slot].astype(BF16), p["cnv_dw"][slot], p["cnv_ln_w"][slot].reshape(1, d),
                p["cnv_ln_b"][slot].reshape(1, d), p["cnv_w_pw2"][slot].astype(BF16))
        h = conv_module(h, mod_l, *args, seq=seq, tm=_tile_rows(seq, CNV_TILE_ROWS))
        if keep_ctx:
            hc = conv_module(hc, mod_c, *args, seq=ctx_len, tm=_tile_rows(ctx_len, CNV_TILE_ROWS))
    return h, hc
```

```python
import functools

import jax
import jax.numpy as jnp
from jax import lax
from jax.experimental import pallas as pl
from jax.experimental.pallas import tpu as pltpu

D_MODEL = 1024
EPS = 1e-6
NEG_INF = -1e30
GRID_W = 64
ROPE_BASE = 10000.0
N_MIXERS = 4
MIX_DELTANET, MIX_POOL, MIX_SWA, MIX_CONV = 0, 1, 2, 3
CTX_READING_MIXERS = (MIX_DELTANET, MIX_SWA)

DN_HEADS = 8
DN_HEAD_DIM = 128
DN_DIM = DN_HEADS * DN_HEAD_DIM
DN_CONV = 5
DN_CHUNK = 128
DN_INV_BASE = 2
DN_SUB = 4

POOL_WINDOWS = (2, 4, 8, 16)
POOL_GROUP = D_MODEL // len(POOL_WINDOWS)

SWA_HEADS = 16
SWA_KV_HEADS = 4
SWA_HEAD_DIM = 64
SWA_GROUP = SWA_HEADS // SWA_KV_HEADS
SWA_DIM = SWA_HEADS * SWA_HEAD_DIM
SWA_KV_DIM = SWA_KV_HEADS * SWA_HEAD_DIM
SWA_WINDOW = 128
SWA_BLOCK = 128
SWA_SUB = 4

CNV_WIDTH = 31
FFN_DIM = 2816
FFN_CONV = 3

LANES = 128
SUBLANES = 8
VMEM_LIMIT_BYTES = 56 * 1024 * 1024

F32 = jnp.float32
BF16 = jnp.bfloat16
HALO = 2 * SUBLANES
POOL_HALO = SUBLANES
POOL_STRIDE = 4


def _cparams(n_axes=1):
    return pltpu.CompilerParams(
        dimension_semantics=("arbitrary",) * n_axes,
        vmem_limit_bytes=VMEM_LIMIT_BYTES,
    )


def _resident(shape):
    nd = len(shape)
    return pl.BlockSpec(shape, lambda *_: (0,) * nd, pipeline_mode=pl.Buffered(1))


def _resident_layer(shape, layer):
    return pl.BlockSpec((None,) + tuple(shape[1:]), lambda *_: (layer,) + (0,) * (len(shape) - 1),
                        pipeline_mode=pl.Buffered(1))


def _halo_specs(tm, hb, n_tiles, width):
    r = tm // hb
    last = n_tiles * r - 1
    prev = pl.BlockSpec((hb, width), lambda i: (jnp.maximum(i * r - 1, 0), 0))
    main = pl.BlockSpec((tm, width), lambda i: (i, 0))
    nxt = pl.BlockSpec((hb, width), lambda i: (jnp.minimum((i + 1) * r, last), 0))
    return prev, main, nxt


def _mod_spec(tiles_per_seq):
    return pl.BlockSpec((None, 6, D_MODEL), lambda i: (i // tiles_per_seq, 0, 0))


def _rms_mod(x, nw, sc, sh):
    ms = jnp.mean(x * x, axis=-1, keepdims=True)
    return (x * lax.rsqrt(ms + EPS) * nw) * (1.0 + sc) + sh


def _fill_hin_ext(hin_ref, xp_ref, x_ref, xn_ref, nw, sc, sh, tiles_per_seq, hb, tm):
    i = pl.program_id(0)
    pos = lax.rem(i, tiles_per_seq)
    first = pos == 0
    last = pos == tiles_per_seq - 1
    hp = jnp.where(first, 0.0, _rms_mod(xp_ref[...], nw, sc, sh))
    hn = jnp.where(last, 0.0, _rms_mod(xn_ref[...], nw, sc, sh))
    hin_ref[0:hb, :] = hp.astype(hin_ref.dtype)
    hin_ref[hb:hb + tm, :] = _rms_mod(x_ref[...], nw, sc, sh).astype(hin_ref.dtype)
    hin_ref[hb + tm:hb + tm + hb, :] = hn.astype(hin_ref.dtype)


def _shift_rows(a, d, n_rows):
    if d == 0:
        return a
    return pltpu.roll(a, (-d) % n_rows, 0)


ADA_TN = 1536


def _ada_kernel(cond_ref, w_ref, b_ref, o_ref):
    s = jax.nn.silu(cond_ref[...])
    o_ref[...] = jnp.dot(s, w_ref[...], preferred_element_type=F32,
                         precision=lax.Precision.HIGHEST) + b_ref[...]


def ada_modulation(cond, ada_w, ada_b):
    depth, d, n = ada_w.shape
    rows = cond.shape[0]
    return pl.pallas_call(
        _ada_kernel,
        out_shape=jax.ShapeDtypeStruct((depth, rows, n), F32),
        grid=(depth, n // ADA_TN),
        in_specs=[
            pl.BlockSpec((rows, d), lambda l, j: (0, 0)),
            pl.BlockSpec((None, d, ADA_TN), lambda l, j: (l, 0, j)),
            pl.BlockSpec((None, 1, ADA_TN), lambda l, j: (l, 0, j)),
        ],
        out_specs=pl.BlockSpec((None, rows, ADA_TN), lambda l, j: (l, 0, j)),
        compiler_params=_cparams(2),
        name="ada_modulation",
    )(cond, ada_w, ada_b.reshape(depth, 1, n))


FFN_FC = 256
FFN_DOWN_GROUP = 6


def _ffn_kernel(xp_ref, x_ref, xn_ref, mod_ref, nw_ref, wg_ref, wu_ref, cw_ref, cb_ref, wd_ref,
                o_ref, hin_ref, act_ref, acc_ref, *, tm, tiles_per_seq):
    sh, sc, gt = mod_ref[3:4, :], mod_ref[4:5, :], mod_ref[5:6, :]
    _fill_hin_ext(hin_ref, xp_ref, x_ref, xn_ref, nw_ref[...], sc, sh, tiles_per_seq, HALO, tm)
    n_ext = tm + 2 * HALO
    n_chunks = FFN_DIM // FFN_FC
    group_start = 0
    for c in range(n_chunks):
        cols = slice(c * FFN_FC, (c + 1) * FFN_FC)
        g_ext = jnp.dot(hin_ref[...], wg_ref[:, cols], preferred_element_type=F32)
        u = jnp.dot(hin_ref[HALO:HALO + tm, :], wu_ref[:, cols], preferred_element_type=F32)
        g = cb_ref[:, cols]
        for k in range(FFN_CONV):
            g = g + cw_ref[k:k + 1, cols] * _shift_rows(g_ext, k - 1, n_ext)[HALO:HALO + tm, :]
        act_ref[:, cols] = (jax.nn.silu(g) * u).astype(BF16)
        if (c + 1) % FFN_DOWN_GROUP == 0 or c == n_chunks - 1:
            ks = slice(group_start * FFN_FC, (c + 1) * FFN_FC)
            y = jnp.dot(act_ref[:, ks], wd_ref[ks, :], preferred_element_type=F32)
            if group_start == 0:
                acc_ref[...] = y
            else:
                acc_ref[...] += y
            group_start = c + 1
    o_ref[...] = x_ref[...] + gt * acc_ref[...]


def conv_ffn(h, mod, nw, wg, wu, cw, cb, wd, *, layer, seq, tm):
    n, d = h.shape
    n_tiles = n // tm
    tps = seq // tm
    prev, main, nxt = _halo_specs(tm, HALO, n_tiles, d)
    return pl.pallas_call(
        functools.partial(_ffn_kernel, tm=tm, tiles_per_seq=tps),
        out_shape=jax.ShapeDtypeStruct((n, d), F32),
        grid=(n_tiles,),
        in_specs=[prev, main, nxt, _mod_spec(tps), _resident((1, d)),
                  _resident_layer(wg.shape, layer), _resident_layer(wu.shape, layer), _resident(cw.shape),
                  _resident(cb.shape), _resident_layer(wd.shape, layer)],
        out_specs=pl.BlockSpec((tm, d), lambda i: (i, 0)),
        scratch_shapes=[pltpu.VMEM((tm + 2 * HALO, d), BF16), pltpu.VMEM((tm, FFN_DIM), BF16),
                        pltpu.VMEM((tm, d), F32)],
        compiler_params=_cparams(),
        name="conv_ffn",
    )(h, h, h, mod, nw, wg, wu, cw, cb, wd)


def _mm_res_kernel(a_ref, res_ref, mod_ref, w_ref, o_ref):
    gt = mod_ref[2:3, :]
    o_ref[...] = res_ref[...] + gt * jnp.dot(a_ref[...], w_ref[...], preferred_element_type=F32)


def matmul_residual(a, res, mod, w, *, seq, tm):
    n, k = a.shape
    d = w.shape[1]
    tps = seq // tm
    return pl.pallas_call(
        _mm_res_kernel,
        out_shape=jax.ShapeDtypeStruct((n, d), F32),
        grid=(n // tm,),
        in_specs=[pl.BlockSpec((tm, k), lambda i: (i, 0)), pl.BlockSpec((tm, d), lambda i: (i, 0)),
                  _mod_spec(tps), _resident(w.shape)],
        out_specs=pl.BlockSpec((tm, d), lambda i: (i, 0)),
        compiler_params=_cparams(),
        name="matmul_residual",
    )(a, res, mod, w)


DN_PC = 256
DN_STRIDE = 4


def _dn_proj_kernel(xp_ref, x_ref, xn_ref, mod_ref, nw_ref, wqkv_ref, cw_ref, wz_ref, wgt_ref,
                    aneg_ref, dtb_ref, qkvz_ref, bg_ref, hin_ref, p_ref, a_ref,
                    *, tm, tiles_per_seq):
    sh, sc = mod_ref[0:1, :], mod_ref[1:2, :]
    _fill_hin_ext(hin_ref, xp_ref, x_ref, xn_ref, nw_ref[...], sc, sh, tiles_per_seq, HALO, tm)
    per_out = DN_DIM // DN_PC
    heads_pc = DN_PC // DN_HEAD_DIM
    rows = tm // DN_STRIDE
    for c in range(3 * per_out):
        cols = slice(c * DN_PC, (c + 1) * DN_PC)
        p_ext = jnp.dot(hin_ref[...], wqkv_ref[:, cols], preferred_element_type=F32)
        which = c // per_out
        for hh in range(heads_pc):
            slot = (c % 2) * heads_pc + hh
            lanes = slice(c * DN_PC + hh * DN_HEAD_DIM, c * DN_PC + (hh + 1) * DN_HEAD_DIM)
            p_ref[slot] = p_ext[:, hh * DN_HEAD_DIM:(hh + 1) * DN_HEAD_DIM]
            for v in range(DN_STRIDE):
                acc = None
                for k in range(DN_CONV):
                    start = HALO + v + k - DN_CONV // 2
                    term = cw_ref[k:k + 1, lanes] * p_ref[slot, pl.ds(start, rows, stride=DN_STRIDE), :]
                    acc = term if acc is None else acc + term
                ah = jax.nn.silu(acc)
                if which < 2:
                    ah = ah * lax.rsqrt(jnp.sum(ah * ah, axis=-1, keepdims=True) + EPS)
                    if which == 0:
                        ah = ah * (DN_HEAD_DIM ** -0.5)
                a_ref[slot, pl.ds(v, rows, stride=DN_STRIDE), :] = ah
            qkvz_ref[:, lanes] = a_ref[slot].astype(BF16)
    hm = hin_ref[HALO:HALO + tm, :]
    qkvz_ref[:, 3 * DN_DIM:] = jnp.dot(hm, wz_ref[...], preferred_element_type=F32).astype(BF16)
    gates = jnp.dot(hm, wgt_ref[...], preferred_element_type=F32)
    lane = lax.broadcasted_iota(jnp.int32, gates.shape, 1)
    beta = jax.nn.sigmoid(gates)
    g = aneg_ref[...] * jax.nn.softplus(gates + dtb_ref[...])
    bg_ref[...] = jnp.where(lane < 2 * DN_HEADS, beta, g)


def dn_project(h, mod, nw, wqkv, cw, wz, wgt, aneg, dtb, *, seq, tm):
    n, d = h.shape
    n_tiles = n // tm
    tps = seq // tm
    prev, main, nxt = _halo_specs(tm, HALO, n_tiles, d)
    row = lambda w: pl.BlockSpec((tm, w), lambda i: (i, 0))
    return pl.pallas_call(
        functools.partial(_dn_proj_kernel, tm=tm, tiles_per_seq=tps),
        out_shape=[jax.ShapeDtypeStruct((n, 4 * DN_DIM), BF16), jax.ShapeDtypeStruct((n, LANES), F32)],
        grid=(n_tiles,),
        in_specs=[prev, main, nxt, _mod_spec(tps), _resident((1, d)), _resident(wqkv.shape),
                  _resident(cw.shape), _resident(wz.shape), _resident(wgt.shape),
                  _resident(aneg.shape), _resident(dtb.shape)],
        out_specs=[row(4 * DN_DIM), row(LANES)],
        scratch_shapes=[pltpu.VMEM((tm + 2 * HALO, d), BF16),
                        pltpu.VMEM((2 * DN_PC // DN_HEAD_DIM, tm + 2 * HALO, DN_HEAD_DIM), F32),
                        pltpu.VMEM((2 * DN_PC // DN_HEAD_DIM, tm, DN_HEAD_DIM), F32)],
        compiler_params=_cparams(),
        name="dn_project",
    )(h, h, h, mod, nw, wqkv, cw, wz, wgt, aneg, dtb)


def _bdot(a, b):
    return jnp.dot(a.astype(BF16), b.astype(BF16), preferred_element_type=F32)


def _exact_mask_dot(mask, x):
    m = mask.astype(BF16)
    hi = x.astype(BF16)
    r1 = x - hi.astype(F32)
    mid = r1.astype(BF16)
    lo = (r1 - mid.astype(F32)).astype(BF16)
    return (jnp.dot(m, hi, preferred_element_type=F32) + jnp.dot(m, mid, preferred_element_type=F32)
            + jnp.dot(m, lo, preferred_element_type=F32))


def _unit_tri_inverses(lmats, lowers, eye, blk):
    c = lmats[0].shape[0]
    xs = [eye - jnp.where(blk[0], l, 0.0) for l in lmats]
    size = DN_INV_BASE
    for lvl in range(1, len(blk)):
        sel = jnp.logical_and(blk[lvl], jnp.logical_not(blk[lvl - 1]))
        if size < SUBLANES:
            ts = [_bdot(jnp.where(sel, l, 0.0), x) for l, x in zip(lmats, xs)]
            xs = [x - _bdot(x, t) for x, t in zip(xs, ts)]
            size *= 2
            continue
        groups = range(c // size)

        def active(a, lower):
            return jnp.concatenate([a[g * size:(g + 1) * size] for g in groups if (g % 2 == 1) == lower], axis=0)

        def expand(a, lower, rest):
            parts, k = [], 0
            for g in groups:
                if (g % 2 == 1) == lower:
                    parts.append(a[k * size:(k + 1) * size])
                    k += 1
                else:
                    parts.append(rest[g * size:(g + 1) * size])
            return jnp.concatenate(parts, axis=0)

        zero = jnp.zeros_like(eye)
        ts = [_bdot(active(jnp.where(sel, l, 0.0), lo), x) for l, lo, x in zip(lmats, lowers, xs)]
        us = [_bdot(active(x, lo), expand(t, lo, zero)) for x, lo, t in zip(xs, lowers, ts)]
        xs = [expand(active(x, lo) - u, lo, x) for x, lo, u in zip(xs, lowers, us)]
        size *= 2
    return xs


def _dn_scan_kernel(qf_ref, kf_ref, vf_ref, bgf_ref, qb_ref, kb_ref, vb_ref, bgb_ref, s0_ref,
                    of_ref, ob_ref, sfin_ref, s_ref, *, n_steps, n_sub):
    n = pl.program_id(1)
    c = DN_CHUNK

    @pl.when(n == 0)
    def _():
        s_ref[...] = s0_ref[...]

    ri = lax.broadcasted_iota(jnp.int32, (c, c), 0)
    ci = lax.broadcasted_iota(jnp.int32, (c, c), 1)
    eye = (ri == ci).astype(F32)
    blk = []
    bs = DN_INV_BASE
    while bs <= c:
        blk.append((ri // bs) == (ci // bs))
        bs *= 2

    dirs = ((qf_ref, kf_ref, vf_ref, bgf_ref, of_ref, ri >= ci, ri > ci),
            (qb_ref, kb_ref, vb_ref, bgb_ref, ob_ref, ri <= ci, ri < ci))
    inst = []
    for d, (q_ref, k_ref, v_ref, bg_ref, o_ref, incl, strict) in enumerate(dirs):
        for sub in range(n_sub):
            rows = slice(sub * c, (sub + 1) * c)
            bg = bg_ref[rows, :]
            is_g = lax.broadcasted_iota(jnp.int32, bg.shape, 1) >= 2 * DN_HEADS
            bgg = jnp.where(is_g, bg, 0.0)
            cum = _exact_mask_dot(incl, bgg)
            cum_t = cum.T
            tot = jnp.sum(bgg, axis=0, keepdims=True)
            egc = jnp.exp(cum)
            ekd = jnp.exp(tot - cum)
            etot = jnp.exp(tot)
            for h in range(DN_HEADS):
                lb = d * DN_HEADS + h
                lg = 2 * DN_HEADS + lb
                hs = slice(h * DN_HEAD_DIM, (h + 1) * DN_HEAD_DIM)
                inst.append(dict(
                    d=d, h=h, sub=sub, o=o_ref, rows=rows, hs=hs, incl=incl, strict=strict,
                    q=q_ref[rows, hs], k=k_ref[rows, hs], v=v_ref[rows, hs],
                    beta=bg[:, lb:lb + 1], gc=cum[:, lg:lg + 1], gc_row=cum_t[lg:lg + 1, :],
                    egc=egc[:, lg:lg + 1], ekd=ekd[:, lg:lg + 1], etot=etot[:, lg:lg + 1]))
    nt = (((1,), (1,)), ((), ()))
    tn = (((0,), (0,)), ((), ()))
    kq = [lax.dot_general(jnp.concatenate([t["k"], t["q"]], axis=0), t["k"], nt, preferred_element_type=F32)
          for t in inst]
    decays = [jnp.where(t["incl"], jnp.exp(jnp.where(t["incl"], t["gc"] - t["gc_row"], 0.0)), 0.0) for t in inst]
    lmats = [jnp.where(t["strict"], t["beta"] * r[:c] * dc, 0.0) for t, r, dc in zip(inst, kq, decays)]
    qks = [(r[c:] * dc).astype(BF16) for r, dc in zip(kq, decays)]
    tinvs = _unit_tri_inverses(lmats, [t["d"] == 0 for t in inst], eye, blk)
    rhss = [jnp.concatenate([t["v"].astype(F32) * t["beta"], t["k"].astype(F32) * (t["beta"] * t["egc"])], axis=1)
            for t in inst]
    uws = [_bdot(ti, r) for ti, r in zip(tinvs, rhss)]
    wq_lhs = [jnp.concatenate([uw[:, DN_HEAD_DIM:], t["q"].astype(F32) * t["egc"]], axis=0).astype(BF16)
              for t, uw in zip(inst, uws)]
    k_decs = [(t["k"].astype(F32) * t["ekd"]).astype(BF16) for t in inst]
    for pos in range(n_sub):
        cur = [i for i, t in enumerate(inst) if t["sub"] == (pos if t["d"] == 0 else n_sub - 1 - pos)]
        wqs = [jnp.dot(wq_lhs[i], s_ref[inst[i]["d"], inst[i]["h"]].astype(BF16), preferred_element_type=F32)
               for i in cur]
        v_news = [(uws[i][:, :DN_HEAD_DIM] - wq[:c]).astype(BF16) for i, wq in zip(cur, wqs)]
        outs = [wq[c:] + jnp.dot(qks[i], vn, preferred_element_type=F32) for i, wq, vn in zip(cur, wqs, v_news)]
        upds = [lax.dot_general(k_decs[i], vn, tn, preferred_element_type=F32) for i, vn in zip(cur, v_news)]
        for i, o, upd in zip(cur, outs, upds):
            t = inst[i]
            t["o"][t["rows"], t["hs"]] = o.astype(t["o"].dtype)
            s_ref[t["d"], t["h"]] = s_ref[t["d"], t["h"]] * t["etot"] + upd

    @pl.when(n == n_steps - 1)
    def _():
        sfin_ref[...] = s_ref[...]


def dn_scan(qkvz, bg, s0, *, batch, seq):
    n_sub = min(DN_SUB, seq // DN_CHUNK)
    rows = n_sub * DN_CHUNK
    nc = seq // rows
    fwd = lambda w, col=0: pl.BlockSpec((rows, w), lambda b, n: (b * nc + n, col))
    bwd = lambda w, col=0: pl.BlockSpec((rows, w), lambda b, n: (b * nc + nc - 1 - n, col))
    st = pl.BlockSpec((None, 2, DN_HEADS, DN_HEAD_DIM, DN_HEAD_DIM), lambda b, n: (b, 0, 0, 0, 0))
    o_shape = jax.ShapeDtypeStruct((qkvz.shape[0], DN_DIM), BF16)
    return pl.pallas_call(
        functools.partial(_dn_scan_kernel, n_steps=nc, n_sub=n_sub),
        out_shape=[o_shape, o_shape, jax.ShapeDtypeStruct(s0.shape, F32)],
        grid=(batch, nc),
        in_specs=[fwd(DN_DIM, 0), fwd(DN_DIM, 1), fwd(DN_DIM, 2), fwd(LANES),
                  bwd(DN_DIM, 0), bwd(DN_DIM, 1), bwd(DN_DIM, 2), bwd(LANES), st],
        out_specs=[fwd(DN_DIM), bwd(DN_DIM), st],
        scratch_shapes=[pltpu.VMEM((2, DN_HEADS, DN_HEAD_DIM, DN_HEAD_DIM), F32)],
        compiler_params=_cparams(2),
        name="dn_scan",
    )(qkvz, qkvz, qkvz, bg, qkvz, qkvz, qkvz, bg, s0)


def _dn_readout_kernel(of_ref, ob_ref, z_ref, res_ref, mod_ref, nw_ref, w_ref, o_ref, y_ref):
    gt = mod_ref[2:3, :]
    for h in range(DN_HEADS):
        hs = slice(h * DN_HEAD_DIM, (h + 1) * DN_HEAD_DIM)
        o = of_ref[:, hs].astype(F32) + ob_ref[:, hs].astype(F32)
        ms = jnp.mean(o * o, axis=-1, keepdims=True)
        y = o * lax.rsqrt(ms + EPS) * nw_ref[...]
        y_ref[:, hs] = (y * jax.nn.silu(z_ref[:, hs].astype(F32))).astype(BF16)
    o_ref[...] = res_ref[...] + gt * jnp.dot(y_ref[...], w_ref[...], preferred_element_type=F32)


def dn_readout(o_f, o_b, qkvz, res, mod, nw, w, *, seq, tm):
    n, d = res.shape
    tps = seq // tm
    row = lambda w_, col=0: pl.BlockSpec((tm, w_), lambda i: (i, col))
    return pl.pallas_call(
        _dn_readout_kernel,
        out_shape=jax.ShapeDtypeStruct((n, d), F32),
        grid=(n // tm,),
        in_specs=[row(DN_DIM), row(DN_DIM), row(DN_DIM, 3), row(d), _mod_spec(tps),
                  _resident(nw.shape), _resident(w.shape)],
        out_specs=row(d),
        scratch_shapes=[pltpu.VMEM((tm, DN_DIM), BF16)],
        compiler_params=_cparams(),
        name="dn_readout",
    )(o_f, o_b, qkvz, res, mod, nw, w)


def _pool_kernel(xp_ref, x_ref, xn_ref, mod_ref, nw_ref, wg_ref, scale_ref, o_ref, hin_ref, d_ref, y_ref,
                 *, tm, tiles_per_seq, seq):
    sh, sc, gt = mod_ref[0:1, :], mod_ref[1:2, :], mod_ref[2:3, :]
    hb = POOL_HALO
    pos = lax.rem(pl.program_id(0), tiles_per_seq)
    hp = jnp.where(pos == 0, 0.0, _rms_mod(xp_ref[...], nw_ref[...], sc, sh))
    hn = jnp.where(pos == tiles_per_seq - 1, 0.0, _rms_mod(xn_ref[...], nw_ref[...], sc, sh))
    hm = _rms_mod(x_ref[...], nw_ref[...], sc, sh)
    n_slab = D_MODEL // LANES
    for j in range(n_slab):
        lanes = slice(j * LANES, (j + 1) * LANES)
        hin_ref[j, 0:hb, :] = hp[:, lanes]
        hin_ref[j, hb:hb + tm, :] = hm[:, lanes]
        hin_ref[j, hb + tm:hb + tm + hb, :] = hn[:, lanes]
    rows = tm // POOL_STRIDE
    slabs_per_group = POOL_GROUP // LANES
    ti = pos * tm + POOL_STRIDE * lax.broadcasted_iota(jnp.int32, (rows, 1), 0)
    for gi, win in enumerate(POOL_WINDOWS):
        half = win // 2
        for v in range(POOL_STRIDE):
            t = ti + v
            cnt = (jnp.minimum(t + (win - half), seq) - jnp.maximum(t - half, 0)).astype(F32)
            for j in range(gi * slabs_per_group, (gi + 1) * slabs_per_group):
                s = None
                for dlt in range(-half, win - half):
                    term = hin_ref[j, pl.ds(hb + v + dlt, rows, stride=POOL_STRIDE), :]
                    s = term if s is None else s + term
                d_ref[j, pl.ds(v, rows, stride=POOL_STRIDE), :] = (
                    s / cnt - hin_ref[j, pl.ds(hb + v, rows, stride=POOL_STRIDE), :])
        cols = slice(gi * POOL_GROUP, (gi + 1) * POOL_GROUP)
        dd = jnp.concatenate([d_ref[j] for j in range(gi * slabs_per_group, (gi + 1) * slabs_per_group)], axis=1)
        y_ref[:, cols] = jnp.dot(dd.astype(BF16), wg_ref[gi], preferred_element_type=F32)
    o_ref[...] = x_ref[...] + gt * (y_ref[...] * scale_ref[...])


def pool_mixer(h, mod, nw, wg, scale, *, seq, tm):
    n, d = h.shape
    n_tiles = n // tm
    tps = seq // tm
    prev, main, nxt = _halo_specs(tm, POOL_HALO, n_tiles, d)
    return pl.pallas_call(
        functools.partial(_pool_kernel, tm=tm, tiles_per_seq=tps, seq=seq),
        out_shape=jax.ShapeDtypeStruct((n, d), F32),
        grid=(n_tiles,),
        in_specs=[prev, main, nxt, _mod_spec(tps), _resident((1, d)), _resident(wg.shape),
                  _resident(scale.shape)],
        out_specs=pl.BlockSpec((tm, d), lambda i: (i, 0)),
        scratch_shapes=[pltpu.VMEM((d // LANES, tm + 2 * POOL_HALO, LANES), F32),
                        pltpu.VMEM((d // LANES, tm, LANES), F32), pltpu.VMEM((tm, d), F32)],
        compiler_params=_cparams(),
        name="pool_mixer",
    )(h, h, h, mod, nw, wg, scale)


SWA_QKV_COLS = SWA_DIM + 4 * SWA_KV_DIM
SWA_NORM_COLS = SWA_DIM + 2 * SWA_KV_DIM
SWA_PC = 256
ROPE_QUARTER = SWA_HEAD_DIM // 4


def _swa_qkv_kernel(x_ref, mod_ref, nw_ref, w_ref, hnw_ref, seg_ref, cos_ref, sin_ref,
                    q_ref, k_ref, v_ref, *, rope):
    sh, sc = mod_ref[0:1, :], mod_ref[1:2, :]
    hin = _rms_mod(x_ref[...], nw_ref[...], sc, sh).astype(BF16)
    pc = SWA_PC
    n_chunks = SWA_QKV_COLS // pc
    q_chunks = SWA_DIM // pc
    k_chunks = 2 * SWA_KV_DIM // pc
    for j in range(n_chunks):
        cols = slice(j * pc, (j + 1) * pc)
        p = jnp.dot(hin, w_ref[:, cols], preferred_element_type=F32)
        if j < q_chunks + k_chunks:
            ms = jnp.dot((p * p).astype(BF16), seg_ref[...], preferred_element_type=F32)
            p = p * lax.rsqrt(ms + EPS) * hnw_ref[:, cols]
            if rope:
                lane = lax.broadcasted_iota(jnp.int32, (p.shape[0], LANES), 1)
                up = lax.rem(lane, 2 * ROPE_QUARTER) < ROPE_QUARTER
                halves = []
                for t in range(pc // LANES):
                    ph = p[:, t * LANES:(t + 1) * LANES]
                    partner = jnp.where(up, pltpu.roll(ph, LANES - ROPE_QUARTER, 1),
                                        pltpu.roll(ph, ROPE_QUARTER, 1))
                    halves.append(ph * cos_ref[...] + partner * sin_ref[...])
                p = jnp.concatenate(halves, axis=1)
        if j < q_chunks:
            q_ref[:, cols] = (p * (SWA_HEAD_DIM ** -0.5)).astype(BF16)
        elif j < q_chunks + k_chunks:
            k_ref[:, (j - q_chunks) * pc:(j - q_chunks + 1) * pc] = p.astype(BF16)
        else:
            jj = j - q_chunks - k_chunks
            v_ref[:, jj * pc:(jj + 1) * pc] = p.astype(BF16)


def swa_qkv(h, mod, nw, w, hnw, seg, cos, sin, *, seq, tm, rope):
    n, d = h.shape
    tps = seq // tm
    row = lambda w_: pl.BlockSpec((tm, w_), lambda i: (i, 0))
    tab = pl.BlockSpec((tm, LANES), lambda i: (i % tps, 0))
    kvw = 2 * SWA_KV_DIM
    return pl.pallas_call(
        functools.partial(_swa_qkv_kernel, rope=rope),
        out_shape=[jax.ShapeDtypeStruct((n, SWA_DIM), BF16), jax.ShapeDtypeStruct((n, kvw), BF16),
                   jax.ShapeDtypeStruct((n, kvw), BF16)],
        grid=(n // tm,),
        in_specs=[row(d), _mod_spec(tps), _resident((1, d)), _resident(w.shape), _resident(hnw.shape),
                  _resident(seg.shape), tab, tab],
        out_specs=[row(SWA_DIM), row(kvw), row(kvw)],
        compiler_params=_cparams(),
        name="swa_qkv",
    )(h, mod, nw, w, hnw, seg, cos, sin)


def _swa_attn_kernel(q_ref, kp_ref, kc_ref, kn_ref, vp_ref, vc_ref, vn_ref, kx_ref, vx_ref, sink_ref,
                     o_ref, *, blocks_per_seq):
    i = pl.program_id(1)
    blk = SWA_BLOCK
    ri = lax.broadcasted_iota(jnp.int32, (SWA_GROUP * blk, blk), 0) % blk
    ci = lax.broadcasted_iota(jnp.int32, (SWA_GROUP * blk, blk), 1)
    lane = lax.broadcasted_iota(jnp.int32, (blk, LANES), 1)
    low = lane < SWA_HEAD_DIM
    nt = (((1,), (1,)), ((), ()))
    lane_tiles = lambda a: [a[:, t * LANES:(t + 1) * LANES] for t in range(a.shape[1] // LANES)]
    inst = []
    for sub in range(SWA_SUB):
        rows = slice(sub * blk, (sub + 1) * blk)
        prev = (kp_ref, vp_ref, slice(0, blk)) if sub == 0 else (kc_ref, vc_ref, slice((sub - 1) * blk, sub * blk))
        nxt = ((kn_ref, vn_ref, slice(0, blk)) if sub == SWA_SUB - 1
               else (kc_ref, vc_ref, slice((sub + 1) * blk, (sub + 2) * blk)))
        gblock = i * SWA_SUB + sub
        ok_prev = jnp.logical_and(ci >= ri, gblock > 0)
        ok_next = jnp.logical_and(ci <= ri, gblock < blocks_per_seq - 1)
        for g in range(SWA_KV_HEADS):
            inst.append(dict(g=g, rows=rows, gs=slice(g * LANES, (g + 1) * LANES), prev=prev, nxt=nxt,
                             ok_prev=ok_prev, ok_next=ok_next))

    def stacked_queries(t):
        qs = []
        for half in range(2):
            qt = q_ref[t["rows"], (2 * t["g"] + half) * LANES:(2 * t["g"] + half + 1) * LANES]
            qs.append(jnp.where(low, qt, jnp.zeros_like(qt)))
            qs.append(jnp.where(low, jnp.zeros_like(qt), qt))
        return jnp.concatenate(qs, axis=0)

    def sink_column(g):
        return jnp.concatenate(
            [jnp.broadcast_to(sink_ref[SWA_GROUP * g + hh:SWA_GROUP * g + hh + 1, 0:1], (blk, 1))
             for hh in range(SWA_GROUP)], axis=0)

    def window(t, which, cur_ref, ctx_ref):
        p_ref, n_ref = t["prev"][which], t["nxt"][which]
        return jnp.concatenate([p_ref[t["prev"][2], t["gs"]], cur_ref[t["rows"], t["gs"]],
                                n_ref[t["nxt"][2], t["gs"]], ctx_ref[:, t["gs"]]], axis=0)

    qst = [stacked_queries(t) for t in inst]
    sinks = [sink_column(t["g"]) for t in inst]
    scores = [lax.dot_general(q, window(t, 0, kc_ref, kx_ref), nt, preferred_element_type=F32)
              for q, t in zip(qst, inst)]
    pieces = []
    for s, t in zip(scores, inst):
        ts = lane_tiles(s)
        pieces.append([jnp.where(t["ok_prev"], ts[0], NEG_INF), ts[1], jnp.where(t["ok_next"], ts[2], NEG_INF)]
                      + ts[3:])
    ms = [jnp.maximum(jnp.max(functools.reduce(jnp.maximum, ps), axis=-1, keepdims=True), sk)
          for ps, sk in zip(pieces, sinks)]
    es = [[jnp.exp(p - m) for p in ps] for ps, m in zip(pieces, ms)]
    invs = [1.0 / (jnp.sum(functools.reduce(jnp.add, e), axis=-1, keepdims=True) + jnp.exp(sk - m))
            for e, sk, m in zip(es, sinks, ms)]
    pvs = [jnp.dot(jnp.concatenate(e, axis=1).astype(BF16), window(t, 1, vc_ref, vx_ref),
                   preferred_element_type=F32) * inv
           for e, t, inv in zip(es, inst, invs)]
    for t, pv in zip(inst, pvs):
        for half in range(2):
            a = pv[(2 * half) * blk:(2 * half + 1) * blk]
            b = pv[(2 * half + 1) * blk:(2 * half + 2) * blk]
            o_ref[t["rows"], (2 * t["g"] + half) * LANES:(2 * t["g"] + half + 1) * LANES] = (
                jnp.where(low, a, b).astype(BF16))


def swa_attention(q, k2, v2, kx2, vx2, sink, *, batch, seq, ctx_len):
    nb = seq // SWA_BLOCK
    ns = nb // SWA_SUB
    kvw = 2 * SWA_KV_DIM
    qspec = pl.BlockSpec((SWA_SUB * SWA_BLOCK, SWA_DIM), lambda b, i: (b * ns + i, 0))
    cur = pl.BlockSpec((SWA_SUB * SWA_BLOCK, kvw), lambda b, i: (b * ns + i, 0))
    halo = lambda off: pl.BlockSpec(
        (SWA_BLOCK, kvw), lambda b, i: (b * nb + jnp.clip(i * SWA_SUB + off, 0, nb - 1), 0))
    xspec = pl.BlockSpec((ctx_len, kvw), lambda b, i: (b, 0))
    return pl.pallas_call(
        functools.partial(_swa_attn_kernel, blocks_per_seq=nb),
        out_shape=jax.ShapeDtypeStruct(q.shape, BF16),
        grid=(batch, ns),
        in_specs=[qspec, halo(-1), cur, halo(SWA_SUB), halo(-1), cur, halo(SWA_SUB), xspec, xspec,
                  _resident(sink.shape)],
        out_specs=qspec,
        compiler_params=_cparams(2),
        name="swa_attention",
    )(q, k2, k2, k2, v2, v2, v2, kx2, vx2, sink)


CNV_RB = 32
CNV_PC = 256
CNV_STRIDE = 4


def _conv_module_kernel(xp_ref, x_ref, xn_ref, mod_ref, nw_ref, w1_ref, dw_ref, lnw_ref, lnb_ref, w2_ref,
                        o_ref, hin_ref, u_ref, c_ref, s_ref, *, tm, tiles_per_seq):
    sh, sc, gt = mod_ref[0:1, :], mod_ref[1:2, :], mod_ref[2:3, :]
    _fill_hin_ext(hin_ref, xp_ref, x_ref, xn_ref, nw_ref[...], sc, sh, tiles_per_seq, HALO, tm)
    d = D_MODEL
    n_slab = d // LANES
    for c in range(d // CNV_PC):
        cols = slice(c * CNV_PC, (c + 1) * CNV_PC)
        gcols = slice(d + c * CNV_PC, d + (c + 1) * CNV_PC)
        a1 = jnp.dot(hin_ref[...], w1_ref[:, cols], preferred_element_type=F32)
        a2 = jnp.dot(hin_ref[...], w1_ref[:, gcols], preferred_element_type=F32)
        u = a1 * jax.nn.sigmoid(a2)
        for t in range(CNV_PC // LANES):
            u_ref[c * (CNV_PC // LANES) + t] = u[:, t * LANES:(t + 1) * LANES]
    pad = CNV_WIDTH // 2
    rows = tm // CNV_STRIDE
    for j in range(n_slab):
        lanes = slice(j * LANES, (j + 1) * LANES)
        for v in range(CNV_STRIDE):
            acc = None
            for k in range(CNV_WIDTH):
                term = dw_ref[k:k + 1, lanes] * u_ref[j, pl.ds(HALO + v + k - pad, rows, stride=CNV_STRIDE), :]
                acc = term if acc is None else acc + term
            c_ref[j, pl.ds(v, rows, stride=CNV_STRIDE), :] = acc
    for rb in range(tm // CNV_RB):
        rsl = slice(rb * CNV_RB, (rb + 1) * CNV_RB)
        xs = [c_ref[j, rsl, :] for j in range(n_slab)]
        tot = xs[0]
        for xj in xs[1:]:
            tot = tot + xj
        mu = jnp.sum(tot, axis=-1, keepdims=True) * (1.0 / d)
        xcs = [xj - mu for xj in xs]
        sq = xcs[0] * xcs[0]
        for xc in xcs[1:]:
            sq = sq + xc * xc
        rstd = lax.rsqrt(jnp.sum(sq, axis=-1, keepdims=True) * (1.0 / d) + EPS)
        for j, xc in enumerate(xcs):
            lanes = slice(j * LANES, (j + 1) * LANES)
            y = xc * rstd * lnw_ref[:, lanes] + lnb_ref[:, lanes]
            s_ref[rsl, lanes] = jax.nn.silu(y).astype(BF16)
    o_ref[...] = x_ref[...] + gt * jnp.dot(s_ref[...], w2_ref[...], preferred_element_type=F32)


def conv_module(h, mod, nw, w1, dw, lnw, lnb, w2, *, seq, tm):
    n, d = h.shape
    n_tiles = n // tm
    tps = seq // tm
    prev, main, nxt = _halo_specs(tm, HALO, n_tiles, d)
    return pl.pallas_call(
        functools.partial(_conv_module_kernel, tm=tm, tiles_per_seq=tps),
        out_shape=jax.ShapeDtypeStruct((n, d), F32),
        grid=(n_tiles,),
        in_specs=[prev, main, nxt, _mod_spec(tps), _resident((1, d)), _resident(w1.shape),
                  _resident(dw.shape), _resident(lnw.shape), _resident(lnb.shape), _resident(w2.shape)],
        out_specs=pl.BlockSpec((tm, d), lambda i: (i, 0)),
        scratch_shapes=[pltpu.VMEM((tm + 2 * HALO, d), BF16),
                        pltpu.VMEM((d // LANES, tm + 2 * HALO, LANES), F32),
                        pltpu.VMEM((d // LANES, tm, LANES), F32),
                        pltpu.VMEM((tm, d), BF16)],
        compiler_params=_cparams(),
        name="conv_module",
    )(h, h, h, mod, nw, w1, dw, lnw, lnb, w2)


def _rope_tables(seq):
    n_freq = SWA_HEAD_DIM // 4
    inv_freq = ROPE_BASE ** (-jnp.arange(n_freq, dtype=F32) / n_freq)
    pos = jnp.arange(seq)
    rows = (pos // GRID_W).astype(F32)
    cols = (pos % GRID_W).astype(F32)
    ang_r = rows[:, None] * inv_freq
    ang_c = cols[:, None] * inv_freq
    ang = jnp.concatenate([ang_r, ang_r, ang_c, ang_c], axis=-1)
    sign = jnp.where((jnp.arange(SWA_HEAD_DIM) % (2 * ROPE_QUARTER)) < ROPE_QUARTER, -1.0, 1.0)
    cos = jnp.tile(jnp.cos(ang), (1, LANES // SWA_HEAD_DIM))
    sin = jnp.tile(jnp.sin(ang) * sign, (1, LANES // SWA_HEAD_DIM))
    return cos, sin


TILE_ROWS = 1024


def _tile_rows(seq):
    return min(seq, TILE_ROWS)


def kernel(x, c, ctx, c_ctx, ada_w, ada_b, norm_mix_w, norm_ffn_w, dn_w_in, dn_conv, dn_a_log, dn_dt_bias,
           dn_norm_w, dn_w_out, pool_w_grp, pool_scale, swa_w_qkv, swa_q_norm, swa_k_norm, swa_sink,
           swa_w_out, cnv_w_pw1, cnv_dw, cnv_ln_w, cnv_ln_b, cnv_w_pw2, ffn_w_gate, ffn_w_up, ffn_conv,
           ffn_conv_b, ffn_w_down):
    bsz, seq, d = x.shape
    ctx_len = ctx.shape[1]
    depth = ada_w.shape[0]
    h = x.reshape(bsz * seq, d)
    hc = ctx.reshape(bsz * ctx_len, d)
    p = dict(norm_mix_w=norm_mix_w, norm_ffn_w=norm_ffn_w, dn_w_in=dn_w_in, dn_conv=dn_conv,
             dn_a_log=dn_a_log, dn_dt_bias=dn_dt_bias, dn_norm_w=dn_norm_w, dn_w_out=dn_w_out,
             pool_w_grp=pool_w_grp, pool_scale=pool_scale, swa_w_qkv=swa_w_qkv, swa_q_norm=swa_q_norm,
             swa_k_norm=swa_k_norm, swa_sink=swa_sink, swa_w_out=swa_w_out, cnv_w_pw1=cnv_w_pw1,
             cnv_dw=cnv_dw, cnv_ln_w=cnv_ln_w, cnv_ln_b=cnv_ln_b, cnv_w_pw2=cnv_w_pw2,
             ffn_w_gate=ffn_w_gate, ffn_w_up=ffn_w_up, ffn_conv=ffn_conv, ffn_conv_b=ffn_conv_b,
             ffn_w_down=ffn_w_down)
    dims = dict(bsz=bsz, seq=seq, ctx_len=ctx_len, depth=depth)
    mods = modulation(c, c_ctx, ada_w, ada_b)
    for i in range(depth):
        mod_l, mod_c = layer_modulation(mods, i, bsz)
        h, hc = mixer_layer(i, h, hc, mod_l, mod_c, p, dims)
        h, hc = ffn_layer(i, h, hc, mod_l, mod_c, p, dims)
    return h.reshape(bsz, seq, d)


def modulation(c, c_ctx, ada_w, ada_b):
    bsz, d = c.shape
    n_cond = -(-(bsz + 1) // SUBLANES) * SUBLANES
    cond = jnp.zeros((n_cond, d), F32).at[:bsz].set(c).at[bsz].set(c_ctx)
    return ada_modulation(cond, ada_w, ada_b)


def layer_modulation(mods, i, bsz):
    d = mods.shape[-1] // 6
    mod_l = mods[i, :bsz].reshape(bsz, 6, d)
    mod_c = jnp.broadcast_to(mods[i, bsz].reshape(1, 6, d), (bsz, 6, d))
    return mod_l, mod_c


def _keeps_ctx(i, depth):
    return any((j % N_MIXERS) in CTX_READING_MIXERS for j in range(i + 1, depth))


def ffn_layer(i, h, hc, mod_l, mod_c, p, dims):
    d = h.shape[-1]
    seq, ctx_len = dims["seq"], dims["ctx_len"]
    ffn_args = (p["norm_ffn_w"][i].reshape(1, d), p["ffn_w_gate"].astype(BF16),
                p["ffn_w_up"].astype(BF16), p["ffn_conv"][i], p["ffn_conv_b"][i].reshape(1, FFN_DIM),
                p["ffn_w_down"].astype(BF16))
    h = conv_ffn(h, mod_l, *ffn_args, layer=i, seq=seq, tm=_tile_rows(seq))
    if _keeps_ctx(i, dims["depth"]):
        hc = conv_ffn(hc, mod_c, *ffn_args, layer=i, seq=ctx_len, tm=_tile_rows(ctx_len))
    return h, hc


def mixer_layer(i, h, hc, mod_l, mod_c, p, dims):
    d = h.shape[-1]
    bsz, seq, ctx_len = dims["bsz"], dims["seq"], dims["ctx_len"]
    tm_c = _tile_rows(ctx_len)
    kind, slot = i % N_MIXERS, i // N_MIXERS
    keep_ctx = _keeps_ctx(i, dims["depth"])
    nw_mix = p["norm_mix_w"][i].reshape(1, d)
    lat = dict(seq=seq, tm=_tile_rows(seq))
    cx = dict(seq=ctx_len, tm=tm_c)

    if kind == MIX_DELTANET:
        w_in = p["dn_w_in"][slot]
        wqkv = w_in[:, :3 * DN_DIM].astype(BF16)
        wz = w_in[:, 3 * DN_DIM:4 * DN_DIM].astype(BF16)
        wgt = jnp.zeros((d, LANES), F32).at[:, :4 * DN_HEADS].set(w_in[:, 4 * DN_DIM:]).astype(BF16)
        aneg = jnp.zeros((1, LANES), F32).at[0, 2 * DN_HEADS:4 * DN_HEADS].set(
            -jnp.exp(p["dn_a_log"][slot].astype(F32)).reshape(-1))
        dtb = jnp.zeros((1, LANES), F32).at[0, 2 * DN_HEADS:4 * DN_HEADS].set(
            p["dn_dt_bias"][slot].astype(F32).reshape(-1))
        hnw = p["dn_norm_w"][slot].reshape(1, DN_HEAD_DIM)
        wout = p["dn_w_out"][slot].astype(BF16)
        proj = functools.partial(dn_project, nw=nw_mix, wqkv=wqkv, cw=p["dn_conv"][slot], wz=wz, wgt=wgt,
                                 aneg=aneg, dtb=dtb)
        pc, bgc = proj(hc, mod_c, **cx)
        pl_, bgl = proj(h, mod_l, **lat)
        s0 = jnp.zeros((bsz, 2, DN_HEADS, DN_HEAD_DIM, DN_HEAD_DIM), F32)
        ocf, ocb, s_ctx = dn_scan(pc, bgc, s0, batch=bsz, seq=ctx_len)
        olf, olb, _ = dn_scan(pl_, bgl, s_ctx, batch=bsz, seq=seq)
        h = dn_readout(olf, olb, pl_, h, mod_l, hnw, wout, **lat)
        if keep_ctx:
            hc = dn_readout(ocf, ocb, pc, hc, mod_c, hnw, wout, **cx)
    elif kind == MIX_POOL:
        wg = p["pool_w_grp"][slot].astype(BF16)
        scale = p["pool_scale"][slot].reshape(1, d)
        h = pool_mixer(h, mod_l, nw_mix, wg, scale, **lat)
        if keep_ctx:
            hc = pool_mixer(hc, mod_c, nw_mix, wg, scale, **cx)
    elif kind == MIX_SWA:
        assert not keep_ctx
        wq = p["swa_w_qkv"][slot]
        dup = lambda w_: jnp.repeat(w_.reshape(d, SWA_KV_HEADS, 1, SWA_HEAD_DIM), 2, axis=2).reshape(
            d, 2 * SWA_KV_DIM)
        w_all = jnp.concatenate([wq[:, :SWA_DIM], dup(wq[:, SWA_DIM:SWA_DIM + SWA_KV_DIM]),
                                 dup(wq[:, SWA_DIM + SWA_KV_DIM:])], axis=1).astype(BF16)
        hnw = jnp.concatenate([jnp.tile(p["swa_q_norm"][slot], SWA_HEADS),
                               jnp.tile(p["swa_k_norm"][slot], 2 * SWA_KV_HEADS)]).reshape(1, SWA_NORM_COLS)
        li = jnp.arange(SWA_PC)
        seg = jnp.where((li[:, None] // SWA_HEAD_DIM) == (li[None, :] // SWA_HEAD_DIM),
                        1.0 / SWA_HEAD_DIM, 0.0).astype(BF16)
        cos, sin = _rope_tables(seq)
        sink = jnp.broadcast_to(p["swa_sink"][slot].astype(F32).reshape(SWA_HEADS, 1), (SWA_HEADS, LANES))
        ql, k2, v2 = swa_qkv(h, mod_l, nw_mix, w_all, hnw, seg, cos, sin, rope=True, **lat)
        _, kx2, vx2 = swa_qkv(hc, mod_c, nw_mix, w_all, hnw, seg, cos[:tm_c], sin[:tm_c], rope=False, **cx)
        o = swa_attention(ql, k2, v2, kx2, vx2, sink, batch=bsz, seq=seq, ctx_len=ctx_len)
        h = matmul_residual(o, h, mod_l, p["swa_w_out"][slot].astype(BF16), **lat)
    else:
        args = (nw_mix, p["cnv_w_pw1"][slot].astype(BF16), p["cnv_dw"][slot], p["cnv_ln_w"][slot].reshape(1, d),
                p["cnv_ln_b"][slot].reshape(1, d), p["cnv_w_pw2"][slot].astype(BF16))
        h = conv_module(h, mod_l, *args, **lat)
        if keep_ctx:
            hc = conv_module(hc, mod_c, *args, **cx)
    return h, hc
```

```python
import functools

import jax
import jax.numpy as jnp
from jax import lax
from jax.experimental import pallas as pl
from jax.experimental.pallas import tpu as pltpu

D_MODEL = 1024
EPS = 1e-6
NEG_INF = -1e30
GRID_W = 64
ROPE_BASE = 10000.0
N_MIXERS = 4
MIX_DELTANET, MIX_POOL, MIX_SWA, MIX_CONV = 0, 1, 2, 3
CTX_READING_MIXERS = (MIX_DELTANET, MIX_SWA)

DN_HEADS = 8
DN_HEAD_DIM = 128
DN_DIM = DN_HEADS * DN_HEAD_DIM
DN_CONV = 5
DN_CHUNK = 128
DN_INV_BASE = 2
DN_SUB = 4

POOL_WINDOWS = (2, 4, 8, 16)
POOL_GROUP = D_MODEL // len(POOL_WINDOWS)

SWA_HEADS = 16
SWA_KV_HEADS = 4
SWA_HEAD_DIM = 64
SWA_GROUP = SWA_HEADS // SWA_KV_HEADS
SWA_DIM = SWA_HEADS * SWA_HEAD_DIM
SWA_KV_DIM = SWA_KV_HEADS * SWA_HEAD_DIM
SWA_WINDOW = 128
SWA_BLOCK = 128
SWA_SUB = 4

CNV_WIDTH = 31
FFN_DIM = 2816
FFN_CONV = 3

LANES = 128
SUBLANES = 8
VMEM_LIMIT_BYTES = 56 * 1024 * 1024

F32 = jnp.float32
BF16 = jnp.bfloat16
HALO = 2 * SUBLANES
POOL_HALO = SUBLANES
POOL_STRIDE = 4


def _cparams(n_axes=1):
    return pltpu.CompilerParams(
        dimension_semantics=("arbitrary",) * n_axes,
        vmem_limit_bytes=VMEM_LIMIT_BYTES,
    )


def _resident(shape):
    nd = len(shape)
    return pl.BlockSpec(shape, lambda *_: (0,) * nd, pipeline_mode=pl.Buffered(1))


def _resident_layer(shape, layer):
    return pl.BlockSpec((None,) + tuple(shape[1:]), lambda *_: (layer,) + (0,) * (len(shape) - 1),
                        pipeline_mode=pl.Buffered(1))


def _halo_specs(tm, hb, n_tiles, width):
    r = tm // hb
    last = n_tiles * r - 1
    prev = pl.BlockSpec((hb, width), lambda i: (jnp.maximum(i * r - 1, 0), 0))
    main = pl.BlockSpec((tm, width), lambda i: (i, 0))
    nxt = pl.BlockSpec((hb, width), lambda i: (jnp.minimum((i + 1) * r, last), 0))
    return prev, main, nxt


def _mod_spec(tiles_per_seq):
    return pl.BlockSpec((None, 6, D_MODEL), lambda i: (i // tiles_per_seq, 0, 0))


def _rms_mod(x, nw, sc, sh):
    ms = jnp.mean(x * x, axis=-1, keepdims=True)
    return (x * lax.rsqrt(ms + EPS) * nw) * (1.0 + sc) + sh


def _fill_hin_ext(hin_ref, xp_ref, x_ref, xn_ref, nw, sc, sh, tiles_per_seq, hb, tm):
    i = pl.program_id(0)
    pos = lax.rem(i, tiles_per_seq)
    first = pos == 0
    last = pos == tiles_per_seq - 1
    hp = jnp.where(first, 0.0, _rms_mod(xp_ref[...], nw, sc, sh))
    hn = jnp.where(last, 0.0, _rms_mod(xn_ref[...], nw, sc, sh))
    hin_ref[0:hb, :] = hp.astype(hin_ref.dtype)
    hin_ref[hb:hb + tm, :] = _rms_mod(x_ref[...], nw, sc, sh).astype(hin_ref.dtype)
    hin_ref[hb + tm:hb + tm + hb, :] = hn.astype(hin_ref.dtype)


def _shift_rows(a, d, n_rows):
    if d == 0:
        return a
    return pltpu.roll(a, (-d) % n_rows, 0)


ADA_TN = 1536


def _ada_kernel(cond_ref, w_ref, b_ref, o_ref):
    s = jax.nn.silu(cond_ref[...])
    o_ref[...] = jnp.dot(s, w_ref[...], preferred_element_type=F32,
                         precision=lax.Precision.HIGHEST) + b_ref[...]


def ada_modulation(cond, ada_w, ada_b):
    depth, d, n = ada_w.shape
    rows = cond.shape[0]
    return pl.pallas_call(
        _ada_kernel,
        out_shape=jax.ShapeDtypeStruct((depth, rows, n), F32),
        grid=(depth, n // ADA_TN),
        in_specs=[
            pl.BlockSpec((rows, d), lambda l, j: (0, 0)),
            pl.BlockSpec((None, d, ADA_TN), lambda l, j: (l, 0, j)),
            pl.BlockSpec((None, 1, ADA_TN), lambda l, j: (l, 0, j)),
        ],
        out_specs=pl.BlockSpec((None, rows, ADA_TN), lambda l, j: (l, 0, j)),
        compiler_params=_cparams(2),
        name="ada_modulation",
    )(cond, ada_w, ada_b.reshape(depth, 1, n))


FFN_FC = 256
FFN_DOWN_GROUP = 6


def _ffn_kernel(xp_ref, x_ref, xn_ref, mod_ref, nw_ref, wg_ref, wu_ref, cw_ref, cb_ref, wd_ref,
                o_ref, hin_ref, act_ref, acc_ref, *, tm, tiles_per_seq):
    sh, sc, gt = mod_ref[3:4, :], mod_ref[4:5, :], mod_ref[5:6, :]
    _fill_hin_ext(hin_ref, xp_ref, x_ref, xn_ref, nw_ref[...], sc, sh, tiles_per_seq, HALO, tm)
    n_ext = tm + 2 * HALO
    n_chunks = FFN_DIM // FFN_FC
    group_start = 0
    for c in range(n_chunks):
        cols = slice(c * FFN_FC, (c + 1) * FFN_FC)
        g_ext = jnp.dot(hin_ref[...], wg_ref[:, cols], preferred_element_type=F32)
        u = jnp.dot(hin_ref[HALO:HALO + tm, :], wu_ref[:, cols], preferred_element_type=F32)
        g = cb_ref[:, cols]
        for k in range(FFN_CONV):
            g = g + cw_ref[k:k + 1, cols] * _shift_rows(g_ext, k - 1, n_ext)[HALO:HALO + tm, :]
        act_ref[:, cols] = (jax.nn.silu(g) * u).astype(BF16)
        if (c + 1) % FFN_DOWN_GROUP == 0 or c == n_chunks - 1:
            ks = slice(group_start * FFN_FC, (c + 1) * FFN_FC)
            y = jnp.dot(act_ref[:, ks], wd_ref[ks, :], preferred_element_type=F32)
            if group_start == 0:
                acc_ref[...] = y
            else:
                acc_ref[...] += y
            group_start = c + 1
    o_ref[...] = x_ref[...] + gt * acc_ref[...]


def conv_ffn(h, mod, nw, wg, wu, cw, cb, wd, *, layer, seq, tm):
    n, d = h.shape
    n_tiles = n // tm
    tps = seq // tm
    prev, main, nxt = _halo_specs(tm, HALO, n_tiles, d)
    return pl.pallas_call(
        functools.partial(_ffn_kernel, tm=tm, tiles_per_seq=tps),
        out_shape=jax.ShapeDtypeStruct((n, d), F32),
        grid=(n_tiles,),
        in_specs=[prev, main, nxt, _mod_spec(tps), _resident((1, d)),
                  _resident_layer(wg.shape, layer), _resident_layer(wu.shape, layer), _resident(cw.shape),
                  _resident(cb.shape), _resident_layer(wd.shape, layer)],
        out_specs=pl.BlockSpec((tm, d), lambda i: (i, 0)),
        scratch_shapes=[pltpu.VMEM((tm + 2 * HALO, d), BF16), pltpu.VMEM((tm, FFN_DIM), BF16),
                        pltpu.VMEM((tm, d), F32)],
        compiler_params=_cparams(),
        name="conv_ffn",
    )(h, h, h, mod, nw, wg, wu, cw, cb, wd)


def _mm_res_kernel(a_ref, res_ref, mod_ref, w_ref, o_ref):
    gt = mod_ref[2:3, :]
    o_ref[...] = res_ref[...] + gt * jnp.dot(a_ref[...], w_ref[...], preferred_element_type=F32)


def matmul_residual(a, res, mod, w, *, seq, tm):
    n, k = a.shape
    d = w.shape[1]
    tps = seq // tm
    return pl.pallas_call(
        _mm_res_kernel,
        out_shape=jax.ShapeDtypeStruct((n, d), F32),
        grid=(n // tm,),
        in_specs=[pl.BlockSpec((tm, k), lambda i: (i, 0)), pl.BlockSpec((tm, d), lambda i: (i, 0)),
                  _mod_spec(tps), _resident(w.shape)],
        out_specs=pl.BlockSpec((tm, d), lambda i: (i, 0)),
        compiler_params=_cparams(),
        name="matmul_residual",
    )(a, res, mod, w)


DN_PC = 256
DN_STRIDE = 4


def _dn_proj_kernel(xp_ref, x_ref, xn_ref, mod_ref, nw_ref, wqkv_ref, cw_ref, wz_ref, wgt_ref,
                    aneg_ref, dtb_ref, q_ref, k_ref, v_ref, z_ref, bg_ref, hin_ref, p_ref, a_ref,
                    *, tm, tiles_per_seq):
    sh, sc = mod_ref[0:1, :], mod_ref[1:2, :]
    _fill_hin_ext(hin_ref, xp_ref, x_ref, xn_ref, nw_ref[...], sc, sh, tiles_per_seq, HALO, tm)
    outs = (q_ref, k_ref, v_ref)
    per_out = DN_DIM // DN_PC
    heads_pc = DN_PC // DN_HEAD_DIM
    rows = tm // DN_STRIDE
    for c in range(3 * per_out):
        cols = slice(c * DN_PC, (c + 1) * DN_PC)
        p_ext = jnp.dot(hin_ref[...], wqkv_ref[:, cols], preferred_element_type=F32)
        which, cc = divmod(c, per_out)
        for hh in range(heads_pc):
            slot = (c % 2) * heads_pc + hh
            lanes = slice(c * DN_PC + hh * DN_HEAD_DIM, c * DN_PC + (hh + 1) * DN_HEAD_DIM)
            p_ref[slot] = p_ext[:, hh * DN_HEAD_DIM:(hh + 1) * DN_HEAD_DIM]
            for v in range(DN_STRIDE):
                acc = None
                for k in range(DN_CONV):
                    start = HALO + v + k - DN_CONV // 2
                    term = cw_ref[k:k + 1, lanes] * p_ref[slot, pl.ds(start, rows, stride=DN_STRIDE), :]
                    acc = term if acc is None else acc + term
                ah = jax.nn.silu(acc)
                if which < 2:
                    ah = ah * lax.rsqrt(jnp.sum(ah * ah, axis=-1, keepdims=True) + EPS)
                    if which == 0:
                        ah = ah * (DN_HEAD_DIM ** -0.5)
                a_ref[slot, pl.ds(v, rows, stride=DN_STRIDE), :] = ah
            lo = cc * DN_PC + hh * DN_HEAD_DIM
            outs[which][:, lo:lo + DN_HEAD_DIM] = a_ref[slot].astype(BF16)
    hm = hin_ref[HALO:HALO + tm, :]
    z_ref[...] = jnp.dot(hm, wz_ref[...], preferred_element_type=F32).astype(BF16)
    gates = jnp.dot(hm, wgt_ref[...], preferred_element_type=F32)
    lane = lax.broadcasted_iota(jnp.int32, gates.shape, 1)
    beta = jax.nn.sigmoid(gates)
    g = aneg_ref[...] * jax.nn.softplus(gates + dtb_ref[...])
    bg_ref[...] = jnp.where(lane < 2 * DN_HEADS, beta, g)


def dn_project(h, mod, nw, wqkv, cw, wz, wgt, aneg, dtb, *, seq, tm):
    n, d = h.shape
    n_tiles = n // tm
    tps = seq // tm
    prev, main, nxt = _halo_specs(tm, HALO, n_tiles, d)
    row = lambda w: pl.BlockSpec((tm, w), lambda i: (i, 0))
    return pl.pallas_call(
        functools.partial(_dn_proj_kernel, tm=tm, tiles_per_seq=tps),
        out_shape=[jax.ShapeDtypeStruct((n, DN_DIM), BF16)] * 4 + [jax.ShapeDtypeStruct((n, LANES), F32)],
        grid=(n_tiles,),
        in_specs=[prev, main, nxt, _mod_spec(tps), _resident((1, d)), _resident(wqkv.shape),
                  _resident(cw.shape), _resident(wz.shape), _resident(wgt.shape),
                  _resident(aneg.shape), _resident(dtb.shape)],
        out_specs=[row(DN_DIM)] * 4 + [row(LANES)],
        scratch_shapes=[pltpu.VMEM((tm + 2 * HALO, d), BF16),
                        pltpu.VMEM((2 * DN_PC // DN_HEAD_DIM, tm + 2 * HALO, DN_HEAD_DIM), F32),
                        pltpu.VMEM((2 * DN_PC // DN_HEAD_DIM, tm, DN_HEAD_DIM), F32)],
        compiler_params=_cparams(),
        name="dn_project",
    )(h, h, h, mod, nw, wqkv, cw, wz, wgt, aneg, dtb)


def _bdot(a, b):
    return jnp.dot(a.astype(BF16), b.astype(BF16), preferred_element_type=F32)


def _exact_mask_dot(mask, x):
    m = mask.astype(BF16)
    hi = x.astype(BF16)
    r1 = x - hi.astype(F32)
    mid = r1.astype(BF16)
    lo = (r1 - mid.astype(F32)).astype(BF16)
    return (jnp.dot(m, hi, preferred_element_type=F32) + jnp.dot(m, mid, preferred_element_type=F32)
            + jnp.dot(m, lo, preferred_element_type=F32))


def _unit_tri_inverses(lmats, lowers, eye, blk):
    c = lmats[0].shape[0]
    xs = [eye - jnp.where(blk[0], l, 0.0) for l in lmats]
    size = DN_INV_BASE
    for lvl in range(1, len(blk)):
        sel = jnp.logical_and(blk[lvl], jnp.logical_not(blk[lvl - 1]))
        if size < SUBLANES:
            ts = [_bdot(jnp.where(sel, l, 0.0), x) for l, x in zip(lmats, xs)]
            xs = [x - _bdot(x, t) for x, t in zip(xs, ts)]
            size *= 2
            continue
        groups = range(c // size)

        def active(a, lower):
            return jnp.concatenate([a[g * size:(g + 1) * size] for g in groups if (g % 2 == 1) == lower], axis=0)

        def expand(a, lower, rest):
            parts, k = [], 0
            for g in groups:
                if (g % 2 == 1) == lower:
                    parts.append(a[k * size:(k + 1) * size])
                    k += 1
                else:
                    parts.append(rest[g * size:(g + 1) * size])
            return jnp.concatenate(parts, axis=0)

        zero = jnp.zeros_like(eye)
        ts = [_bdot(active(jnp.where(sel, l, 0.0), lo), x) for l, lo, x in zip(lmats, lowers, xs)]
        us = [_bdot(active(x, lo), expand(t, lo, zero)) for x, lo, t in zip(xs, lowers, ts)]
        xs = [expand(active(x, lo) - u, lo, x) for x, lo, u in zip(xs, lowers, us)]
        size *= 2
    return xs


def _dn_scan_kernel(qf_ref, kf_ref, vf_ref, bgf_ref, qb_ref, kb_ref, vb_ref, bgb_ref, s0_ref,
                    of_ref, ob_ref, sfin_ref, s_ref, *, n_steps, n_sub):
    n = pl.program_id(1)
    c = DN_CHUNK

    @pl.when(n == 0)
    def _():
        s_ref[...] = s0_ref[...]

    ri = lax.broadcasted_iota(jnp.int32, (c, c), 0)
    ci = lax.broadcasted_iota(jnp.int32, (c, c), 1)
    eye = (ri == ci).astype(F32)
    blk = []
    bs = DN_INV_BASE
    while bs <= c:
        blk.append((ri // bs) == (ci // bs))
        bs *= 2

    dirs = ((qf_ref, kf_ref, vf_ref, bgf_ref, of_ref, ri >= ci, ri > ci),
            (qb_ref, kb_ref, vb_ref, bgb_ref, ob_ref, ri <= ci, ri < ci))
    inst = []
    for d, (q_ref, k_ref, v_ref, bg_ref, o_ref, incl, strict) in enumerate(dirs):
        for sub in range(n_sub):
            rows = slice(sub * c, (sub + 1) * c)
            bg = bg_ref[rows, :]
            is_g = lax.broadcasted_iota(jnp.int32, bg.shape, 1) >= 2 * DN_HEADS
            bgg = jnp.where(is_g, bg, 0.0)
            cum = _exact_mask_dot(incl, bgg)
            cum_t = cum.T
            tot = jnp.sum(bgg, axis=0, keepdims=True)
            egc = jnp.exp(cum)
            ekd = jnp.exp(tot - cum)
            etot = jnp.exp(tot)
            for h in range(DN_HEADS):
                lb = d * DN_HEADS + h
                lg = 2 * DN_HEADS + lb
                hs = slice(h * DN_HEAD_DIM, (h + 1) * DN_HEAD_DIM)
                inst.append(dict(
                    d=d, h=h, sub=sub, o=o_ref, rows=rows, hs=hs, incl=incl, strict=strict,
                    q=q_ref[rows, hs], k=k_ref[rows, hs], v=v_ref[rows, hs],
                    beta=bg[:, lb:lb + 1], gc=cum[:, lg:lg + 1], gc_row=cum_t[lg:lg + 1, :],
                    egc=egc[:, lg:lg + 1], ekd=ekd[:, lg:lg + 1], etot=etot[:, lg:lg + 1]))
    nt = (((1,), (1,)), ((), ()))
    tn = (((0,), (0,)), ((), ()))
    kq = [lax.dot_general(jnp.concatenate([t["k"], t["q"]], axis=0), t["k"], nt, preferred_element_type=F32)
          for t in inst]
    decays = [jnp.where(t["incl"], jnp.exp(jnp.where(t["incl"], t["gc"] - t["gc_row"], 0.0)), 0.0) for t in inst]
    lmats = [jnp.where(t["strict"], t["beta"] * r[:c] * dc, 0.0) for t, r, dc in zip(inst, kq, decays)]
    qks = [(r[c:] * dc).astype(BF16) for r, dc in zip(kq, decays)]
    tinvs = _unit_tri_inverses(lmats, [t["d"] == 0 for t in inst], eye, blk)
    rhss = [jnp.concatenate([t["v"].astype(F32) * t["beta"], t["k"].astype(F32) * (t["beta"] * t["egc"])], axis=1)
            for t in inst]
    uws = [_bdot(ti, r) for ti, r in zip(tinvs, rhss)]
    wq_lhs = [jnp.concatenate([uw[:, DN_HEAD_DIM:], t["q"].astype(F32) * t["egc"]], axis=0).astype(BF16)
              for t, uw in zip(inst, uws)]
    k_decs = [(t["k"].astype(F32) * t["ekd"]).astype(BF16) for t in inst]
    for pos in range(n_sub):
        cur = [i for i, t in enumerate(inst) if t["sub"] == (pos if t["d"] == 0 else n_sub - 1 - pos)]
        wqs = [jnp.dot(wq_lhs[i], s_ref[inst[i]["d"], inst[i]["h"]].astype(BF16), preferred_element_type=F32)
               for i in cur]
        v_news = [(uws[i][:, :DN_HEAD_DIM] - wq[:c]).astype(BF16) for i, wq in zip(cur, wqs)]
        outs = [wq[c:] + jnp.dot(qks[i], vn, preferred_element_type=F32) for i, wq, vn in zip(cur, wqs, v_news)]
        upds = [lax.dot_general(k_decs[i], vn, tn, preferred_element_type=F32) for i, vn in zip(cur, v_news)]
        for i, o, upd in zip(cur, outs, upds):
            t = inst[i]
            t["o"][t["rows"], t["hs"]] = o.astype(t["o"].dtype)
            s_ref[t["d"], t["h"]] = s_ref[t["d"], t["h"]] * t["etot"] + upd

    @pl.when(n == n_steps - 1)
    def _():
        sfin_ref[...] = s_ref[...]


def dn_scan(q, k, v, bg, s0, *, batch, seq):
    n_sub = min(DN_SUB, seq // DN_CHUNK)
    rows = n_sub * DN_CHUNK
    nc = seq // rows
    fwd = lambda w: pl.BlockSpec((rows, w), lambda b, n: (b * nc + n, 0))
    bwd = lambda w: pl.BlockSpec((rows, w), lambda b, n: (b * nc + nc - 1 - n, 0))
    st = pl.BlockSpec((None, 2, DN_HEADS, DN_HEAD_DIM, DN_HEAD_DIM), lambda b, n: (b, 0, 0, 0, 0))
    return pl.pallas_call(
        functools.partial(_dn_scan_kernel, n_steps=nc, n_sub=n_sub),
        out_shape=[jax.ShapeDtypeStruct(q.shape, BF16), jax.ShapeDtypeStruct(q.shape, BF16),
                   jax.ShapeDtypeStruct(s0.shape, F32)],
        grid=(batch, nc),
        in_specs=[fwd(DN_DIM), fwd(DN_DIM), fwd(DN_DIM), fwd(LANES),
                  bwd(DN_DIM), bwd(DN_DIM), bwd(DN_DIM), bwd(LANES), st],
        out_specs=[fwd(DN_DIM), bwd(DN_DIM), st],
        scratch_shapes=[pltpu.VMEM((2, DN_HEADS, DN_HEAD_DIM, DN_HEAD_DIM), F32)],
        compiler_params=_cparams(2),
        name="dn_scan",
    )(q, k, v, bg, q, k, v, bg, s0)


def _dn_readout_kernel(of_ref, ob_ref, z_ref, res_ref, mod_ref, nw_ref, w_ref, o_ref, y_ref):
    gt = mod_ref[2:3, :]
    for h in range(DN_HEADS):
        hs = slice(h * DN_HEAD_DIM, (h + 1) * DN_HEAD_DIM)
        o = of_ref[:, hs].astype(F32) + ob_ref[:, hs].astype(F32)
        ms = jnp.mean(o * o, axis=-1, keepdims=True)
        y = o * lax.rsqrt(ms + EPS) * nw_ref[...]
        y_ref[:, hs] = (y * jax.nn.silu(z_ref[:, hs].astype(F32))).astype(BF16)
    o_ref[...] = res_ref[...] + gt * jnp.dot(y_ref[...], w_ref[...], preferred_element_type=F32)


def dn_readout(o_f, o_b, z, res, mod, nw, w, *, seq, tm):
    n, d = res.shape
    tps = seq // tm
    row = lambda w_: pl.BlockSpec((tm, w_), lambda i: (i, 0))
    return pl.pallas_call(
        _dn_readout_kernel,
        out_shape=jax.ShapeDtypeStruct((n, d), F32),
        grid=(n // tm,),
        in_specs=[row(DN_DIM), row(DN_DIM), row(DN_DIM), row(d), _mod_spec(tps),
                  _resident(nw.shape), _resident(w.shape)],
        out_specs=row(d),
        scratch_shapes=[pltpu.VMEM((tm, DN_DIM), BF16)],
        compiler_params=_cparams(),
        name="dn_readout",
    )(o_f, o_b, z, res, mod, nw, w)


def _pool_kernel(xp_ref, x_ref, xn_ref, mod_ref, nw_ref, wg_ref, scale_ref, o_ref, hin_ref, d_ref, y_ref,
                 *, tm, tiles_per_seq, seq):
    sh, sc, gt = mod_ref[0:1, :], mod_ref[1:2, :], mod_ref[2:3, :]
    hb = POOL_HALO
    pos = lax.rem(pl.program_id(0), tiles_per_seq)
    hp = jnp.where(pos == 0, 0.0, _rms_mod(xp_ref[...], nw_ref[...], sc, sh))
    hn = jnp.where(pos == tiles_per_seq - 1, 0.0, _rms_mod(xn_ref[...], nw_ref[...], sc, sh))
    hm = _rms_mod(x_ref[...], nw_ref[...], sc, sh)
    n_slab = D_MODEL // LANES
    for j in range(n_slab):
        lanes = slice(j * LANES, (j + 1) * LANES)
        hin_ref[j, 0:hb, :] = hp[:, lanes]
        hin_ref[j, hb:hb + tm, :] = hm[:, lanes]
        hin_ref[j, hb + tm:hb + tm + hb, :] = hn[:, lanes]
    rows = tm // POOL_STRIDE
    slabs_per_group = POOL_GROUP // LANES
    ti = pos * tm + POOL_STRIDE * lax.broadcasted_iota(jnp.int32, (rows, 1), 0)
    for gi, win in enumerate(POOL_WINDOWS):
        half = win // 2
        for v in range(POOL_STRIDE):
            t = ti + v
            cnt = (jnp.minimum(t + (win - half), seq) - jnp.maximum(t - half, 0)).astype(F32)
            for j in range(gi * slabs_per_group, (gi + 1) * slabs_per_group):
                s = None
                for dlt in range(-half, win - half):
                    term = hin_ref[j, pl.ds(hb + v + dlt, rows, stride=POOL_STRIDE), :]
                    s = term if s is None else s + term
                d_ref[j, pl.ds(v, rows, stride=POOL_STRIDE), :] = (
                    s / cnt - hin_ref[j, pl.ds(hb + v, rows, stride=POOL_STRIDE), :])
        cols = slice(gi * POOL_GROUP, (gi + 1) * POOL_GROUP)
        dd = jnp.concatenate([d_ref[j] for j in range(gi * slabs_per_group, (gi + 1) * slabs_per_group)], axis=1)
        y_ref[:, cols] = jnp.dot(dd.astype(BF16), wg_ref[gi], preferred_element_type=F32)
    o_ref[...] = x_ref[...] + gt * (y_ref[...] * scale_ref[...])


def pool_mixer(h, mod, nw, wg, scale, *, seq, tm):
    n, d = h.shape
    n_tiles = n // tm
    tps = seq // tm
    prev, main, nxt = _halo_specs(tm, POOL_HALO, n_tiles, d)
    return pl.pallas_call(
        functools.partial(_pool_kernel, tm=tm, tiles_per_seq=tps, seq=seq),
        out_shape=jax.ShapeDtypeStruct((n, d), F32),
        grid=(n_tiles,),
        in_specs=[prev, main, nxt, _mod_spec(tps), _resident((1, d)), _resident(wg.shape),
                  _resident(scale.shape)],
        out_specs=pl.BlockSpec((tm, d), lambda i: (i, 0)),
        scratch_shapes=[pltpu.VMEM((d // LANES, tm + 2 * POOL_HALO, LANES), F32),
                        pltpu.VMEM((d // LANES, tm, LANES), F32), pltpu.VMEM((tm, d), F32)],
        compiler_params=_cparams(),
        name="pool_mixer",
    )(h, h, h, mod, nw, wg, scale)


SWA_QKV_COLS = SWA_DIM + 4 * SWA_KV_DIM
SWA_NORM_COLS = SWA_DIM + 2 * SWA_KV_DIM
SWA_PC = 512
ROPE_QUARTER = SWA_HEAD_DIM // 4


def _swa_qkv_kernel(x_ref, mod_ref, nw_ref, w_ref, hnw_ref, seg_ref, cos_ref, sin_ref,
                    q_ref, k_ref, v_ref, *, rope):
    sh, sc = mod_ref[0:1, :], mod_ref[1:2, :]
    hin = _rms_mod(x_ref[...], nw_ref[...], sc, sh).astype(BF16)
    pc = SWA_PC
    n_chunks = SWA_QKV_COLS // pc
    q_chunks = SWA_DIM // pc
    k_chunks = 2 * SWA_KV_DIM // pc
    for j in range(n_chunks):
        cols = slice(j * pc, (j + 1) * pc)
        p = jnp.dot(hin, w_ref[:, cols], preferred_element_type=F32)
        if j < q_chunks + k_chunks:
            ms = jnp.dot((p * p).astype(BF16), seg_ref[...], preferred_element_type=F32)
            p = p * lax.rsqrt(ms + EPS) * hnw_ref[:, cols]
            if rope:
                lane = lax.broadcasted_iota(jnp.int32, (p.shape[0], LANES), 1)
                up = lax.rem(lane, 2 * ROPE_QUARTER) < ROPE_QUARTER
                halves = []
                for t in range(pc // LANES):
                    ph = p[:, t * LANES:(t + 1) * LANES]
                    partner = jnp.where(up, pltpu.roll(ph, LANES - ROPE_QUARTER, 1),
                                        pltpu.roll(ph, ROPE_QUARTER, 1))
                    halves.append(ph * cos_ref[...] + partner * sin_ref[...])
                p = jnp.concatenate(halves, axis=1)
        if j < q_chunks:
            q_ref[:, cols] = (p * (SWA_HEAD_DIM ** -0.5)).astype(BF16)
        elif j < q_chunks + k_chunks:
            k_ref[:, (j - q_chunks) * pc:(j - q_chunks + 1) * pc] = p.astype(BF16)
        else:
            jj = j - q_chunks - k_chunks
            v_ref[:, jj * pc:(jj + 1) * pc] = p.astype(BF16)


def swa_qkv(h, mod, nw, w, hnw, seg, cos, sin, *, seq, tm, rope):
    n, d = h.shape
    tps = seq // tm
    row = lambda w_: pl.BlockSpec((tm, w_), lambda i: (i, 0))
    tab = pl.BlockSpec((tm, LANES), lambda i: (i % tps, 0))
    kvw = 2 * SWA_KV_DIM
    return pl.pallas_call(
        functools.partial(_swa_qkv_kernel, rope=rope),
        out_shape=[jax.ShapeDtypeStruct((n, SWA_DIM), BF16), jax.ShapeDtypeStruct((n, kvw), BF16),
                   jax.ShapeDtypeStruct((n, kvw), BF16)],
        grid=(n // tm,),
        in_specs=[row(d), _mod_spec(tps), _resident((1, d)), _resident(w.shape), _resident(hnw.shape),
                  _resident(seg.shape), tab, tab],
        out_specs=[row(SWA_DIM), row(kvw), row(kvw)],
        compiler_params=_cparams(),
        name="swa_qkv",
    )(h, mod, nw, w, hnw, seg, cos, sin)


def _swa_attn_kernel(q_ref, kp_ref, kc_ref, kn_ref, vp_ref, vc_ref, vn_ref, kx_ref, vx_ref, sink_ref,
                     o_ref, *, blocks_per_seq):
    i = pl.program_id(1)
    blk = SWA_BLOCK
    ri = lax.broadcasted_iota(jnp.int32, (SWA_GROUP * blk, blk), 0) % blk
    ci = lax.broadcasted_iota(jnp.int32, (SWA_GROUP * blk, blk), 1)
    lane = lax.broadcasted_iota(jnp.int32, (blk, LANES), 1)
    low = lane < SWA_HEAD_DIM
    nt = (((1,), (1,)), ((), ()))
    lane_tiles = lambda a: [a[:, t * LANES:(t + 1) * LANES] for t in range(a.shape[1] // LANES)]
    inst = []
    for sub in range(SWA_SUB):
        rows = slice(sub * blk, (sub + 1) * blk)
        prev = (kp_ref, vp_ref, slice(0, blk)) if sub == 0 else (kc_ref, vc_ref, slice((sub - 1) * blk, sub * blk))
        nxt = ((kn_ref, vn_ref, slice(0, blk)) if sub == SWA_SUB - 1
               else (kc_ref, vc_ref, slice((sub + 1) * blk, (sub + 2) * blk)))
        gblock = i * SWA_SUB + sub
        ok_prev = jnp.logical_and(ci >= ri, gblock > 0)
        ok_next = jnp.logical_and(ci <= ri, gblock < blocks_per_seq - 1)
        for g in range(SWA_KV_HEADS):
            inst.append(dict(g=g, rows=rows, gs=slice(g * LANES, (g + 1) * LANES), prev=prev, nxt=nxt,
                             ok_prev=ok_prev, ok_next=ok_next))

    def stacked_queries(t):
        qs = []
        for half in range(2):
            qt = q_ref[t["rows"], (2 * t["g"] + half) * LANES:(2 * t["g"] + half + 1) * LANES]
            qs.append(jnp.where(low, qt, jnp.zeros_like(qt)))
            qs.append(jnp.where(low, jnp.zeros_like(qt), qt))
        return jnp.concatenate(qs, axis=0)

    def sink_column(g):
        return jnp.concatenate(
            [jnp.broadcast_to(sink_ref[SWA_GROUP * g + hh:SWA_GROUP * g + hh + 1, 0:1], (blk, 1))
             for hh in range(SWA_GROUP)], axis=0)

    def window(t, which, cur_ref, ctx_ref):
        p_ref, n_ref = t["prev"][which], t["nxt"][which]
        return jnp.concatenate([p_ref[t["prev"][2], t["gs"]], cur_ref[t["rows"], t["gs"]],
                                n_ref[t["nxt"][2], t["gs"]], ctx_ref[:, t["gs"]]], axis=0)

    qst = [stacked_queries(t) for t in inst]
    sinks = [sink_column(t["g"]) for t in inst]
    scores = [lax.dot_general(q, window(t, 0, kc_ref, kx_ref), nt, preferred_element_type=F32)
              for q, t in zip(qst, inst)]
    pieces = []
    for s, t in zip(scores, inst):
        ts = lane_tiles(s)
        pieces.append([jnp.where(t["ok_prev"], ts[0], NEG_INF), ts[1], jnp.where(t["ok_next"], ts[2], NEG_INF)]
                      + ts[3:])
    ms = [jnp.maximum(jnp.max(functools.reduce(jnp.maximum, ps), axis=-1, keepdims=True), sk)
          for ps, sk in zip(pieces, sinks)]
    es = [[jnp.exp(p - m) for p in ps] for ps, m in zip(pieces, ms)]
    invs = [1.0 / (jnp.sum(functools.reduce(jnp.add, e), axis=-1, keepdims=True) + jnp.exp(sk - m))
            for e, sk, m in zip(es, sinks, ms)]
    pvs = [jnp.dot(jnp.concatenate(e, axis=1).astype(BF16), window(t, 1, vc_ref, vx_ref),
                   preferred_element_type=F32) * inv
           for e, t, inv in zip(es, inst, invs)]
    for t, pv in zip(inst, pvs):
        for half in range(2):
            a = pv[(2 * half) * blk:(2 * half + 1) * blk]
            b = pv[(2 * half + 1) * blk:(2 * half + 2) * blk]
            o_ref[t["rows"], (2 * t["g"] + half) * LANES:(2 * t["g"] + half + 1) * LANES] = (
                jnp.where(low, a, b).astype(BF16))


def swa_attention(q, k2, v2, kx2, vx2, sink, *, batch, seq, ctx_len):
    nb = seq // SWA_BLOCK
    ns = nb // SWA_SUB
    kvw = 2 * SWA_KV_DIM
    qspec = pl.BlockSpec((SWA_SUB * SWA_BLOCK, SWA_DIM), lambda b, i: (b * ns + i, 0))
    cur = pl.BlockSpec((SWA_SUB * SWA_BLOCK, kvw), lambda b, i: (b * ns + i, 0))
    halo = lambda off: pl.BlockSpec(
        (SWA_BLOCK, kvw), lambda b, i: (b * nb + jnp.clip(i * SWA_SUB + off, 0, nb - 1), 0))
    xspec = pl.BlockSpec((ctx_len, kvw), lambda b, i: (b, 0))
    return pl.pallas_call(
        functools.partial(_swa_attn_kernel, blocks_per_seq=nb),
        out_shape=jax.ShapeDtypeStruct(q.shape, BF16),
        grid=(batch, ns),
        in_specs=[qspec, halo(-1), cur, halo(SWA_SUB), halo(-1), cur, halo(SWA_SUB), xspec, xspec,
                  _resident(sink.shape)],
        out_specs=qspec,
        compiler_params=_cparams(2),
        name="swa_attention",
    )(q, k2, k2, k2, v2, v2, v2, kx2, vx2, sink)


CNV_RB = 32
CNV_PC = 256
CNV_STRIDE = 4


def _conv_module_kernel(xp_ref, x_ref, xn_ref, mod_ref, nw_ref, w1_ref, dw_ref, lnw_ref, lnb_ref, w2_ref,
                        o_ref, hin_ref, u_ref, c_ref, s_ref, *, tm, tiles_per_seq):
    sh, sc, gt = mod_ref[0:1, :], mod_ref[1:2, :], mod_ref[2:3, :]
    _fill_hin_ext(hin_ref, xp_ref, x_ref, xn_ref, nw_ref[...], sc, sh, tiles_per_seq, HALO, tm)
    d = D_MODEL
    n_slab = d // LANES
    for c in range(d // CNV_PC):
        cols = slice(c * CNV_PC, (c + 1) * CNV_PC)
        gcols = slice(d + c * CNV_PC, d + (c + 1) * CNV_PC)
        a1 = jnp.dot(hin_ref[...], w1_ref[:, cols], preferred_element_type=F32)
        a2 = jnp.dot(hin_ref[...], w1_ref[:, gcols], preferred_element_type=F32)
        u = a1 * jax.nn.sigmoid(a2)
        for t in range(CNV_PC // LANES):
            u_ref[c * (CNV_PC // LANES) + t] = u[:, t * LANES:(t + 1) * LANES]
    pad = CNV_WIDTH // 2
    rows = tm // CNV_STRIDE
    for j in range(n_slab):
        lanes = slice(j * LANES, (j + 1) * LANES)
        for v in range(CNV_STRIDE):
            acc = None
            for k in range(CNV_WIDTH):
                term = dw_ref[k:k + 1, lanes] * u_ref[j, pl.ds(HALO + v + k - pad, rows, stride=CNV_STRIDE), :]
                acc = term if acc is None else acc + term
            c_ref[j, pl.ds(v, rows, stride=CNV_STRIDE), :] = acc
    for rb in range(tm // CNV_RB):
        rsl = slice(rb * CNV_RB, (rb + 1) * CNV_RB)
        xs = [c_ref[j, rsl, :] for j in range(n_slab)]
        tot = xs[0]
        for xj in xs[1:]:
            tot = tot + xj
        mu = jnp.sum(tot, axis=-1, keepdims=True) * (1.0 / d)
        xcs = [xj - mu for xj in xs]
        sq = xcs[0] * xcs[0]
        for xc in xcs[1:]:
            sq = sq + xc * xc
        rstd = lax.rsqrt(jnp.sum(sq, axis=-1, keepdims=True) * (1.0 / d) + EPS)
        for j, xc in enumerate(xcs):
            lanes = slice(j * LANES, (j + 1) * LANES)
            y = xc * rstd * lnw_ref[:, lanes] + lnb_ref[:, lanes]
            s_ref[rsl, lanes] = jax.nn.silu(y).astype(BF16)
    o_ref[...] = x_ref[...] + gt * jnp.dot(s_ref[...], w2_ref[...], preferred_element_type=F32)


def conv_module(h, mod, nw, w1, dw, lnw, lnb, w2, *, seq, tm):
    n, d = h.shape
    n_tiles = n // tm
    tps = seq // tm
    prev, main, nxt = _halo_specs(tm, HALO, n_tiles, d)
    return pl.pallas_call(
        functools.partial(_conv_module_kernel, tm=tm, tiles_per_seq=tps),
        out_shape=jax.ShapeDtypeStruct((n, d), F32),
        grid=(n_tiles,),
        in_specs=[prev, main, nxt, _mod_spec(tps), _resident((1, d)), _resident(w1.shape),
                  _resident(dw.shape), _resident(lnw.shape), _resident(lnb.shape), _resident(w2.shape)],
        out_specs=pl.BlockSpec((tm, d), lambda i: (i, 0)),
        scratch_shapes=[pltpu.VMEM((tm + 2 * HALO, d), BF16),
                        pltpu.VMEM((d // LANES, tm + 2 * HALO, LANES), F32),
                        pltpu.VMEM((d // LANES, tm, LANES), F32),
                        pltpu.VMEM((tm, d), BF16)],
        compiler_params=_cparams(),
        name="conv_module",
    )(h, h, h, mod, nw, w1, dw, lnw, lnb, w2)


def _rope_tables(seq):
    n_freq = SWA_HEAD_DIM // 4
    inv_freq = ROPE_BASE ** (-jnp.arange(n_freq, dtype=F32) / n_freq)
    pos = jnp.arange(seq)
    rows = (pos // GRID_W).astype(F32)
    cols = (pos % GRID_W).astype(F32)
    ang_r = rows[:, None] * inv_freq
    ang_c = cols[:, None] * inv_freq
    ang = jnp.concatenate([ang_r, ang_r, ang_c, ang_c], axis=-1)
    sign = jnp.where((jnp.arange(SWA_HEAD_DIM) % (2 * ROPE_QUARTER)) < ROPE_QUARTER, -1.0, 1.0)
    cos = jnp.tile(jnp.cos(ang), (1, LANES // SWA_HEAD_DIM))
    sin = jnp.tile(jnp.sin(ang) * sign, (1, LANES // SWA_HEAD_DIM))
    return cos, sin


TILE_ROWS = 1024


def _tile_rows(seq):
    return min(seq, TILE_ROWS)


def kernel(x, c, ctx, c_ctx, ada_w, ada_b, norm_mix_w, norm_ffn_w, dn_w_in, dn_conv, dn_a_log, dn_dt_bias,
           dn_norm_w, dn_w_out, pool_w_grp, pool_scale, swa_w_qkv, swa_q_norm, swa_k_norm, swa_sink,
           swa_w_out, cnv_w_pw1, cnv_dw, cnv_ln_w, cnv_ln_b, cnv_w_pw2, ffn_w_gate, ffn_w_up, ffn_conv,
           ffn_conv_b, ffn_w_down):
    bsz, seq, d = x.shape
    ctx_len = ctx.shape[1]
    depth = ada_w.shape[0]
    h = x.reshape(bsz * seq, d)
    hc = ctx.reshape(bsz * ctx_len, d)
    p = dict(norm_mix_w=norm_mix_w, norm_ffn_w=norm_ffn_w, dn_w_in=dn_w_in, dn_conv=dn_conv,
             dn_a_log=dn_a_log, dn_dt_bias=dn_dt_bias, dn_norm_w=dn_norm_w, dn_w_out=dn_w_out,
             pool_w_grp=pool_w_grp, pool_scale=pool_scale, swa_w_qkv=swa_w_qkv, swa_q_norm=swa_q_norm,
             swa_k_norm=swa_k_norm, swa_sink=swa_sink, swa_w_out=swa_w_out, cnv_w_pw1=cnv_w_pw1,
             cnv_dw=cnv_dw, cnv_ln_w=cnv_ln_w, cnv_ln_b=cnv_ln_b, cnv_w_pw2=cnv_w_pw2,
             ffn_w_gate=ffn_w_gate, ffn_w_up=ffn_w_up, ffn_conv=ffn_conv, ffn_conv_b=ffn_conv_b,
             ffn_w_down=ffn_w_down)
    dims = dict(bsz=bsz, seq=seq, ctx_len=ctx_len, depth=depth)
    mods = modulation(c, c_ctx, ada_w, ada_b)
    for i in range(depth):
        mod_l, mod_c = layer_modulation(mods, i, bsz)
        h, hc = mixer_layer(i, h, hc, mod_l, mod_c, p, dims)
        h, hc = ffn_layer(i, h, hc, mod_l, mod_c, p, dims)
    return h.reshape(bsz, seq, d)


def modulation(c, c_ctx, ada_w, ada_b):
    bsz, d = c.shape
    n_cond = -(-(bsz + 1) // SUBLANES) * SUBLANES
    cond = jnp.zeros((n_cond, d), F32).at[:bsz].set(c).at[bsz].set(c_ctx)
    return ada_modulation(cond, ada_w, ada_b)


def layer_modulation(mods, i, bsz):
    d = mods.shape[-1] // 6
    mod_l = mods[i, :bsz].reshape(bsz, 6, d)
    mod_c = jnp.broadcast_to(mods[i, bsz].reshape(1, 6, d), (bsz, 6, d))
    return mod_l, mod_c


def _keeps_ctx(i, depth):
    return any((j % N_MIXERS) in CTX_READING_MIXERS for j in range(i + 1, depth))


def ffn_layer(i, h, hc, mod_l, mod_c, p, dims):
    d = h.shape[-1]
    seq, ctx_len = dims["seq"], dims["ctx_len"]
    ffn_args = (p["norm_ffn_w"][i].reshape(1, d), p["ffn_w_gate"].astype(BF16),
                p["ffn_w_up"].astype(BF16), p["ffn_conv"][i], p["ffn_conv_b"][i].reshape(1, FFN_DIM),
                p["ffn_w_down"].astype(BF16))
    h = conv_ffn(h, mod_l, *ffn_args, layer=i, seq=seq, tm=_tile_rows(seq))
    if _keeps_ctx(i, dims["depth"]):
        hc = conv_ffn(hc, mod_c, *ffn_args, layer=i, seq=ctx_len, tm=_tile_rows(ctx_len))
    return h, hc


def mixer_layer(i, h, hc, mod_l, mod_c, p, dims):
    d = h.shape[-1]
    bsz, seq, ctx_len = dims["bsz"], dims["seq"], dims["ctx_len"]
    tm_c = _tile_rows(ctx_len)
    kind, slot = i % N_MIXERS, i // N_MIXERS
    keep_ctx = _keeps_ctx(i, dims["depth"])
    nw_mix = p["norm_mix_w"][i].reshape(1, d)
    lat = dict(seq=seq, tm=_tile_rows(seq))
    cx = dict(seq=ctx_len, tm=tm_c)

    if kind == MIX_DELTANET:
        w_in = p["dn_w_in"][slot]
        wqkv = w_in[:, :3 * DN_DIM].astype(BF16)
        wz = w_in[:, 3 * DN_DIM:4 * DN_DIM].astype(BF16)
        wgt = jnp.zeros((d, LANES), F32).at[:, :4 * DN_HEADS].set(w_in[:, 4 * DN_DIM:]).astype(BF16)
        aneg = jnp.zeros((1, LANES), F32).at[0, 2 * DN_HEADS:4 * DN_HEADS].set(
            -jnp.exp(p["dn_a_log"][slot].astype(F32)).reshape(-1))
        dtb = jnp.zeros((1, LANES), F32).at[0, 2 * DN_HEADS:4 * DN_HEADS].set(
            p["dn_dt_bias"][slot].astype(F32).reshape(-1))
        hnw = p["dn_norm_w"][slot].reshape(1, DN_HEAD_DIM)
        wout = p["dn_w_out"][slot].astype(BF16)
        proj = functools.partial(dn_project, nw=nw_mix, wqkv=wqkv, cw=p["dn_conv"][slot], wz=wz, wgt=wgt,
                                 aneg=aneg, dtb=dtb)
        qc, kc, vc, zc, bgc = proj(hc, mod_c, **cx)
        ql, kl, vl, zl, bgl = proj(h, mod_l, **lat)
        s0 = jnp.zeros((bsz, 2, DN_HEADS, DN_HEAD_DIM, DN_HEAD_DIM), F32)
        ocf, ocb, s_ctx = dn_scan(qc, kc, vc, bgc, s0, batch=bsz, seq=ctx_len)
        olf, olb, _ = dn_scan(ql, kl, vl, bgl, s_ctx, batch=bsz, seq=seq)
        h = dn_readout(olf, olb, zl, h, mod_l, hnw, wout, **lat)
        if keep_ctx:
            hc = dn_readout(ocf, ocb, zc, hc, mod_c, hnw, wout, **cx)
    elif kind == MIX_POOL:
        wg = p["pool_w_grp"][slot].astype(BF16)
        scale = p["pool_scale"][slot].reshape(1, d)
        h = pool_mixer(h, mod_l, nw_mix, wg, scale, **lat)
        if keep_ctx:
            hc = pool_mixer(hc, mod_c, nw_mix, wg, scale, **cx)
    elif kind == MIX_SWA:
        assert not keep_ctx
        wq = p["swa_w_qkv"][slot]
        dup = lambda w_: jnp.repeat(w_.reshape(d, SWA_KV_HEADS, 1, SWA_HEAD_DIM), 2, axis=2).reshape(
            d, 2 * SWA_KV_DIM)
        w_all = jnp.concatenate([wq[:, :SWA_DIM], dup(wq[:, SWA_DIM:SWA_DIM + SWA_KV_DIM]),
                                 dup(wq[:, SWA_DIM + SWA_KV_DIM:])], axis=1).astype(BF16)
        hnw = jnp.concatenate([jnp.tile(p["swa_q_norm"][slot], SWA_HEADS),
                               jnp.tile(p["swa_k_norm"][slot], 2 * SWA_KV_HEADS)]).reshape(1, SWA_NORM_COLS)
        li = jnp.arange(SWA_PC)
        seg = jnp.where((li[:, None] // SWA_HEAD_DIM) == (li[None, :] // SWA_HEAD_DIM),
                        1.0 / SWA_HEAD_DIM, 0.0).astype(BF16)
        cos, sin = _rope_tables(seq)
        sink = jnp.broadcast_to(p["swa_sink"][slot].astype(F32).reshape(SWA_HEADS, 1), (SWA_HEADS, LANES))
        ql, k2, v2 = swa_qkv(h, mod_l, nw_mix, w_all, hnw, seg, cos, sin, rope=True, **lat)
        _, kx2, vx2 = swa_qkv(hc, mod_c, nw_mix, w_all, hnw, seg, cos[:tm_c], sin[:tm_c], rope=False, **cx)
        o = swa_attention(ql, k2, v2, kx2, vx2, sink, batch=bsz, seq=seq, ctx_len=ctx_len)
        h = matmul_residual(o, h, mod_l, p["swa_w_out"][slot].astype(BF16), **lat)
    else:
        args = (nw_mix, p["cnv_w_pw1"][slot].astype(BF16), p["cnv_dw"][slot], p["cnv_ln_w"][slot].reshape(1, d),
                p["cnv_ln_b"][slot].reshape(1, d), p["cnv_w_pw2"][slot].astype(BF16))
        h = conv_module(h, mod_l, *args, **lat)
        if keep_ctx:
            hc = conv_module(hc, mod_c, *args, **cx)
    return h, hc
```
